```python
import numpy as np
import jax
import jax.numpy as jnp
from jax import lax

D_MODEL = 1024
BATCH = 16
SEQ = 2048
DEPTH = 2

HEAD_DIM = 64
MIX_WIDTH = D_MODEL
N_GROUPS = 4
GROUP_WIDTH = MIX_WIDTH // N_GROUPS
NSA_HEADS = GROUP_WIDTH // HEAD_DIM
NSA_BRANCHES = 3
CMP_LEN = 32
CMP_STRIDE = 16
CMP_HIDDEN = 128
SLC_BLOCK = 64
SLC_TOP = 8
NSA_WINDOW = 512
CONV_K = 3
SWA_HEADS = GROUP_WIDTH // HEAD_DIM
SWA_KV_HEADS = 2
SWA_WINDOW = 128
POOL_WINDOWS = (2, 4, 8, 16)
POOL_DIM = GROUP_WIDTH // len(POOL_WINDOWS)
D_FF = 128 * ((8 * D_MODEL // 3 + 127) // 128)
Q_BLOCK = 128
EPS = 1e-6
NEG = -1e30
FORCE = 1e4
ATTN_SCALE = HEAD_DIM ** -0.5

SPLIT_SIZES = (
    NSA_HEADS * HEAD_DIM,
    HEAD_DIM, HEAD_DIM,
    HEAD_DIM, HEAD_DIM,
    HEAD_DIM, HEAD_DIM,
    NSA_HEADS * NSA_BRANCHES,
    GROUP_WIDTH, GROUP_WIDTH, GROUP_WIDTH,
    SWA_HEADS * HEAD_DIM,
    SWA_KV_HEADS * HEAD_DIM, SWA_KV_HEADS * HEAD_DIM,
    GROUP_WIDTH,
)
IN_WIDTH = sum(SPLIT_SIZES)

kernel_name = 'hymba_style_hybrid_nsa_conv_swa_pool'


def rms_norm(x, g):
    xf = x.astype(jnp.float32)
    y = xf * lax.rsqrt(jnp.mean(xf * xf, axis=-1, keepdims=True) + EPS)
    return (y * g.astype(jnp.float32)).astype(x.dtype)


def swiglu(h, w1, w3, w2):
    return (jax.nn.silu(h @ w1) * (h @ w3)) @ w2


def banded_attention(q, k, v, window, sinks=None):
    B, T, H, dh = q.shape
    G = k.shape[2]
    R = H // G
    nb = T // Q_BLOCK
    span = Q_BLOCK + window
    kp = jnp.pad(k, ((0, 0), (window, 0), (0, 0), (0, 0)))
    vp = jnp.pad(v, ((0, 0), (window, 0), (0, 0), (0, 0)))
    idx = np.arange(nb)[:, None] * Q_BLOCK + np.arange(span)[None, :]
    kb = kp[:, idx]
    vb = vp[:, idx]
    qb = q.reshape(B, nb, Q_BLOCK, G, R, dh)
    s = jnp.einsum('bnqgrd,bnkgd->bngrqk', qb, kb).astype(jnp.float32) * ATTN_SCALE
    qpos = np.arange(nb)[:, None, None] * Q_BLOCK + np.arange(Q_BLOCK)[None, :, None]
    kpos = idx[:, None, :] - window
    diff = qpos - kpos
    mask = (diff >= 0) & (diff < window) & (kpos >= 0)
    s = jnp.where(mask[None, :, None, None], s, NEG)
    if sinks is None:
        p = jax.nn.softmax(s, axis=-1)
    else:
        sink = sinks.astype(jnp.float32).reshape(1, 1, G, R, 1, 1)
        m = jnp.maximum(jnp.max(s, axis=-1, keepdims=True), sink)
        e = jnp.exp(s - m)
        p = e / (jnp.sum(e, axis=-1, keepdims=True) + jnp.exp(sink - m))
    o = jnp.einsum('bngrqk,bnkgd->bnqgrd', p.astype(v.dtype), vb)
    return o.reshape(B, T, H, dh)


def compress(kv, pos, w1, w2):
    B, T, dh = kv.shape
    n_c = (T - CMP_LEN) // CMP_STRIDE + 1
    idx = np.arange(n_c)[:, None] * CMP_STRIDE + np.arange(CMP_LEN)[None, :]
    blk = kv[:, idx] + pos.astype(kv.dtype)
    return jax.nn.gelu(blk.reshape(B, n_c, CMP_LEN * dh) @ w1) @ w2


def selected_attention(q, k, v, sel):
    B, T, H, dh = q.shape
    n = sel.shape[-1]
    kb = k.reshape(B, T // SLC_BLOCK, SLC_BLOCK, dh)
    vb = v.reshape(B, T // SLC_BLOCK, SLC_BLOCK, dh)
    gather = jax.vmap(lambda blocks, ix: blocks[ix])
    offs = jnp.arange(SLC_BLOCK)

    def one_block(i):
        t0 = i * Q_BLOCK
        qb = lax.dynamic_slice_in_dim(q, t0, Q_BLOCK, axis=1)
        ib = lax.dynamic_slice_in_dim(sel, t0, Q_BLOCK, axis=1)
        kg = gather(kb, ib)
        vg = gather(vb, ib)
        s = jnp.einsum('bqhd,bqnld->bqhnl', qb, kg).astype(jnp.float32) * ATTN_SCALE
        kpos = ib[..., None] * SLC_BLOCK + offs
        tpos = t0 + jnp.arange(Q_BLOCK)
        mask = (kpos <= tpos[None, :, None, None])[:, :, None]
        s = jnp.where(mask, s, NEG).reshape(B, Q_BLOCK, H, n * SLC_BLOCK)
        p = jax.nn.softmax(s, axis=-1).reshape(B, Q_BLOCK, H, n, SLC_BLOCK)
        return jnp.einsum('bqhnl,bqnld->bqhd', p.astype(vg.dtype), vg)

    out = lax.map(one_block, jnp.arange(T // Q_BLOCK))
    return jnp.moveaxis(out, 0, 1).reshape(B, T, H, dh)


def nsa(q, kc, vc, ks, vs, kw, vw, gates):
    B, T, H, dh = q.shape
    n_c = kc.shape[1]
    t = np.arange(T)
    s = jnp.einsum('bthd,bnd->bhtn', q, kc).astype(jnp.float32) * ATTN_SCALE
    c_valid = (np.arange(n_c)[None, :] * CMP_STRIDE + CMP_LEN - 1) <= t[:, None]
    p_cmp = jax.nn.softmax(jnp.where(c_valid, s, NEG), axis=-1) * c_valid
    o_cmp = jnp.einsum('bhtn,bnd->bthd', p_cmp.astype(vc.dtype), vc)
    n_sel = T // SLC_BLOCK
    ci = np.arange(n_c)[:, None] * CMP_STRIDE
    sj = np.arange(n_sel)[None, :] * SLC_BLOCK
    overlap = ((ci <= sj + SLC_BLOCK - 1) & (ci + CMP_LEN - 1 >= sj)).astype(np.float32)
    imp = jnp.einsum('bhtn,nj->btj', p_cmp, jnp.asarray(overlap))
    j = np.arange(n_sel)[None, :]
    cur = (t // SLC_BLOCK)[:, None]
    forced = (j == 0) | (j == cur) | (j == cur - 1)
    future = j * SLC_BLOCK > t[:, None]
    score = jnp.where(forced, FORCE, jnp.where(future, -1.0, imp))
    _, sel = lax.top_k(score, min(SLC_TOP, n_sel))
    o_slc = selected_attention(q, ks, vs, sel)
    o_win = banded_attention(q, kw[:, :, None], vw[:, :, None], NSA_WINDOW)
    return gates[..., 0:1] * o_cmp + gates[..., 1:2] * o_slc + gates[..., 2:3] * o_win


def causal_depthwise_conv(z, w):
    C = z.shape[-1]
    return lax.conv_general_dilated(
        z, w[:, None, :].astype(z.dtype), window_strides=(1,),
        padding=[(CONV_K - 1, 0)], dimension_numbers=('NWC', 'WIO', 'NWC'),
        feature_group_count=C)


def multiscale_pool(v, pool_w, pool_scale):
    B, T, _ = v.shape
    vf = v.astype(jnp.float32).reshape(B, T, len(POOL_WINDOWS), POOL_DIM)
    cs = jnp.cumsum(vf, axis=1)
    outs = []
    for g, w in enumerate(POOL_WINDOWS):
        c = cs[:, :, g]
        prev = jnp.pad(c, ((0, 0), (w, 0), (0, 0)))[:, :T]
        cnt = jnp.minimum(jnp.arange(1, T + 1), w).astype(jnp.float32)[None, :, None]
        outs.append((c - prev) / cnt - vf[:, :, g])
    d = jnp.stack(outs, axis=2)
    d = jnp.einsum('btgc,gcd->btgd', d, pool_w.astype(jnp.float32)).reshape(B, T, GROUP_WIDTH)
    return (d * pool_scale.astype(jnp.float32)).astype(v.dtype)


def token_mixing(h, w_in, nsa_q_norm, nsa_kc_norm, nsa_ks_norm, nsa_kw_norm,
                 cmp_pos_k, cmp_w1_k, cmp_w2_k, cmp_pos_v, cmp_w1_v, cmp_w2_v,
                 conv_w, swa_q_norm, swa_k_norm, swa_sinks, pool_w, pool_scale,
                 group_norm, w_out):
    B, T, _ = h.shape
    u = h @ w_in
    points = [int(p) for p in np.cumsum(SPLIT_SIZES)[:-1]]
    (a_q, a_kc, a_vc, a_ks, a_vs, a_kw, a_vw, a_g,
     b_b, b_c, b_x, c_q, c_k, c_v, d_v) = jnp.split(u, points, axis=-1)
    q = rms_norm(a_q.reshape(B, T, NSA_HEADS, HEAD_DIM), nsa_q_norm)
    kc = rms_norm(compress(a_kc, cmp_pos_k, cmp_w1_k, cmp_w2_k), nsa_kc_norm)
    vc = compress(a_vc, cmp_pos_v, cmp_w1_v, cmp_w2_v)
    ks = rms_norm(a_ks, nsa_ks_norm)
    kw = rms_norm(a_kw, nsa_kw_norm)
    gates = jax.nn.sigmoid(a_g.reshape(B, T, NSA_HEADS, NSA_BRANCHES))
    o_a = nsa(q, kc, vc, ks, a_vs, kw, a_vw, gates)
    o_b = b_b * causal_depthwise_conv(b_c * b_x, conv_w)
    cq = rms_norm(c_q.reshape(B, T, SWA_HEADS, HEAD_DIM), swa_q_norm)
    ck = rms_norm(c_k.reshape(B, T, SWA_KV_HEADS, HEAD_DIM), swa_k_norm)
    cv = c_v.reshape(B, T, SWA_KV_HEADS, HEAD_DIM)
    o_c = banded_attention(cq, ck, cv, SWA_WINDOW, swa_sinks)
    o_d = multiscale_pool(d_v, pool_w, pool_scale)
    y = jnp.concatenate([o_a.reshape(B, T, GROUP_WIDTH), o_b,
                         o_c.reshape(B, T, GROUP_WIDTH), o_d], axis=-1)
    y = rms_norm(y.reshape(B, T, N_GROUPS, GROUP_WIDTH),
                 group_norm.reshape(N_GROUPS, GROUP_WIDTH)).reshape(B, T, MIX_WIDTH)
    return y @ w_out


def setup_inputs(seed: int = 0) -> dict:
    key = jax.random.key(seed)
    keys = iter(jax.random.split(key, 40))
    L, D = DEPTH, D_MODEL

    def nrm(shape, scale):
        return scale * jax.random.normal(next(keys), shape, jnp.float32)

    def gain(shape):
        return 1.0 + nrm(shape, 0.1)

    return {
        'x': nrm((BATCH, SEQ, D), 1.0),
        'ffn1_norm': gain((L, D)),
        'ffn1_w1': nrm((L, D, D_FF), D ** -0.5),
        'ffn1_w3': nrm((L, D, D_FF), D ** -0.5),
        'ffn1_w2': nrm((L, D_FF, D), D_FF ** -0.5),
        'mix_norm': gain((L, D)),
        'w_in': nrm((L, D, IN_WIDTH), D ** -0.5),
        'nsa_q_norm': gain((L, HEAD_DIM)),
        'nsa_kc_norm': gain((L, HEAD_DIM)),
        'nsa_ks_norm': gain((L, HEAD_DIM)),
        'nsa_kw_norm': gain((L, HEAD_DIM)),
        'cmp_pos_k': nrm((L, CMP_LEN, HEAD_DIM), 0.1),
        'cmp_w1_k': nrm((L, CMP_LEN * HEAD_DIM, CMP_HIDDEN), (CMP_LEN * HEAD_DIM) ** -0.5),
        'cmp_w2_k': nrm((L, CMP_HIDDEN, HEAD_DIM), CMP_HIDDEN ** -0.5),
        'cmp_pos_v': nrm((L, CMP_LEN, HEAD_DIM), 0.1),
        'cmp_w1_v': nrm((L, CMP_LEN * HEAD_DIM, CMP_HIDDEN), (CMP_LEN * HEAD_DIM) ** -0.5),
        'cmp_w2_v': nrm((L, CMP_HIDDEN, HEAD_DIM), CMP_HIDDEN ** -0.5),
        'conv_w': nrm((L, CONV_K, GROUP_WIDTH), CONV_K ** -0.5),
        'swa_q_norm': gain((L, HEAD_DIM)),
        'swa_k_norm': gain((L, HEAD_DIM)),
        'swa_sinks': nrm((L, SWA_HEADS), 0.5),
        'pool_w': nrm((L, len(POOL_WINDOWS), POOL_DIM, POOL_DIM), POOL_DIM ** -0.5),
        'pool_scale': gain((L, GROUP_WIDTH)),
        'group_norm': gain((L, MIX_WIDTH)),
        'w_out': nrm((L, MIX_WIDTH, D), MIX_WIDTH ** -0.5),
        'ffn2_norm': gain((L, D)),
        'ffn2_w1': nrm((L, D, D_FF), D ** -0.5),
        'ffn2_w3': nrm((L, D, D_FF), D ** -0.5),
        'ffn2_w2': nrm((L, D_FF, D), D_FF ** -0.5),
    }


def reference(x, ffn1_norm, ffn1_w1, ffn1_w3, ffn1_w2, mix_norm, w_in,
              nsa_q_norm, nsa_kc_norm, nsa_ks_norm, nsa_kw_norm,
              cmp_pos_k, cmp_w1_k, cmp_w2_k, cmp_pos_v, cmp_w1_v, cmp_w2_v,
              conv_w, swa_q_norm, swa_k_norm, swa_sinks, pool_w, pool_scale,
              group_norm, w_out, ffn2_norm, ffn2_w1, ffn2_w3, ffn2_w2):
    for l in range(DEPTH):
        h = rms_norm(x, ffn1_norm[l])
        x = x + 0.5 * swiglu(h, ffn1_w1[l], ffn1_w3[l], ffn1_w2[l])
        h = rms_norm(x, mix_norm[l])
        x = x + token_mixing(h, w_in[l], nsa_q_norm[l], nsa_kc_norm[l], nsa_ks_norm[l],
                             nsa_kw_norm[l], cmp_pos_k[l], cmp_w1_k[l], cmp_w2_k[l],
                             cmp_pos_v[l], cmp_w1_v[l], cmp_w2_v[l], conv_w[l],
                             swa_q_norm[l], swa_k_norm[l], swa_sinks[l], pool_w[l],
                             pool_scale[l], group_norm[l], w_out[l])
        h = rms_norm(x, ffn2_norm[l])
        x = x + 0.5 * swiglu(h, ffn2_w1[l], ffn2_w3[l], ffn2_w2[l])
    return x
```

```python
import functools

import numpy as np
import jax
import jax.numpy as jnp
from jax import lax
from jax.experimental import pallas as pl
from jax.experimental.pallas import tpu as pltpu

HEAD_DIM = 64
GROUP_WIDTH = 256
N_GROUPS = 4
NSA_HEADS = 4
NSA_BRANCHES = 3
CMP_LEN = 32
CMP_STRIDE = 16
SLC_BLOCK = 64
SLC_TOP = 8
NSA_WINDOW = 512
SWA_HEADS = 4
SWA_KV_HEADS = 2
SWA_WINDOW = 128
POOL_WINDOWS = (2, 4, 8, 16)
Q_TILE = 128
EPS = 1e-6
NEG = -1e30
FORCE = 1e4
ATTN_SCALE = HEAD_DIM ** -0.5
SLC_SHIFT = SLC_BLOCK.bit_length() - 1
POOL_SHIFT = (GROUP_WIDTH // len(POOL_WINDOWS)).bit_length() - 1

LANES = 128
GATE_PAD = LANES
CONV_HALO = 8
POOL_HALO = 16
VMEM_LIMIT = 56 * 2 ** 20

N_COLS = 768
R_COLS = 256
KC_COLS = 128
B_COLS = 768
D_COLS = 256
OFF_N = 0
OFF_R = OFF_N + N_COLS
OFF_KC = OFF_R + R_COLS
OFF_G = OFF_KC + KC_COLS
OFF_B = OFF_G + GATE_PAD
OFF_D = OFF_B + B_COLS
W_COLS = OFF_D + D_COLS

SPLIT_SIZES = (256, 64, 64, 64, 64, 64, 64, 12, 256, 256, 256, 256, 128, 128, 256)

f32 = jnp.float32
bf16 = jnp.bfloat16


def _rms(x, g):
    return x * lax.rsqrt(jnp.mean(x * x, axis=-1, keepdims=True) + EPS) * g


def _dot(a, b):
    return jnp.dot(a, b, preferred_element_type=f32)


def _dot_nt(a, b):
    return lax.dot_general(a, b, (((1,), (1,)), ((), ())), preferred_element_type=f32)


def _dot_split(a, b):
    hi = a.astype(bf16)
    lo = (a - hi.astype(f32)).astype(bf16)
    return _dot(hi, b) + _dot(lo, b)


def _const_spec(shape):
    nd = len(shape)
    return pl.BlockSpec(shape, lambda *_: (0,) * nd, pipeline_mode=pl.Buffered(1))


def _ffn_kernel(*refs, with_mix):
    if with_mix:
        x_ref, y_ref, wo_ref, g_ref, w1_ref, w3_ref, w2_ref, o_ref = refs
        x = x_ref[...] + _dot(y_ref[...], wo_ref[...])
    else:
        x_ref, g_ref, w1_ref, w3_ref, w2_ref, o_ref = refs
        x = x_ref[...]
    h = _rms(x, g_ref[...]).astype(bf16)
    a = _dot(h, w1_ref[...])
    b = _dot(h, w3_ref[...])
    act = (a * jax.nn.sigmoid(a) * b).astype(bf16)
    o_ref[...] = x + 0.5 * _dot(act, w2_ref[...])


def _ffn(x, gain, w1, w3, w2, y=None, w_out=None, tm=512):
    n, d = x.shape
    dff = w1.shape[1]
    with_mix = y is not None
    row = lambda w: pl.BlockSpec((tm, w), lambda i: (i, 0))
    args, specs = [x], [row(d)]
    if with_mix:
        args += [y, w_out]
        specs += [row(y.shape[1]), _const_spec(w_out.shape)]
    args += [gain, w1, w3, w2]
    specs += [_const_spec((1, d)), _const_spec((d, dff)), _const_spec((d, dff)), _const_spec((dff, d))]
    return pl.pallas_call(
        functools.partial(_ffn_kernel, with_mix=with_mix),
        grid=(n // tm,),
        in_specs=specs,
        out_specs=row(d),
        out_shape=jax.ShapeDtypeStruct((n, d), f32),
        compiler_params=pltpu.CompilerParams(dimension_semantics=("parallel",),
                                             vmem_limit_bytes=VMEM_LIMIT),
        name="ffn_mix" if with_mix else "ffn",
    )(*args)


def _proj_kernel(x_ref, g_ref, w_ref, seg_ref, gn_ref, cw_ref, pw_ref, ps_ref,
                 qq_ref, kk_ref, vv_ref, kc_ref, gt_ref, ob_ref, od_ref,
                 zhalo_ref, vhalo_ref):
    tt = x_ref.shape[1]
    ti = pl.program_id(1)

    @pl.when(ti == 0)
    def _():
        zhalo_ref[...] = jnp.zeros_like(zhalo_ref)
        vhalo_ref[...] = jnp.zeros_like(vhalo_ref)

    h = _rms(x_ref[0], g_ref[...]).astype(bf16)

    for c in range(N_COLS // GROUP_WIDTH):
        lo, hi = OFF_N + c * GROUP_WIDTH, OFF_N + (c + 1) * GROUP_WIDTH
        u = _dot(h, w_ref[:, lo:hi])
        ss = _dot_split(u * u, seg_ref[...])
        n = (u * lax.rsqrt(ss * (1.0 / HEAD_DIM) + EPS) * gn_ref[:, lo:hi]).astype(bf16)
        if c < 2:
            qq_ref[0, :, c * GROUP_WIDTH:(c + 1) * GROUP_WIDTH] = n
        else:
            kk_ref[0] = n

    vv_ref[0] = _dot(h, w_ref[:, OFF_R:OFF_R + R_COLS]).astype(bf16)
    kc_ref[0] = _dot(h, w_ref[:, OFF_KC:OFF_KC + KC_COLS])
    gt_ref[0] = jax.nn.sigmoid(_dot(h, w_ref[:, OFF_G:OFF_G + GATE_PAD]))

    ub = _dot(h, w_ref[:, OFF_B:OFF_B + B_COLS])
    z = ub[:, GROUP_WIDTH:2 * GROUP_WIDTH] * ub[:, 2 * GROUP_WIDTH:]
    ze = jnp.concatenate([zhalo_ref[...], z], axis=0)
    z1 = pltpu.roll(ze, 1, 0)[CONV_HALO:]
    z2 = pltpu.roll(ze, 2, 0)[CONV_HALO:]
    conv = cw_ref[0:1, :] * z2 + cw_ref[1:2, :] * z1 + cw_ref[2:3, :] * z
    ob_ref[0] = ub[:, :GROUP_WIDTH] * conv
    zhalo_ref[...] = z[tt - CONV_HALO:]

    v = _dot(h, w_ref[:, OFF_D:OFF_D + D_COLS])
    s = jnp.concatenate([vhalo_ref[...], v], axis=0)
    grp = jnp.right_shift(lax.broadcasted_iota(jnp.int32, (1, D_COLS), 1), POOL_SHIFT)
    acc = None
    for k, w in enumerate(POOL_WINDOWS):
        s = s + pltpu.roll(s, w // 2, 0)
        acc = s if acc is None else jnp.where(grp >= k, s, acc)
    win = jnp.where(grp == 0, POOL_WINDOWS[0],
                    jnp.where(grp == 1, POOL_WINDOWS[1],
                              jnp.where(grp == 2, POOL_WINDOWS[2], POOL_WINDOWS[3])))
    t = ti * tt + lax.broadcasted_iota(jnp.int32, (tt, 1), 0)
    cnt = jnp.minimum(t + 1, win).astype(f32)
    d = acc[POOL_HALO:] / cnt - v
    od_ref[0] = _dot(d.astype(bf16), pw_ref[...]) * ps_ref[...]
    vhalo_ref[...] = v[tt - POOL_HALO:]


def _proj(x3, gain, w, seg, gn, conv_w, pool_w, pool_scale, tt=512):
    b, t, d = x3.shape
    tile = lambda w_, : pl.BlockSpec((1, tt, w_), lambda i, j: (i, j, 0))
    outs = [(2 * GROUP_WIDTH, bf16), (GROUP_WIDTH, bf16), (R_COLS, bf16), (KC_COLS, f32),
            (GATE_PAD, f32), (GROUP_WIDTH, f32), (D_COLS, f32)]
    return pl.pallas_call(
        _proj_kernel,
        grid=(b, t // tt),
        in_specs=[tile(d), _const_spec(gain.shape), _const_spec(w.shape), _const_spec(seg.shape),
                  _const_spec(gn.shape), _const_spec(conv_w.shape), _const_spec(pool_w.shape),
                  _const_spec(pool_scale.shape)],
        out_specs=[tile(w_) for w_, _ in outs],
        out_shape=[jax.ShapeDtypeStruct((b, t, w_), dt) for w_, dt in outs],
        scratch_shapes=[pltpu.VMEM((CONV_HALO, GROUP_WIDTH), f32), pltpu.VMEM((POOL_HALO, D_COLS), f32)],
        compiler_params=pltpu.CompilerParams(dimension_semantics=("parallel", "arbitrary"),
                                             vmem_limit_bytes=VMEM_LIMIT),
        name="proj",
    )(x3, gain, w, seg, gn, conv_w, pool_w, pool_scale)


def _compress_kernel(kr_ref, vr_ref, pk_ref, w1k_ref, w2k_ref, gk_ref, pv_ref, w1v_ref, w2v_ref,
                     kc_ref, vc_ref):
    nch = kr_ref.shape[1]
    half = kr_ref.shape[2]

    def mlp(c, pos_ref, w1_ref, w2_ref):
        ha = _dot((c + pos_ref[0:1, :]).astype(bf16), w1_ref[0:half, :])
        hb = _dot((c + pos_ref[1:2, :]).astype(bf16), w1_ref[half:2 * half, :])
        hid = jax.nn.gelu(ha + pltpu.roll(hb, nch - 1, 0))
        return _dot(hid.astype(bf16), w2_ref[...])

    kc_ref[0] = _rms(mlp(kr_ref[0], pk_ref, w1k_ref, w2k_ref), gk_ref[...]).astype(bf16)
    vc_ref[0] = mlp(vr_ref[0], pv_ref, w1v_ref, w2v_ref).astype(bf16)


def _compress(kr, vr, pk, w1k, w2k, gk, pv, w1v, w2v):
    b, nch, width = kr.shape
    blk = pl.BlockSpec((1, nch, width), lambda i: (i, 0, 0))
    out = pl.BlockSpec((1, nch, HEAD_DIM), lambda i: (i, 0, 0))
    consts = [pk, w1k, w2k, gk, pv, w1v, w2v]
    return pl.pallas_call(
        _compress_kernel,
        grid=(b,),
        in_specs=[blk, blk] + [_const_spec(c.shape) for c in consts],
        out_specs=[out, out],
        out_shape=[jax.ShapeDtypeStruct((b, nch, HEAD_DIM), bf16)] * 2,
        compiler_params=pltpu.CompilerParams(dimension_semantics=("parallel",),
                                             vmem_limit_bytes=VMEM_LIMIT),
        name="compress",
    )(kr, vr, *consts)


def _softmax_rows(s):
    m = jnp.max(s, axis=-1, keepdims=True)
    e = jnp.exp(s - m)
    return e / jnp.sum(e, axis=-1, keepdims=True)


def _attn_kernel(qq_ref, kk_ref, vv_ref, kc_ref, vc_ref, gt_ref, ob_ref, od_ref,
                 sink_ref, gg_ref, ov_ref, ex_ref, y_ref):
    tq = qq_ref.shape[1]
    t_len = kk_ref.shape[1]
    nch = kc_ref.shape[1]
    n_sel = ov_ref.shape[1]
    rows = NSA_HEADS * tq
    t0 = pl.program_id(1) * tq

    q = qq_ref[0]
    qa = jnp.concatenate([q[:, h * HEAD_DIM:(h + 1) * HEAD_DIM] for h in range(NSA_HEADS)], axis=0)

    def qpos(shape):
        return t0 + (lax.broadcasted_iota(jnp.int32, shape, 0) & (tq - 1))

    sc = _dot_nt(qa, kc_ref[0])
    ncol = lax.broadcasted_iota(jnp.int32, (rows, nch), 1)
    c_valid = (ncol * CMP_STRIDE + (CMP_LEN - 1) <= qpos((rows, nch))) & (ncol < nch - 1)
    p_cmp = jnp.where(c_valid, _softmax_rows(jnp.where(c_valid, sc, NEG)), 0.0)
    o_cmp = _dot(p_cmp.astype(bf16), vc_ref[0])
    p_sum = p_cmp[0:tq]
    for h in range(1, NSA_HEADS):
        p_sum = p_sum + p_cmp[h * tq:(h + 1) * tq]
    imp = _dot_split(p_sum, ov_ref[...])

    j = lax.broadcasted_iota(jnp.int32, (tq, n_sel), 1)
    t = t0 + lax.broadcasted_iota(jnp.int32, (tq, n_sel), 0)
    cur = jnp.right_shift(t, SLC_SHIFT)
    forced = (j == 0) | (j == cur) | (j == cur - 1)
    score = jnp.where(forced, FORCE, jnp.where(j * SLC_BLOCK > t, -1.0, imp))
    rank = jnp.zeros((tq, n_sel), f32)
    for jp in range(n_sel):
        col = score[:, jp:jp + 1]
        ahead = (col > score) | ((col == score) & (j > jp))
        rank = rank + jnp.where(ahead, 1.0, 0.0)
    sel = jnp.where(rank < float(min(SLC_TOP, n_sel)), 1.0, 0.0).astype(bf16)
    key_sel = _dot(sel, ex_ref[...])
    key_sel = jnp.concatenate([key_sel] * NSA_HEADS, axis=0)

    s = _dot_nt(qa, kk_ref[0, :, 0:HEAD_DIM])
    kpos = lax.broadcasted_iota(jnp.int32, (rows, t_len), 1)
    ok = (key_sel > 0.5) & (kpos <= qpos((rows, t_len)))
    o_slc = _dot(_softmax_rows(jnp.where(ok, s, NEG)).astype(bf16), vv_ref[0, :, 0:HEAD_DIM])

    span = min(NSA_WINDOW + tq, t_len)
    s0 = pl.multiple_of(jnp.maximum(t0 - NSA_WINDOW, 0), tq)
    s = _dot_nt(qa, kk_ref[0, pl.ds(s0, span), HEAD_DIM:2 * HEAD_DIM])
    diff = qpos((rows, span)) - (s0 + lax.broadcasted_iota(jnp.int32, (rows, span), 1))
    ok = (diff >= 0) & (diff < NSA_WINDOW)
    o_win = _dot(_softmax_rows(jnp.where(ok, s, NEG)).astype(bf16),
                 vv_ref[0, pl.ds(s0, span), HEAD_DIM:2 * HEAD_DIM])

    gt = gt_ref[0]
    o_a = []
    for h in range(NSA_HEADS):
        r = slice(h * tq, (h + 1) * tq)
        g = lambda k: gt[:, NSA_BRANCHES * h + k:NSA_BRANCHES * h + k + 1]
        o_a.append(g(0) * o_cmp[r] + g(1) * o_slc[r] + g(2) * o_win[r])
    o_a = jnp.concatenate(o_a, axis=1)

    rep = SWA_HEADS // SWA_KV_HEADS
    span = min(SWA_WINDOW + tq, t_len)
    s0 = pl.multiple_of(jnp.maximum(t0 - SWA_WINDOW, 0), tq)
    cq0 = NSA_HEADS * HEAD_DIM
    ck0 = 2 * HEAD_DIM
    o_c = []
    for g in range(SWA_KV_HEADS):
        heads = range(g * rep, (g + 1) * rep)
        qg = jnp.concatenate([q[:, cq0 + h * HEAD_DIM:cq0 + (h + 1) * HEAD_DIM] for h in heads], axis=0)
        cols = slice(ck0 + g * HEAD_DIM, ck0 + (g + 1) * HEAD_DIM)
        s = _dot_nt(qg, kk_ref[0, pl.ds(s0, span), cols])
        shape = (rep * tq, span)
        diff = qpos(shape) - (s0 + lax.broadcasted_iota(jnp.int32, shape, 1))
        s = jnp.where((diff >= 0) & (diff < SWA_WINDOW), s, NEG)
        sink = jnp.concatenate(
            [jnp.broadcast_to(sink_ref[:, h:h + 1], (tq, 1)) for h in heads], axis=0)
        m = jnp.maximum(jnp.max(s, axis=-1, keepdims=True), sink)
        e = jnp.exp(s - m)
        p = e / (jnp.sum(e, axis=-1, keepdims=True) + jnp.exp(sink - m))
        o = _dot(p.astype(bf16), vv_ref[0, pl.ds(s0, span), cols])
        o_c += [o[k * tq:(k + 1) * tq] for k in range(rep)]
    o_c = jnp.concatenate(o_c, axis=1)

    for gi, o in enumerate((o_a, ob_ref[0], o_c, od_ref[0])):
        cols = slice(gi * GROUP_WIDTH, (gi + 1) * GROUP_WIDTH)
        y_ref[0, :, cols] = _rms(o, gg_ref[:, cols]).astype(bf16)


def _attn(qq, kk, vv, kc, vc, gt, ob, od, sinks, gg, ov, ex):
    b, t, _ = qq.shape
    tq = Q_TILE
    tile = lambda a: pl.BlockSpec((1, tq, a.shape[2]), lambda i, j: (i, j, 0))
    full = lambda a: pl.BlockSpec((1,) + a.shape[1:], lambda i, j: (i, 0, 0))
    width = N_GROUPS * GROUP_WIDTH
    return pl.pallas_call(
        _attn_kernel,
        grid=(b, t // tq),
        in_specs=[tile(qq), full(kk), full(vv), full(kc), full(vc), tile(gt), tile(ob), tile(od),
                  _const_spec(sinks.shape), _const_spec(gg.shape), _const_spec(ov.shape),
                  _const_spec(ex.shape)],
        out_specs=pl.BlockSpec((1, tq, width), lambda i, j: (i, j, 0)),
        out_shape=jax.ShapeDtypeStruct((b, t, width), bf16),
        compiler_params=pltpu.CompilerParams(dimension_semantics=("parallel", "parallel"),
                                             vmem_limit_bytes=VMEM_LIMIT),
        name="attn",
    )(qq, kk, vv, kc, vc, gt, ob, od, sinks, gg, ov, ex)


def _selection_constants(t_len):
    n_c = (t_len - CMP_LEN) // CMP_STRIDE + 1
    nch = t_len // CMP_STRIDE
    n_sel = t_len // SLC_BLOCK
    ci = np.arange(nch)[:, None] * CMP_STRIDE
    sj = np.arange(n_sel)[None, :] * SLC_BLOCK
    overlap = (ci <= sj + SLC_BLOCK - 1) & (ci + CMP_LEN - 1 >= sj) & (np.arange(nch)[:, None] < n_c)
    expand = np.arange(t_len)[None, :] // SLC_BLOCK == np.arange(n_sel)[:, None]
    return jnp.asarray(overlap, bf16), jnp.asarray(expand, bf16)


def _mixer_weights(w_in, nsa_q, nsa_ks, nsa_kw, swa_q, swa_k):
    offs = np.concatenate([[0], np.cumsum(SPLIT_SIZES)])
    (a_q, a_kc, a_vc, a_ks, a_vs, a_kw, a_vw, a_g,
     b_b, b_c, b_x, c_q, c_k, c_v, d_v) = [w_in[:, offs[k]:offs[k + 1]] for k in range(len(SPLIT_SIZES))]
    a_g = jnp.pad(a_g, ((0, 0), (0, GATE_PAD - a_g.shape[1])))
    w = jnp.concatenate([a_q, c_q, a_ks, a_kw, c_k, a_vs, a_vw, c_v, a_kc, a_vc, a_g,
                         b_b, b_c, b_x, d_v], axis=1).astype(bf16)
    gn = jnp.concatenate([jnp.tile(nsa_q * ATTN_SCALE, NSA_HEADS), jnp.tile(swa_q * ATTN_SCALE, SWA_HEADS),
                          nsa_ks, nsa_kw, jnp.tile(swa_k, SWA_KV_HEADS)])[None, :]
    return w, gn


def _block_diag(blocks):
    n, r, c = blocks.shape
    eye = jnp.eye(n, dtype=blocks.dtype)
    return (eye[:, None, :, None] * blocks[:, :, None, :]).reshape(n * r, n * c)


def kernel(x, ffn1_norm, ffn1_w1, ffn1_w3, ffn1_w2, mix_norm, w_in, nsa_q_norm, nsa_kc_norm, nsa_ks_norm, nsa_kw_norm, cmp_pos_k, cmp_w1_k, cmp_w2_k, cmp_pos_v, cmp_w1_v, cmp_w2_v, conv_w, swa_q_norm, swa_k_norm, swa_sinks, pool_w, pool_scale, group_norm, w_out, ffn2_norm, ffn2_w1, ffn2_w3, ffn2_w2):
    b, t, d = x.shape
    depth = w_in.shape[0]
    nch = t // CMP_STRIDE
    overlap, expand = _selection_constants(t)
    seg = jnp.asarray(np.kron(np.eye(GROUP_WIDTH // HEAD_DIM), np.ones((HEAD_DIM, HEAD_DIM))), bf16)
    row = lambda v: v[None, :].astype(f32)
    cb = lambda v: v.astype(bf16)

    x = x.reshape(b * t, d)
    for l in range(depth):
        x = _ffn(x, row(ffn1_norm[l]), cb(ffn1_w1[l]), cb(ffn1_w3[l]), cb(ffn1_w2[l]))

        w, gn = _mixer_weights(w_in[l], nsa_q_norm[l], nsa_ks_norm[l], nsa_kw_norm[l],
                               swa_q_norm[l], swa_k_norm[l])
        qq, kk, vv, kvc, gt, ob, od = _proj(
            x.reshape(b, t, d), row(mix_norm[l]), w, seg, gn, conv_w[l],
            cb(_block_diag(pool_w[l])), row(pool_scale[l]))

        chunks = lambda a: a.reshape(b, nch, CMP_STRIDE * HEAD_DIM)
        pos = lambda p: p.reshape(CMP_LEN // CMP_STRIDE, CMP_STRIDE * HEAD_DIM)
        kc, vc = _compress(chunks(kvc[..., :HEAD_DIM]), chunks(kvc[..., HEAD_DIM:]),
                           pos(cmp_pos_k[l]), cb(cmp_w1_k[l]), cb(cmp_w2_k[l]), row(nsa_kc_norm[l]),
                           pos(cmp_pos_v[l]), cb(cmp_w1_v[l]), cb(cmp_w2_v[l]))

        y = _attn(qq, kk, vv, kc, vc, gt, ob, od, row(swa_sinks[l]), row(group_norm[l]), overlap, expand)

        x = _ffn(x, row(ffn2_norm[l]), cb(ffn2_w1[l]), cb(ffn2_w3[l]), cb(ffn2_w2[l]),
                 y=y.reshape(b * t, d), w_out=cb(w_out[l]))
    return x.reshape(b, t, d)
```

```python
import functools

import numpy as np
import jax
import jax.numpy as jnp
from jax import lax
from jax.experimental import pallas as pl
from jax.experimental.pallas import tpu as pltpu

HEAD_DIM = 64
GROUP_WIDTH = 256
N_GROUPS = 4
NSA_HEADS = 4
NSA_BRANCHES = 3
CMP_LEN = 32
CMP_STRIDE = 16
SLC_BLOCK = 64
SLC_TOP = 8
NSA_WINDOW = 512
SWA_HEADS = 4
SWA_KV_HEADS = 2
SWA_WINDOW = 128
POOL_WINDOWS = (2, 4, 8, 16)
Q_TILE = 128
EPS = 1e-6
NEG = -1e30
FORCE = 1e4
ATTN_SCALE = HEAD_DIM ** -0.5
LOG2E = float(np.log2(np.e))
SLC_SEG = 512
SLC_SHIFT = SLC_BLOCK.bit_length() - 1
POOL_SHIFT = (GROUP_WIDTH // len(POOL_WINDOWS)).bit_length() - 1

LANES = 128
SUBLANES = 8
CONV_HALO = 8
POOL_HALO = 16
VMEM_LIMIT = 56 * 2 ** 20

QT_ROWS = 512
VT_ROWS = 512
VHEAD = 2 * HEAD_DIM
GT_ROWS = 2 * SUBLANES
T_ROWS = QT_ROWS + VT_ROWS + GT_ROWS
K_COLS = 256
KC_COLS = 128
B_COLS = 768
D_COLS = 256
OFF_K = 0
OFF_KC = OFF_K + K_COLS
OFF_B = OFF_KC + KC_COLS
OFF_D = OFF_B + B_COLS
W_COLS = OFF_D + D_COLS

SPLIT_SIZES = (256, 64, 64, 64, 64, 64, 64, 12, 256, 256, 256, 256, 128, 128, 256)

f32 = jnp.float32
bf16 = jnp.bfloat16


def _rms(x, g):
    return x * lax.rsqrt(jnp.mean(x * x, axis=-1, keepdims=True) + EPS) * g


def _dot(a, b):
    return jnp.dot(a, b, preferred_element_type=f32)


def _dot_nt(a, b):
    return lax.dot_general(a, b, (((1,), (1,)), ((), ())), preferred_element_type=f32)


def _split(a):
    hi = a.astype(bf16)
    return hi, (a - hi.astype(f32)).astype(bf16)


def _const_spec(shape):
    nd = len(shape)
    return pl.BlockSpec(shape, lambda *_: (0,) * nd, pipeline_mode=pl.Buffered(1))


def _ffn_kernel(*refs, with_mix):
    if with_mix:
        x_ref, ya_ref, yb_ref, woa_ref, wob_ref, g_ref, w1_ref, w3_ref, w2_ref, o_ref = refs
        x = x_ref[...] + _dot(ya_ref[...], woa_ref[...]) + _dot(yb_ref[...], wob_ref[...])
    else:
        x_ref, g_ref, w1_ref, w3_ref, w2_ref, o_ref = refs
        x = x_ref[...]
    h = _rms(x, g_ref[...]).astype(bf16)
    a = _dot(h, w1_ref[...])
    b = _dot(h, w3_ref[...])
    act = (a * jax.nn.sigmoid(a) * b).astype(bf16)
    o_ref[...] = x + 0.5 * _dot(act, w2_ref[...])


def _ffn(x, gain, w1, w3, w2, mix=None, tm=512):
    n, d = x.shape
    dff = w1.shape[1]
    row = lambda w: pl.BlockSpec((tm, w), lambda i: (i, 0))
    args, specs = [x], [row(d)]
    if mix is not None:
        ya, yb, woa, wob = mix
        args += [ya, yb, woa, wob]
        specs += [row(ya.shape[1]), row(yb.shape[1]), _const_spec(woa.shape), _const_spec(wob.shape)]
    args += [gain, w1, w3, w2]
    specs += [_const_spec((1, d)), _const_spec((d, dff)), _const_spec((d, dff)), _const_spec((dff, d))]
    return pl.pallas_call(
        functools.partial(_ffn_kernel, with_mix=mix is not None),
        grid=(n // tm,),
        in_specs=specs,
        out_specs=row(d),
        out_shape=jax.ShapeDtypeStruct((n, d), f32),
        compiler_params=pltpu.CompilerParams(dimension_semantics=("parallel",),
                                             vmem_limit_bytes=VMEM_LIMIT),
        name="ffn" if mix is None else "ffn_mix",
    )(*args)


def _proj_kernel(x_ref, g_ref, wt_ref, w_ref, seg_ref, gq_ref, gk_ref, cw_ref, pw_ref, ps_ref, gbd_ref,
                 qt_ref, kk_ref, vt_ref, kc_ref, gt_ref, ybd_ref,
                 zhalo_ref, vhalo_ref):
    tt = x_ref.shape[1]
    ti = pl.program_id(1)

    @pl.when(ti == 0)
    def _():
        zhalo_ref[...] = jnp.zeros_like(zhalo_ref)
        vhalo_ref[...] = jnp.zeros_like(vhalo_ref)

    h = _rms(x_ref[0], g_ref[...]).astype(bf16)

    ut = _dot_nt(wt_ref[...], h)
    for hd in range(QT_ROWS // HEAD_DIM):
        rows = slice(hd * HEAD_DIM, (hd + 1) * HEAD_DIM)
        u = ut[rows]
        ms = jnp.mean(u * u, axis=0, keepdims=True)
        qt_ref[0, rows, :] = (u * lax.rsqrt(ms + EPS) * gq_ref[rows, :]).astype(bf16)
    ones_rows = jnp.where((lax.broadcasted_iota(jnp.int32, (VT_ROWS, 1), 0) & HEAD_DIM) != 0, 1.0, 0.0)
    vt = (ut[QT_ROWS:QT_ROWS + VT_ROWS] + ones_rows).astype(bf16)
    for c in range(tt // LANES):
        vt_ref[0, c] = vt[:, c * LANES:(c + 1) * LANES]
    gt_ref[0] = jax.nn.sigmoid(ut[QT_ROWS + VT_ROWS:])

    u = _dot(h, w_ref[:, OFF_K:OFF_K + K_COLS])
    hi, lo = _split(u * u)
    ss = _dot(hi, seg_ref[...]) + _dot(lo, seg_ref[...])
    kk_ref[0] = (u * lax.rsqrt(ss * (1.0 / HEAD_DIM) + EPS) * gk_ref[...]).astype(bf16)
    kc_ref[0] = _dot(h, w_ref[:, OFF_KC:OFF_KC + KC_COLS])

    ub = _dot(h, w_ref[:, OFF_B:OFF_B + B_COLS])
    z = ub[:, GROUP_WIDTH:2 * GROUP_WIDTH] * ub[:, 2 * GROUP_WIDTH:]
    ze = jnp.concatenate([zhalo_ref[...], z], axis=0)
    z1 = pltpu.roll(ze, 1, 0)[CONV_HALO:]
    z2 = pltpu.roll(ze, 2, 0)[CONV_HALO:]
    conv = cw_ref[0:1, :] * z2 + cw_ref[1:2, :] * z1 + cw_ref[2:3, :] * z
    o_b = ub[:, :GROUP_WIDTH] * conv
    zhalo_ref[...] = z[tt - CONV_HALO:]

    v = _dot(h, w_ref[:, OFF_D:OFF_D + D_COLS])
    s = jnp.concatenate([vhalo_ref[...], v], axis=0)
    grp = jnp.right_shift(lax.broadcasted_iota(jnp.int32, (1, D_COLS), 1), POOL_SHIFT)
    acc = None
    for k, w in enumerate(POOL_WINDOWS):
        s = s + pltpu.roll(s, w // 2, 0)
        acc = s if acc is None else jnp.where(grp >= k, s, acc)
    win = jnp.where(grp == 0, POOL_WINDOWS[0],
                    jnp.where(grp == 1, POOL_WINDOWS[1],
                              jnp.where(grp == 2, POOL_WINDOWS[2], POOL_WINDOWS[3])))
    t = ti * tt + lax.broadcasted_iota(jnp.int32, (tt, 1), 0)
    cnt = jnp.minimum(t + 1, win).astype(f32)
    d = acc[POOL_HALO:] / cnt - v
    o_d = _dot(d.astype(bf16), pw_ref[...]) * ps_ref[...]
    vhalo_ref[...] = v[tt - POOL_HALO:]

    ybd_ref[0, :, :GROUP_WIDTH] = _rms(o_b, gbd_ref[:, :GROUP_WIDTH]).astype(bf16)
    ybd_ref[0, :, GROUP_WIDTH:] = _rms(o_d, gbd_ref[:, GROUP_WIDTH:]).astype(bf16)


def _proj(x3, gain, wt, w, seg, gq, gk, conv_w, pool_w, pool_scale, gbd, tt=512):
    b, t, d = x3.shape
    tok = lambda w_: pl.BlockSpec((1, tt, w_), lambda i, j: (i, j, 0))
    feat = lambda r: pl.BlockSpec((1, r, tt), lambda i, j: (i, 0, j))
    consts = [gain, wt, w, seg, gq, gk, conv_w, pool_w, pool_scale, gbd]
    return pl.pallas_call(
        _proj_kernel,
        grid=(b, t // tt),
        in_specs=[tok(d)] + [_const_spec(c.shape) for c in consts],
        out_specs=[feat(QT_ROWS), tok(K_COLS),
                   pl.BlockSpec((1, tt // LANES, VT_ROWS, LANES), lambda i, j: (i, j, 0, 0)),
                   tok(KC_COLS), feat(GT_ROWS), tok(2 * GROUP_WIDTH)],
        out_shape=[jax.ShapeDtypeStruct((b, QT_ROWS, t), bf16),
                   jax.ShapeDtypeStruct((b, t, K_COLS), bf16),
                   jax.ShapeDtypeStruct((b, t // LANES, VT_ROWS, LANES), bf16),
                   jax.ShapeDtypeStruct((b, t, KC_COLS), f32),
                   jax.ShapeDtypeStruct((b, GT_ROWS, t), f32),
                   jax.ShapeDtypeStruct((b, t, 2 * GROUP_WIDTH), bf16)],
        scratch_shapes=[pltpu.VMEM((CONV_HALO, GROUP_WIDTH), f32), pltpu.VMEM((POOL_HALO, D_COLS), f32)],
        compiler_params=pltpu.CompilerParams(dimension_semantics=("parallel", "arbitrary"),
                                             vmem_limit_bytes=VMEM_LIMIT),
        name="proj",
    )(x3, *consts)


def _compress_kernel(kr_ref, vr_ref, pk_ref, w1k_ref, w2k_ref, gk_ref, pv_ref, w1v_ref, w2vt_ref,
                     kc_ref, vct_ref):
    nch = kr_ref.shape[1]
    half = kr_ref.shape[2]

    def hidden(c, pos_ref, w1_ref):
        ha = _dot((c + pos_ref[0:1, :]).astype(bf16), w1_ref[0:half, :])
        hb = _dot((c + pos_ref[1:2, :]).astype(bf16), w1_ref[half:2 * half, :])
        return jax.nn.gelu(ha + pltpu.roll(hb, nch - 1, 0)).astype(bf16)

    kc_ref[0] = _rms(_dot(hidden(kr_ref[0], pk_ref, w1k_ref), w2k_ref[...]), gk_ref[...]).astype(bf16)
    vct_ref[0] = _dot_nt(w2vt_ref[...], hidden(vr_ref[0], pv_ref, w1v_ref)).astype(bf16)


def _compress(kr, vr, pk, w1k, w2k, gk, pv, w1v, w2vt):
    b, nch, width = kr.shape
    blk = pl.BlockSpec((1, nch, width), lambda i: (i, 0, 0))
    consts = [pk, w1k, w2k, gk, pv, w1v, w2vt]
    return pl.pallas_call(
        _compress_kernel,
        grid=(b,),
        in_specs=[blk, blk] + [_const_spec(c.shape) for c in consts],
        out_specs=[pl.BlockSpec((1, nch, HEAD_DIM), lambda i: (i, 0, 0)),
                   pl.BlockSpec((1, HEAD_DIM, nch), lambda i: (i, 0, 0))],
        out_shape=[jax.ShapeDtypeStruct((b, nch, HEAD_DIM), bf16),
                   jax.ShapeDtypeStruct((b, HEAD_DIM, nch), bf16)],
        compiler_params=pltpu.CompilerParams(dimension_semantics=("parallel",),
                                             vmem_limit_bytes=VMEM_LIMIT),
        name="compress",
    )(kr, vr, *consts)


def _attn_kernel(qt_ref, kk_ref, vt_ref, kc_ref, vct_ref, gt_ref, sink_ref, gac_ref, ovt_ref,
                 y_ref, negsel_ref):
    tq = qt_ref.shape[2]
    t_len = kk_ref.shape[1]
    nch = kc_ref.shape[1]
    n_sel = ovt_ref.shape[0]
    seg = min(SLC_SEG, t_len)
    t0 = pl.program_id(1) * tq
    lane_head = lambda a, h: a[:, h * tq:(h + 1) * tq]

    def stack_heads(first, count):
        return jnp.concatenate(
            [qt_ref[0, (first + h) * HEAD_DIM:(first + h + 1) * HEAD_DIM, :] for h in range(count)], axis=1)

    def values(chunk0, n_chunks, head):
        rows = slice(head * VHEAD, (head + 1) * VHEAD)
        return jnp.concatenate([vt_ref[0, chunk0 + c, rows, :] for c in range(n_chunks)], axis=1)

    def finish(acc, extra=None):
        den = acc[HEAD_DIM:]
        if extra is not None:
            den = den + extra
        return acc[:HEAD_DIM] / den

    def band_bias(s0, span, window):
        diff = (t0 + lax.broadcasted_iota(jnp.int32, (span, tq), 1)
                - (s0 + lax.broadcasted_iota(jnp.int32, (span, tq), 0)))
        return jnp.where((diff >= 0) & (diff < window), 0.0, NEG)

    qa = stack_heads(0, NSA_HEADS)

    sc = _dot(kc_ref[0], qa)
    n_i = lax.broadcasted_iota(jnp.int32, sc.shape, 0)
    t_i = t0 + (lax.broadcasted_iota(jnp.int32, sc.shape, 1) & (tq - 1))
    c_valid = (n_i * CMP_STRIDE + (CMP_LEN - 1) <= t_i) & (n_i < nch - 1)
    sc = jnp.where(c_valid, sc, NEG)
    e = jnp.exp2(sc - jnp.max(sc, axis=0, keepdims=True))
    p_cmp = jnp.where(c_valid, e / jnp.sum(e, axis=0, keepdims=True), 0.0)
    o_cmp = _dot(vct_ref[0], p_cmp.astype(bf16))
    p_sum = lane_head(p_cmp, 0)
    for h in range(1, NSA_HEADS):
        p_sum = p_sum + lane_head(p_cmp, h)
    p_hi, p_lo = _split(p_sum)
    imp = _dot(ovt_ref[...], p_hi) + _dot(ovt_ref[...], p_lo)

    j = lax.broadcasted_iota(jnp.int32, (n_sel, tq), 0)
    t = t0 + lax.broadcasted_iota(jnp.int32, (n_sel, tq), 1)
    cur = jnp.right_shift(t, SLC_SHIFT)
    forced = (j == 0) | (j == cur) | (j == cur - 1)
    score = jnp.where(forced, FORCE, jnp.where(j * SLC_BLOCK > t, -1.0, imp))
    rank = jnp.zeros((n_sel, tq), f32)
    for jp in range(n_sel):
        other = score[jp:jp + 1, :]
        ahead = (other > score) | ((other == score) & (j > jp))
        rank = rank + jnp.where(ahead, 1.0, 0.0)
    negsel_ref[...] = jnp.where(rank < float(min(SLC_TOP, n_sel)), 0.0, NEG)

    def seg_body(k, carry):
        m, acc = carry
        base = pl.multiple_of(k * seg, seg)
        s = _dot(kk_ref[0, pl.ds(base, seg), 0:HEAD_DIM], qa)
        bias = jnp.concatenate(
            [jnp.broadcast_to(negsel_ref[pl.ds(k * (seg // SLC_BLOCK) + b, 1), :], (SLC_BLOCK, tq))
             for b in range(seg // SLC_BLOCK)], axis=0)
        causal = (base + lax.broadcasted_iota(jnp.int32, (seg, tq), 0)
                  <= t0 + lax.broadcasted_iota(jnp.int32, (seg, tq), 1))
        bias = jnp.where(causal, bias, NEG)
        z = jnp.concatenate([lane_head(s, h) + bias for h in range(NSA_HEADS)], axis=1)
        m_new = jnp.maximum(m, jnp.max(z, axis=0, keepdims=True))
        p = jnp.exp2(z - m_new).astype(bf16)
        pv = _dot(values(k * (seg // LANES), seg // LANES, 0), p)
        return m_new, jnp.exp2(m - m_new) * acc + pv

    n_seg = (t0 + tq - 1) // seg + 1
    _, acc = lax.fori_loop(0, n_seg, seg_body,
                           (jnp.full((1, NSA_HEADS * tq), NEG, f32), jnp.zeros((VHEAD, NSA_HEADS * tq), f32)))
    o_slc = finish(acc)

    span = min(NSA_WINDOW + tq, t_len)
    s0 = pl.multiple_of(jnp.maximum(t0 - NSA_WINDOW, 0), tq)
    bias = band_bias(s0, span, NSA_WINDOW)
    s = _dot(kk_ref[0, pl.ds(s0, span), HEAD_DIM:2 * HEAD_DIM], qa)
    z = jnp.concatenate([lane_head(s, h) + bias for h in range(NSA_HEADS)], axis=1)
    p = jnp.exp2(z - jnp.max(z, axis=0, keepdims=True)).astype(bf16)
    o_win = finish(_dot(values(s0 // LANES, span // LANES, 1), p))

    gt = gt_ref[0]
    o_a = []
    for h in range(NSA_HEADS):
        g = lambda k: gt[NSA_BRANCHES * h + k:NSA_BRANCHES * h + k + 1, :]
        o_a.append(g(0) * lane_head(o_cmp, h) + g(1) * lane_head(o_slc, h) + g(2) * lane_head(o_win, h))
    o_a = jnp.concatenate(o_a, axis=0)

    rep = SWA_HEADS // SWA_KV_HEADS
    span = min(SWA_WINDOW + tq, t_len)
    s0 = pl.multiple_of(jnp.maximum(t0 - SWA_WINDOW, 0), tq)
    bias = band_bias(s0, span, SWA_WINDOW)
    o_c = []
    for g in range(SWA_KV_HEADS):
        qg = stack_heads(NSA_HEADS + g * rep, rep)
        s = _dot(kk_ref[0, pl.ds(s0, span), (2 + g) * HEAD_DIM:(3 + g) * HEAD_DIM], qg)
        ps, sink_terms = [], []
        for r in range(rep):
            sink = sink_ref[:, g * rep + r:g * rep + r + 1]
            z = lane_head(s, r) + bias
            m = jnp.maximum(jnp.max(z, axis=0, keepdims=True), sink)
            ps.append(jnp.exp2(z - m).astype(bf16))
            sink_terms.append(jnp.exp2(sink - m))
        acc = _dot(values(s0 // LANES, span // LANES, 2 + g), jnp.concatenate(ps, axis=1))
        o_c += [finish(lane_head(acc, r), sink_terms[r]) for r in range(rep)]
    o_c = jnp.concatenate(o_c, axis=0)

    for gi, o in enumerate((o_a, o_c)):
        cols = slice(gi * GROUP_WIDTH, (gi + 1) * GROUP_WIDTH)
        y_ref[0, :, cols] = _rms(o.T, gac_ref[:, cols]).astype(bf16)


def _attn(qt, kk, vt, kc, vct, gt, sinks, gac, ovt):
    b, _, t = qt.shape
    tq = Q_TILE
    full = lambda a: pl.BlockSpec((1,) + a.shape[1:], lambda i, j: (i,) + (0,) * (a.ndim - 1))
    feat = lambda a: pl.BlockSpec((1, a.shape[1], tq), lambda i, j: (i, 0, j))
    width = 2 * GROUP_WIDTH
    return pl.pallas_call(
        _attn_kernel,
        grid=(b, t // tq),
        in_specs=[feat(qt), full(kk), full(vt), full(kc), full(vct), feat(gt),
                  _const_spec(sinks.shape), _const_spec(gac.shape), _const_spec(ovt.shape)],
        out_specs=pl.BlockSpec((1, tq, width), lambda i, j: (i, j, 0)),
        out_shape=jax.ShapeDtypeStruct((b, t, width), bf16),
        scratch_shapes=[pltpu.VMEM((ovt.shape[0], tq), f32)],
        compiler_params=pltpu.CompilerParams(dimension_semantics=("parallel", "parallel"),
                                             vmem_limit_bytes=VMEM_LIMIT),
        name="attn",
    )(qt, kk, vt, kc, vct, gt, sinks, gac, ovt)


def _overlap_t(t_len):
    n_c = (t_len - CMP_LEN) // CMP_STRIDE + 1
    nch = t_len // CMP_STRIDE
    ci = np.arange(nch)[None, :] * CMP_STRIDE
    sj = np.arange(t_len // SLC_BLOCK)[:, None] * SLC_BLOCK
    ov = (ci <= sj + SLC_BLOCK - 1) & (ci + CMP_LEN - 1 >= sj) & (np.arange(nch)[None, :] < n_c)
    return jnp.asarray(ov, bf16)


def _mixer_weights(w_in, nsa_q, nsa_ks, nsa_kw, swa_q, swa_k):
    offs = np.concatenate([[0], np.cumsum(SPLIT_SIZES)])
    (a_q, a_kc, a_vc, a_ks, a_vs, a_kw, a_vw, a_g,
     b_b, b_c, b_x, c_q, c_k, c_v, d_v) = [w_in[:, offs[k]:offs[k + 1]] for k in range(len(SPLIT_SIZES))]
    a_g = jnp.pad(a_g, ((0, 0), (0, GT_ROWS - a_g.shape[1])))
    z = jnp.zeros_like(a_vs)
    wt = jnp.concatenate([a_q, c_q, a_vs, z, a_vw, z, c_v[:, :HEAD_DIM], z, c_v[:, HEAD_DIM:], z, a_g],
                         axis=1).T.astype(bf16)
    w = jnp.concatenate([a_ks, a_kw, c_k, a_kc, a_vc, b_b, b_c, b_x, d_v], axis=1).astype(bf16)
    q_scale = ATTN_SCALE * LOG2E
    gq = jnp.concatenate([jnp.tile(nsa_q * q_scale, NSA_HEADS), jnp.tile(swa_q * q_scale, SWA_HEADS)])[:, None]
    gk = jnp.concatenate([nsa_ks, nsa_kw, jnp.tile(swa_k, SWA_KV_HEADS)])[None, :]
    return wt, w, gq, gk


def _block_diag(blocks):
    n, r, c = blocks.shape
    eye = jnp.eye(n, dtype=blocks.dtype)
    return (eye[:, None, :, None] * blocks[:, :, None, :]).reshape(n * r, n * c)


def kernel(x, ffn1_norm, ffn1_w1, ffn1_w3, ffn1_w2, mix_norm, w_in, nsa_q_norm, nsa_kc_norm, nsa_ks_norm, nsa_kw_norm, cmp_pos_k, cmp_w1_k, cmp_w2_k, cmp_pos_v, cmp_w1_v, cmp_w2_v, conv_w, swa_q_norm, swa_k_norm, swa_sinks, pool_w, pool_scale, group_norm, w_out, ffn2_norm, ffn2_w1, ffn2_w3, ffn2_w2):
    b, t, d = x.shape
    depth = w_in.shape[0]
    nch = t // CMP_STRIDE
    ovt = _overlap_t(t)
    seg = jnp.asarray(np.kron(np.eye(K_COLS // HEAD_DIM), np.ones((HEAD_DIM, HEAD_DIM))), bf16)
    row = lambda v: v[None, :].astype(f32)
    cb = lambda v: v.astype(bf16)
    groups = lambda v, ids: jnp.concatenate([v[i * GROUP_WIDTH:(i + 1) * GROUP_WIDTH] for i in ids], axis=0)

    x = x.reshape(b * t, d)
    for l in range(depth):
        x = _ffn(x, row(ffn1_norm[l]), cb(ffn1_w1[l]), cb(ffn1_w3[l]), cb(ffn1_w2[l]))

        wt, w, gq, gk = _mixer_weights(w_in[l], nsa_q_norm[l], nsa_ks_norm[l], nsa_kw_norm[l],
                                       swa_q_norm[l], swa_k_norm[l])
        qt, kk, vt, kvc, gt, ybd = _proj(
            x.reshape(b, t, d), row(mix_norm[l]), wt, w, seg, gq, gk, conv_w[l],
            cb(_block_diag(pool_w[l])), row(pool_scale[l]), row(groups(group_norm[l], (1, 3))))

        chunks = lambda a: a.reshape(b, nch, CMP_STRIDE * HEAD_DIM)
        pos = lambda p: p.reshape(CMP_LEN // CMP_STRIDE, CMP_STRIDE * HEAD_DIM)
        kc, vct = _compress(chunks(kvc[..., :HEAD_DIM]), chunks(kvc[..., HEAD_DIM:]),
                            pos(cmp_pos_k[l]), cb(cmp_w1_k[l]), cb(cmp_w2_k[l]), row(nsa_kc_norm[l]),
                            pos(cmp_pos_v[l]), cb(cmp_w1_v[l]), cb(cmp_w2_v[l].T))

        yac = _attn(qt, kk, vt, kc, vct, gt, row(swa_sinks[l] * LOG2E),
                    row(groups(group_norm[l], (0, 2))), ovt)

        mix = (yac.reshape(b * t, -1), ybd.reshape(b * t, -1),
               cb(groups(w_out[l], (0, 2))), cb(groups(w_out[l], (1, 3))))
        x = _ffn(x, row(ffn2_norm[l]), cb(ffn2_w1[l]), cb(ffn2_w3[l]), cb(ffn2_w2[l]), mix=mix)
    return x.reshape(b, t, d)
```

```python
import functools

import numpy as np
import jax
import jax.numpy as jnp
from jax import lax
from jax.experimental import pallas as pl
from jax.experimental.pallas import tpu as pltpu

HEAD_DIM = 64
GROUP_WIDTH = 256
N_GROUPS = 4
NSA_HEADS = 4
NSA_BRANCHES = 3
CMP_LEN = 32
CMP_STRIDE = 16
SLC_BLOCK = 64
SLC_TOP = 8
NSA_WINDOW = 512
SWA_HEADS = 4
SWA_KV_HEADS = 2
SWA_WINDOW = 128
POOL_WINDOWS = (2, 4, 8, 16)
Q_TILE = 256
SUBQ = 128
EPS = 1e-6
NEG = -1e30
FORCE = 1e4
ATTN_SCALE = HEAD_DIM ** -0.5
LOG2E = float(np.log2(np.e))
SLC_SEG = 512
SLC_SHIFT = SLC_BLOCK.bit_length() - 1
POOL_SHIFT = (GROUP_WIDTH // len(POOL_WINDOWS)).bit_length() - 1

LANES = 128
SUBLANES = 8
CONV_HALO = 8
POOL_HALO = 16
VMEM_LIMIT = 56 * 2 ** 20

QT_ROWS = 512
V_HEADS = 4
VHEAD = 2 * HEAD_DIM
VT_ROWS = V_HEADS * VHEAD
GT_ROWS = 2 * SUBLANES
T_ROWS = QT_ROWS + V_HEADS * HEAD_DIM + GT_ROWS
K_COLS = 256
KC_COLS = 128
B_COLS = 768
D_COLS = 256
OFF_K = 0
OFF_KC = OFF_K + K_COLS
OFF_B = OFF_KC + KC_COLS
OFF_D = OFF_B + B_COLS
W_COLS = OFF_D + D_COLS

SPLIT_SIZES = (256, 64, 64, 64, 64, 64, 64, 12, 256, 256, 256, 256, 128, 128, 256)

f32 = jnp.float32
bf16 = jnp.bfloat16


def _rms(x, g):
    return x * lax.rsqrt(jnp.mean(x * x, axis=-1, keepdims=True) + EPS) * g


def _dot(a, b):
    return jnp.dot(a, b, preferred_element_type=f32)


def _dot_nt(a, b):
    return lax.dot_general(a, b, (((1,), (1,)), ((), ())), preferred_element_type=f32)


def _split(a):
    hi = a.astype(bf16)
    return hi, (a - hi.astype(f32)).astype(bf16)


def _const_spec(shape):
    nd = len(shape)
    return pl.BlockSpec(shape, lambda *_: (0,) * nd, pipeline_mode=pl.Buffered(1))


def _ffn_kernel(*refs, with_mix):
    if with_mix:
        x_ref, ya_ref, yb_ref, woa_ref, wob_ref, g_ref, w1_ref, w3_ref, w2_ref, o_ref = refs
        x = x_ref[...] + _dot(ya_ref[...], woa_ref[...]) + _dot(yb_ref[...], wob_ref[...])
    else:
        x_ref, g_ref, w1_ref, w3_ref, w2_ref, o_ref = refs
        x = x_ref[...]
    h = _rms(x, g_ref[...]).astype(bf16)
    a = _dot(h, w1_ref[...])
    b = _dot(h, w3_ref[...])
    act = (a * jax.nn.sigmoid(a) * b).astype(bf16)
    o_ref[...] = x + 0.5 * _dot(act, w2_ref[...])


def _ffn(x, gain, w1, w3, w2, mix=None, tm=512):
    n, d = x.shape
    dff = w1.shape[1]
    row = lambda w: pl.BlockSpec((tm, w), lambda i: (i, 0))
    args, specs = [x], [row(d)]
    if mix is not None:
        ya, yb, woa, wob = mix
        args += [ya, yb, woa, wob]
        specs += [row(ya.shape[1]), row(yb.shape[1]), _const_spec(woa.shape), _const_spec(wob.shape)]
    args += [gain, w1, w3, w2]
    specs += [_const_spec((1, d)), _const_spec((d, dff)), _const_spec((d, dff)), _const_spec((dff, d))]
    return pl.pallas_call(
        functools.partial(_ffn_kernel, with_mix=mix is not None),
        grid=(n // tm,),
        in_specs=specs,
        out_specs=row(d),
        out_shape=jax.ShapeDtypeStruct((n, d), f32),
        compiler_params=pltpu.CompilerParams(dimension_semantics=("parallel",),
                                             vmem_limit_bytes=VMEM_LIMIT),
        name="ffn" if mix is None else "ffn_mix",
    )(*args)


def _proj_kernel(x_ref, g_ref, wt_ref, w_ref, seg_ref, gq_ref, gk_ref, cw_ref, pw_ref, ps_ref, gbd_ref,
                 qt_ref, kk_ref, vt_ref, kc_ref, gt_ref, ybd_ref,
                 zhalo_ref, vhalo_ref):
    tt = x_ref.shape[1]
    ti = pl.program_id(1)

    @pl.when(ti == 0)
    def _():
        zhalo_ref[...] = jnp.zeros_like(zhalo_ref)
        vhalo_ref[...] = jnp.zeros_like(vhalo_ref)

    h = _rms(x_ref[0], g_ref[...]).astype(bf16)

    ut = _dot_nt(wt_ref[...], h)
    u_k = _dot(h, w_ref[:, OFF_K:OFF_K + K_COLS])
    ub = _dot(h, w_ref[:, OFF_B:OFF_B + B_COLS])
    v = _dot(h, w_ref[:, OFF_D:OFF_D + D_COLS])
    kc_ref[0] = _dot(h, w_ref[:, OFF_KC:OFF_KC + KC_COLS])

    for hd in range(QT_ROWS // HEAD_DIM):
        rows = slice(hd * HEAD_DIM, (hd + 1) * HEAD_DIM)
        u = ut[rows]
        ms = jnp.mean(u * u, axis=0, keepdims=True)
        qt_ref[0, rows, :] = (u * lax.rsqrt(ms + EPS) * gq_ref[rows, :]).astype(bf16)
    ones = jnp.ones((HEAD_DIM, LANES), bf16)
    for hv in range(V_HEADS):
        vh = ut[QT_ROWS + hv * HEAD_DIM:QT_ROWS + (hv + 1) * HEAD_DIM].astype(bf16)
        for c in range(tt // LANES):
            vt_ref[0, c, hv * VHEAD:hv * VHEAD + HEAD_DIM, :] = vh[:, c * LANES:(c + 1) * LANES]
            vt_ref[0, c, hv * VHEAD + HEAD_DIM:(hv + 1) * VHEAD, :] = ones
    gt_ref[0] = jax.nn.sigmoid(ut[QT_ROWS + V_HEADS * HEAD_DIM:])

    hi, lo = _split(u_k * u_k)
    ss = _dot(hi, seg_ref[...]) + _dot(lo, seg_ref[...])
    kk_ref[0] = (u_k * lax.rsqrt(ss * (1.0 / HEAD_DIM) + EPS) * gk_ref[...]).astype(bf16)

    s = jnp.concatenate([vhalo_ref[...], v], axis=0)
    grp = jnp.right_shift(lax.broadcasted_iota(jnp.int32, (1, D_COLS), 1), POOL_SHIFT)
    acc = None
    for k, w in enumerate(POOL_WINDOWS):
        s = s + pltpu.roll(s, w // 2, 0)
        acc = s if acc is None else jnp.where(grp >= k, s, acc)
    win = jnp.where(grp == 0, POOL_WINDOWS[0],
                    jnp.where(grp == 1, POOL_WINDOWS[1],
                              jnp.where(grp == 2, POOL_WINDOWS[2], POOL_WINDOWS[3])))
    t = ti * tt + lax.broadcasted_iota(jnp.int32, (tt, 1), 0)
    cnt = jnp.minimum(t + 1, win).astype(f32)
    d = acc[POOL_HALO:] / cnt - v
    o_d = _dot(d.astype(bf16), pw_ref[...]) * ps_ref[...]
    vhalo_ref[...] = v[tt - POOL_HALO:]

    z = ub[:, GROUP_WIDTH:2 * GROUP_WIDTH] * ub[:, 2 * GROUP_WIDTH:]
    ze = jnp.concatenate([zhalo_ref[...], z], axis=0)
    z1 = pltpu.roll(ze, 1, 0)[CONV_HALO:]
    z2 = pltpu.roll(ze, 2, 0)[CONV_HALO:]
    conv = cw_ref[0:1, :] * z2 + cw_ref[1:2, :] * z1 + cw_ref[2:3, :] * z
    o_b = ub[:, :GROUP_WIDTH] * conv
    zhalo_ref[...] = z[tt - CONV_HALO:]

    ybd_ref[0, :, :GROUP_WIDTH] = _rms(o_b, gbd_ref[:, :GROUP_WIDTH]).astype(bf16)
    ybd_ref[0, :, GROUP_WIDTH:] = _rms(o_d, gbd_ref[:, GROUP_WIDTH:]).astype(bf16)


def _proj(x3, gain, wt, w, seg, gq, gk, conv_w, pool_w, pool_scale, gbd, tt=512):
    b, t, d = x3.shape
    tok = lambda w_: pl.BlockSpec((1, tt, w_), lambda i, j: (i, j, 0))
    feat = lambda r: pl.BlockSpec((1, r, tt), lambda i, j: (i, 0, j))
    consts = [gain, wt, w, seg, gq, gk, conv_w, pool_w, pool_scale, gbd]
    return pl.pallas_call(
        _proj_kernel,
        grid=(b, t // tt),
        in_specs=[tok(d)] + [_const_spec(c.shape) for c in consts],
        out_specs=[feat(QT_ROWS), tok(K_COLS),
                   pl.BlockSpec((1, tt // LANES, VT_ROWS, LANES), lambda i, j: (i, j, 0, 0)),
                   tok(KC_COLS), feat(GT_ROWS), tok(2 * GROUP_WIDTH)],
        out_shape=[jax.ShapeDtypeStruct((b, QT_ROWS, t), bf16),
                   jax.ShapeDtypeStruct((b, t, K_COLS), bf16),
                   jax.ShapeDtypeStruct((b, t // LANES, VT_ROWS, LANES), bf16),
                   jax.ShapeDtypeStruct((b, t, KC_COLS), f32),
                   jax.ShapeDtypeStruct((b, GT_ROWS, t), f32),
                   jax.ShapeDtypeStruct((b, t, 2 * GROUP_WIDTH), bf16)],
        scratch_shapes=[pltpu.VMEM((CONV_HALO, GROUP_WIDTH), f32), pltpu.VMEM((POOL_HALO, D_COLS), f32)],
        compiler_params=pltpu.CompilerParams(dimension_semantics=("parallel", "arbitrary"),
                                             vmem_limit_bytes=VMEM_LIMIT),
        name="proj",
    )(x3, *consts)


def _compress_kernel(kr_ref, vr_ref, pk_ref, w1k_ref, w2k_ref, gk_ref, pv_ref, w1v_ref, w2vt_ref,
                     kc_ref, vct_ref):
    nch = kr_ref.shape[1]
    half = kr_ref.shape[2]

    def hidden(c, pos_ref, w1_ref):
        ha = _dot((c + pos_ref[0:1, :]).astype(bf16), w1_ref[0:half, :])
        hb = _dot((c + pos_ref[1:2, :]).astype(bf16), w1_ref[half:2 * half, :])
        return jax.nn.gelu(ha + pltpu.roll(hb, nch - 1, 0)).astype(bf16)

    kc_ref[0] = _rms(_dot(hidden(kr_ref[0], pk_ref, w1k_ref), w2k_ref[...]), gk_ref[...]).astype(bf16)
    vct_ref[0] = _dot_nt(w2vt_ref[...], hidden(vr_ref[0], pv_ref, w1v_ref)).astype(bf16)


def _compress(kr, vr, pk, w1k, w2k, gk, pv, w1v, w2vt):
    b, nch, width = kr.shape
    blk = pl.BlockSpec((1, nch, width), lambda i: (i, 0, 0))
    consts = [pk, w1k, w2k, gk, pv, w1v, w2vt]
    return pl.pallas_call(
        _compress_kernel,
        grid=(b,),
        in_specs=[blk, blk] + [_const_spec(c.shape) for c in consts],
        out_specs=[pl.BlockSpec((1, nch, HEAD_DIM), lambda i: (i, 0, 0)),
                   pl.BlockSpec((1, HEAD_DIM, nch), lambda i: (i, 0, 0))],
        out_shape=[jax.ShapeDtypeStruct((b, nch, HEAD_DIM), bf16),
                   jax.ShapeDtypeStruct((b, HEAD_DIM, nch), bf16)],
        compiler_params=pltpu.CompilerParams(dimension_semantics=("parallel",),
                                             vmem_limit_bytes=VMEM_LIMIT),
        name="compress",
    )(kr, vr, *consts)


def _attn_kernel(qt_ref, kk_ref, vt_ref, kc_ref, vct_ref, gt_ref, sink_ref, gac_ref, ovt_ref,
                 y_ref, negsel_ref):
    tile = qt_ref.shape[2]
    sq = SUBQ
    subs = range(tile // sq)
    t_len = kk_ref.shape[1]
    nch = kc_ref.shape[1]
    n_sel = ovt_ref.shape[0]
    seg = min(SLC_SEG, t_len)
    rep = SWA_HEADS // SWA_KV_HEADS
    kv_groups = range(SWA_KV_HEADS)
    tile0 = pl.program_id(1) * tile
    t0 = [tile0 + u * sq for u in subs]
    lane_head = lambda a, h: a[:, h * sq:(h + 1) * sq]

    def stack_heads(u, first, count):
        return jnp.concatenate(
            [qt_ref[0, (first + h) * HEAD_DIM:(first + h + 1) * HEAD_DIM, u * sq:(u + 1) * sq]
             for h in range(count)], axis=1)

    def values(chunk0, n_chunks, head):
        rows = slice(head * VHEAD, (head + 1) * VHEAD)
        return jnp.concatenate([vt_ref[0, chunk0 + c, rows, :] for c in range(n_chunks)], axis=1)

    def finish(acc, extra=None):
        den = acc[HEAD_DIM:]
        if extra is not None:
            den = den + extra
        return acc[:HEAD_DIM] / den

    def band_bias(u, s0, span, window):
        diff = (t0[u] + lax.broadcasted_iota(jnp.int32, (span, sq), 1)
                - (s0 + lax.broadcasted_iota(jnp.int32, (span, sq), 0)))
        return jnp.where((diff >= 0) & (diff < window), 0.0, NEG)

    def emit(u, gi, o):
        cols = slice(gi * GROUP_WIDTH, (gi + 1) * GROUP_WIDTH)
        y_ref[0, u * sq:(u + 1) * sq, cols] = _rms(o.T, gac_ref[:, cols]).astype(bf16)

    sink = lambda g, r: sink_ref[:, g * rep + r:g * rep + r + 1]
    span_w = min(NSA_WINDOW + sq, t_len)
    span_c = min(SWA_WINDOW + sq, t_len)
    s0_w = [pl.multiple_of(jnp.maximum(t0[u] - NSA_WINDOW, 0), sq) for u in subs]
    s0_c = [pl.multiple_of(jnp.maximum(t0[u] - SWA_WINDOW, 0), sq) for u in subs]

    qa = [stack_heads(u, 0, NSA_HEADS) for u in subs]
    s_cmp = [_dot(kc_ref[0], qa[u]) for u in subs]
    s_w = [_dot(kk_ref[0, pl.ds(s0_w[u], span_w), HEAD_DIM:2 * HEAD_DIM], qa[u]) for u in subs]
    s_c = [[_dot(kk_ref[0, pl.ds(s0_c[u], span_c), (2 + g) * HEAD_DIM:(3 + g) * HEAD_DIM],
                 stack_heads(u, NSA_HEADS + g * rep, rep)) for g in kv_groups] for u in subs]

    n_i = lax.broadcasted_iota(jnp.int32, (nch, NSA_HEADS * sq), 0)
    c_lane = lax.broadcasted_iota(jnp.int32, (nch, NSA_HEADS * sq), 1) & (sq - 1)
    c_valid = [(n_i * CMP_STRIDE + (CMP_LEN - 1) <= t0[u] + c_lane) & (n_i < nch - 1) for u in subs]
    z_cmp = [jnp.where(c_valid[u], s_cmp[u], NEG) for u in subs]
    m_cmp = [jnp.max(z_cmp[u], axis=0, keepdims=True) for u in subs]
    z_w, m_w, z_c, m_c = [], [], [], []
    for u in subs:
        bias = band_bias(u, s0_w[u], span_w, NSA_WINDOW)
        z_w.append(jnp.concatenate([lane_head(s_w[u], h) + bias for h in range(NSA_HEADS)], axis=1))
        m_w.append(jnp.max(z_w[u], axis=0, keepdims=True))
        bias = band_bias(u, s0_c[u], span_c, SWA_WINDOW)
        z_c.append([[lane_head(s_c[u][g], r) + bias for r in range(rep)] for g in kv_groups])
        m_c.append([[jnp.maximum(jnp.max(z_c[u][g][r], axis=0, keepdims=True), sink(g, r))
                     for r in range(rep)] for g in kv_groups])

    e_cmp = [jnp.exp2(z_cmp[u] - m_cmp[u]) for u in subs]
    p_w = [jnp.exp2(z_w[u] - m_w[u]).astype(bf16) for u in subs]
    p_c = [[jnp.concatenate([jnp.exp2(z_c[u][g][r] - m_c[u][g][r]).astype(bf16) for r in range(rep)], axis=1)
            for g in kv_groups] for u in subs]

    p_cmp = [jnp.where(c_valid[u], e_cmp[u] / jnp.sum(e_cmp[u], axis=0, keepdims=True), 0.0) for u in subs]
    o_cmp = [_dot(vct_ref[0], p_cmp[u].astype(bf16)) for u in subs]
    imp = []
    for u in subs:
        p_sum = lane_head(p_cmp[u], 0)
        for h in range(1, NSA_HEADS):
            p_sum = p_sum + lane_head(p_cmp[u], h)
        p_hi, p_lo = _split(p_sum)
        imp.append(_dot(ovt_ref[...], p_hi) + _dot(ovt_ref[...], p_lo))
    o_win = [finish(_dot(values(s0_w[u] // LANES, span_w // LANES, 1), p_w[u])) for u in subs]
    for u in subs:
        o_c = []
        for g in kv_groups:
            acc = _dot(values(s0_c[u] // LANES, span_c // LANES, 2 + g), p_c[u][g])
            o_c += [finish(lane_head(acc, r), jnp.exp2(sink(g, r) - m_c[u][g][r])) for r in range(rep)]
        emit(u, 1, jnp.concatenate(o_c, axis=0))

    j = lax.broadcasted_iota(jnp.int32, (n_sel, sq), 0)
    for u in subs:
        t = t0[u] + lax.broadcasted_iota(jnp.int32, (n_sel, sq), 1)
        cur = jnp.right_shift(t, SLC_SHIFT)
        forced = (j == 0) | (j == cur) | (j == cur - 1)
        score = jnp.where(forced, FORCE, jnp.where(j * SLC_BLOCK > t, -1.0, imp[u]))
        rank = jnp.zeros((n_sel, sq), f32)
        for jp in range(n_sel):
            other = score[jp:jp + 1, :]
            ahead = (other > score) | ((other == score) & (j > jp))
            rank = rank + jnp.where(ahead, 1.0, 0.0)
        negsel_ref[u] = jnp.where(rank < float(min(SLC_TOP, n_sel)), 0.0, NEG)

    def seg_body(k, carry):
        m, acc = carry
        base = pl.multiple_of(k * seg, seg)
        k_seg = kk_ref[0, pl.ds(base, seg), 0:HEAD_DIM]
        v_seg = values(k * (seg // LANES), seg // LANES, 0)
        s = [_dot(k_seg, qa[u]) for u in subs]
        z, m_new = [], []
        for u in subs:
            bias = jnp.concatenate(
                [jnp.broadcast_to(negsel_ref[u, pl.ds(k * (seg // SLC_BLOCK) + b, 1), :], (SLC_BLOCK, sq))
                 for b in range(seg // SLC_BLOCK)], axis=0)
            causal = (base + lax.broadcasted_iota(jnp.int32, (seg, sq), 0)
                      <= t0[u] + lax.broadcasted_iota(jnp.int32, (seg, sq), 1))
            bias = jnp.where(causal, bias, NEG)
            z.append(jnp.concatenate([lane_head(s[u], h) + bias for h in range(NSA_HEADS)], axis=1))
            m_new.append(jnp.maximum(m[u], jnp.max(z[u], axis=0, keepdims=True)))
        p = [jnp.exp2(z[u] - m_new[u]).astype(bf16) for u in subs]
        pv = [_dot(v_seg, p[u]) for u in subs]
        return tuple(m_new), tuple(jnp.exp2(m[u] - m_new[u]) * acc[u] + pv[u] for u in subs)

    n_seg = (tile0 + tile - 1) // seg + 1
    init = (tuple(jnp.full((1, NSA_HEADS * sq), NEG, f32) for _ in subs),
            tuple(jnp.zeros((VHEAD, NSA_HEADS * sq), f32) for _ in subs))
    _, acc = lax.fori_loop(0, n_seg, seg_body, init)

    for u in subs:
        o_slc = finish(acc[u])
        gt = gt_ref[0, :, u * sq:(u + 1) * sq]
        o_a = []
        for h in range(NSA_HEADS):
            g = lambda k: gt[NSA_BRANCHES * h + k:NSA_BRANCHES * h + k + 1, :]
            o_a.append(g(0) * lane_head(o_cmp[u], h) + g(1) * lane_head(o_slc, h)
                       + g(2) * lane_head(o_win[u], h))
        emit(u, 0, jnp.concatenate(o_a, axis=0))


def _attn(qt, kk, vt, kc, vct, gt, sinks, gac, ovt):
    b, _, t = qt.shape
    tq = Q_TILE
    full = lambda a: pl.BlockSpec((1,) + a.shape[1:], lambda i, j: (i,) + (0,) * (a.ndim - 1))
    feat = lambda a: pl.BlockSpec((1, a.shape[1], tq), lambda i, j: (i, 0, j))
    width = 2 * GROUP_WIDTH
    return pl.pallas_call(
        _attn_kernel,
        grid=(b, t // tq),
        in_specs=[feat(qt), full(kk), full(vt), full(kc), full(vct), feat(gt),
                  _const_spec(sinks.shape), _const_spec(gac.shape), _const_spec(ovt.shape)],
        out_specs=pl.BlockSpec((1, tq, width), lambda i, j: (i, j, 0)),
        out_shape=jax.ShapeDtypeStruct((b, t, width), bf16),
        scratch_shapes=[pltpu.VMEM((tq // SUBQ, ovt.shape[0], SUBQ), f32)],
        compiler_params=pltpu.CompilerParams(dimension_semantics=("parallel", "parallel"),
                                             vmem_limit_bytes=VMEM_LIMIT),
        name="attn",
    )(qt, kk, vt, kc, vct, gt, sinks, gac, ovt)


def _overlap_t(t_len):
    n_c = (t_len - CMP_LEN) // CMP_STRIDE + 1
    nch = t_len // CMP_STRIDE
    ci = np.arange(nch)[None, :] * CMP_STRIDE
    sj = np.arange(t_len // SLC_BLOCK)[:, None] * SLC_BLOCK
    ov = (ci <= sj + SLC_BLOCK - 1) & (ci + CMP_LEN - 1 >= sj) & (np.arange(nch)[None, :] < n_c)
    return jnp.asarray(ov, bf16)


def _mixer_weights(w_in, nsa_q, nsa_ks, nsa_kw, swa_q, swa_k):
    offs = np.concatenate([[0], np.cumsum(SPLIT_SIZES)])
    (a_q, a_kc, a_vc, a_ks, a_vs, a_kw, a_vw, a_g,
     b_b, b_c, b_x, c_q, c_k, c_v, d_v) = [w_in[:, offs[k]:offs[k + 1]] for k in range(len(SPLIT_SIZES))]
    a_g = jnp.pad(a_g, ((0, 0), (0, GT_ROWS - a_g.shape[1])))
    wt = jnp.concatenate([a_q, c_q, a_vs, a_vw, c_v, a_g], axis=1).T.astype(bf16)
    w = jnp.concatenate([a_ks, a_kw, c_k, a_kc, a_vc, b_b, b_c, b_x, d_v], axis=1).astype(bf16)
    q_scale = ATTN_SCALE * LOG2E
    gq = jnp.concatenate([jnp.tile(nsa_q * q_scale, NSA_HEADS), jnp.tile(swa_q * q_scale, SWA_HEADS)])[:, None]
    gk = jnp.concatenate([nsa_ks, nsa_kw, jnp.tile(swa_k, SWA_KV_HEADS)])[None, :]
    return wt, w, gq, gk


def _block_diag(blocks):
    n, r, c = blocks.shape
    eye = jnp.eye(n, dtype=blocks.dtype)
    return (eye[:, None, :, None] * blocks[:, :, None, :]).reshape(n * r, n * c)


def kernel(x, ffn1_norm, ffn1_w1, ffn1_w3, ffn1_w2, mix_norm, w_in, nsa_q_norm, nsa_kc_norm, nsa_ks_norm, nsa_kw_norm, cmp_pos_k, cmp_w1_k, cmp_w2_k, cmp_pos_v, cmp_w1_v, cmp_w2_v, conv_w, swa_q_norm, swa_k_norm, swa_sinks, pool_w, pool_scale, group_norm, w_out, ffn2_norm, ffn2_w1, ffn2_w3, ffn2_w2):
    b, t, d = x.shape
    depth = w_in.shape[0]
    nch = t // CMP_STRIDE
    ovt = _overlap_t(t)
    seg = jnp.asarray(np.kron(np.eye(K_COLS // HEAD_DIM), np.ones((HEAD_DIM, HEAD_DIM))), bf16)
    row = lambda v: v[None, :].astype(f32)
    cb = lambda v: v.astype(bf16)
    groups = lambda v, ids: jnp.concatenate([v[i * GROUP_WIDTH:(i + 1) * GROUP_WIDTH] for i in ids], axis=0)

    x = x.reshape(b * t, d)
    for l in range(depth):
        x = _ffn(x, row(ffn1_norm[l]), cb(ffn1_w1[l]), cb(ffn1_w3[l]), cb(ffn1_w2[l]))

        wt, w, gq, gk = _mixer_weights(w_in[l], nsa_q_norm[l], nsa_ks_norm[l], nsa_kw_norm[l],
                                       swa_q_norm[l], swa_k_norm[l])
        qt, kk, vt, kvc, gt, ybd = _proj(
            x.reshape(b, t, d), row(mix_norm[l]), wt, w, seg, gq, gk, conv_w[l],
            cb(_block_diag(pool_w[l])), row(pool_scale[l]), row(groups(group_norm[l], (1, 3))))

        chunks = lambda a: a.reshape(b, nch, CMP_STRIDE * HEAD_DIM)
        pos = lambda p: p.reshape(CMP_LEN // CMP_STRIDE, CMP_STRIDE * HEAD_DIM)
        kc, vct = _compress(chunks(kvc[..., :HEAD_DIM]), chunks(kvc[..., HEAD_DIM:]),
                            pos(cmp_pos_k[l]), cb(cmp_w1_k[l]), cb(cmp_w2_k[l]), row(nsa_kc_norm[l]),
                            pos(cmp_pos_v[l]), cb(cmp_w1_v[l]), cb(cmp_w2_v[l].T))

        yac = _attn(qt, kk, vt, kc, vct, gt, row(swa_sinks[l] * LOG2E),
                    row(groups(group_norm[l], (0, 2))), ovt)

        mix = (yac.reshape(b * t, -1), ybd.reshape(b * t, -1),
               cb(groups(w_out[l], (0, 2))), cb(groups(w_out[l], (1, 3))))
        x = _ffn(x, row(ffn2_norm[l]), cb(ffn2_w1[l]), cb(ffn2_w3[l]), cb(ffn2_w2[l]), mix=mix)
    return x.reshape(b, t, d)
```

```python
import functools

import numpy as np
import jax
import jax.numpy as jnp
from jax import lax
from jax.experimental import pallas as pl
from jax.experimental.pallas import tpu as pltpu

HEAD_DIM = 64
GROUP_WIDTH = 256
N_GROUPS = 4
NSA_HEADS = 4
NSA_BRANCHES = 3
CMP_LEN = 32
CMP_STRIDE = 16
SLC_BLOCK = 64
SLC_TOP = 8
NSA_WINDOW = 512
SWA_HEADS = 4
SWA_KV_HEADS = 2
SWA_WINDOW = 128
POOL_WINDOWS = (2, 4, 8, 16)
Q_TILE = 256
SUBQ = 128
EPS = 1e-6
NEG = -1e30
FORCE = 1e4
ATTN_SCALE = HEAD_DIM ** -0.5
LOG2E = float(np.log2(np.e))
SLC_SEG = 512
SLC_SHIFT = SLC_BLOCK.bit_length() - 1
POOL_SHIFT = (GROUP_WIDTH // len(POOL_WINDOWS)).bit_length() - 1

LANES = 128
SUBLANES = 8
CONV_HALO = 8
POOL_HALO = 16
VMEM_LIMIT = 56 * 2 ** 20

QT_ROWS = 512
V_HEADS = 4
VHEAD = 2 * HEAD_DIM
VT_ROWS = V_HEADS * VHEAD
GT_ROWS = 2 * SUBLANES
T_ROWS = QT_ROWS + V_HEADS * HEAD_DIM + GT_ROWS
K_COLS = 256
KC_COLS = 128
B_COLS = 768
D_COLS = 256
OFF_K = 0
OFF_KC = OFF_K + K_COLS
OFF_B = OFF_KC + KC_COLS
OFF_D = OFF_B + B_COLS
W_COLS = OFF_D + D_COLS

SPLIT_SIZES = (256, 64, 64, 64, 64, 64, 64, 12, 256, 256, 256, 256, 128, 128, 256)

f32 = jnp.float32
bf16 = jnp.bfloat16


def _rms(x, g):
    return x * lax.rsqrt(jnp.mean(x * x, axis=-1, keepdims=True) + EPS) * g


def _dot(a, b):
    return jnp.dot(a, b, preferred_element_type=f32)


def _dot_nt(a, b):
    return lax.dot_general(a, b, (((1,), (1,)), ((), ())), preferred_element_type=f32)


def _split(a):
    hi = a.astype(bf16)
    return hi, (a - hi.astype(f32)).astype(bf16)


def _const_spec(shape):
    nd = len(shape)
    return pl.BlockSpec(shape, lambda *_: (0,) * nd, pipeline_mode=pl.Buffered(1))


def _ffn_kernel(*refs, with_mix):
    if with_mix:
        x_ref, ya_ref, yb_ref, woa_ref, wob_ref, g_ref, w1_ref, w3_ref, w2_ref, o_ref = refs
        x = x_ref[...] + _dot(ya_ref[...], woa_ref[...]) + _dot(yb_ref[...], wob_ref[...])
    else:
        x_ref, g_ref, w1_ref, w3_ref, w2_ref, o_ref = refs
        x = x_ref[...]
    h = _rms(x, g_ref[...]).astype(bf16)
    a = _dot(h, w1_ref[...])
    b = _dot(h, w3_ref[...])
    act = (a * jax.nn.sigmoid(a) * b).astype(bf16)
    o_ref[...] = x + 0.5 * _dot(act, w2_ref[...])


def _ffn(x, gain, w1, w3, w2, mix=None, tm=512):
    n, d = x.shape
    dff = w1.shape[1]
    row = lambda w: pl.BlockSpec((tm, w), lambda i: (i, 0))
    args, specs = [x], [row(d)]
    if mix is not None:
        ya, yb, woa, wob = mix
        args += [ya, yb, woa, wob]
        specs += [row(ya.shape[1]), row(yb.shape[1]), _const_spec(woa.shape), _const_spec(wob.shape)]
    args += [gain, w1, w3, w2]
    specs += [_const_spec((1, d)), _const_spec((d, dff)), _const_spec((d, dff)), _const_spec((dff, d))]
    return pl.pallas_call(
        functools.partial(_ffn_kernel, with_mix=mix is not None),
        grid=(n // tm,),
        in_specs=specs,
        out_specs=row(d),
        out_shape=jax.ShapeDtypeStruct((n, d), f32),
        compiler_params=pltpu.CompilerParams(dimension_semantics=("parallel",),
                                             vmem_limit_bytes=VMEM_LIMIT),
        name="ffn" if mix is None else "ffn_mix",
    )(*args)


def _proj_kernel(x_ref, g_ref, wt_ref, w_ref, seg_ref, gq_ref, gk_ref, cw_ref, pw_ref, ps_ref, gbd_ref,
                 qt_ref, kk_ref, vt_ref, kc_ref, gt_ref, ybd_ref,
                 zhalo_ref, vhalo_ref):
    tt = x_ref.shape[1]
    ti = pl.program_id(1)

    @pl.when(ti == 0)
    def _():
        zhalo_ref[...] = jnp.zeros_like(zhalo_ref)
        vhalo_ref[...] = jnp.zeros_like(vhalo_ref)

    h = _rms(x_ref[0], g_ref[...]).astype(bf16)

    ut = _dot_nt(wt_ref[...], h)
    u_k = _dot(h, w_ref[:, OFF_K:OFF_K + K_COLS])
    ub = _dot(h, w_ref[:, OFF_B:OFF_B + B_COLS])
    v = _dot(h, w_ref[:, OFF_D:OFF_D + D_COLS])
    kc_ref[0] = _dot(h, w_ref[:, OFF_KC:OFF_KC + KC_COLS])

    for hd in range(QT_ROWS // HEAD_DIM):
        rows = slice(hd * HEAD_DIM, (hd + 1) * HEAD_DIM)
        u = ut[rows]
        ms = jnp.mean(u * u, axis=0, keepdims=True)
        qt_ref[0, rows, :] = (u * lax.rsqrt(ms + EPS) * gq_ref[rows, :]).astype(bf16)
    ones = jnp.ones((HEAD_DIM, LANES), bf16)
    for hv in range(V_HEADS):
        vh = ut[QT_ROWS + hv * HEAD_DIM:QT_ROWS + (hv + 1) * HEAD_DIM].astype(bf16)
        for c in range(tt // LANES):
            vt_ref[0, c, hv * VHEAD:hv * VHEAD + HEAD_DIM, :] = vh[:, c * LANES:(c + 1) * LANES]
            vt_ref[0, c, hv * VHEAD + HEAD_DIM:(hv + 1) * VHEAD, :] = ones
    gt_ref[0] = jax.nn.sigmoid(ut[QT_ROWS + V_HEADS * HEAD_DIM:])

    hi, lo = _split(u_k * u_k)
    ss = _dot(hi, seg_ref[...]) + _dot(lo, seg_ref[...])
    kk_ref[0] = (u_k * lax.rsqrt(ss * (1.0 / HEAD_DIM) + EPS) * gk_ref[...]).astype(bf16)

    s = jnp.concatenate([vhalo_ref[...], v], axis=0)
    grp = jnp.right_shift(lax.broadcasted_iota(jnp.int32, (1, D_COLS), 1), POOL_SHIFT)
    acc = None
    for k, w in enumerate(POOL_WINDOWS):
        s = s + pltpu.roll(s, w // 2, 0)
        acc = s if acc is None else jnp.where(grp >= k, s, acc)
    win = jnp.where(grp == 0, POOL_WINDOWS[0],
                    jnp.where(grp == 1, POOL_WINDOWS[1],
                              jnp.where(grp == 2, POOL_WINDOWS[2], POOL_WINDOWS[3])))
    t = ti * tt + lax.broadcasted_iota(jnp.int32, (tt, 1), 0)
    cnt = jnp.minimum(t + 1, win).astype(f32)
    d = acc[POOL_HALO:] / cnt - v
    o_d = _dot(d.astype(bf16), pw_ref[...]) * ps_ref[...]
    vhalo_ref[...] = v[tt - POOL_HALO:]

    z = ub[:, GROUP_WIDTH:2 * GROUP_WIDTH] * ub[:, 2 * GROUP_WIDTH:]
    ze = jnp.concatenate([zhalo_ref[...], z], axis=0)
    z1 = pltpu.roll(ze, 1, 0)[CONV_HALO:]
    z2 = pltpu.roll(ze, 2, 0)[CONV_HALO:]
    conv = cw_ref[0:1, :] * z2 + cw_ref[1:2, :] * z1 + cw_ref[2:3, :] * z
    o_b = ub[:, :GROUP_WIDTH] * conv
    zhalo_ref[...] = z[tt - CONV_HALO:]

    ybd_ref[0, :, :GROUP_WIDTH] = _rms(o_b, gbd_ref[:, :GROUP_WIDTH]).astype(bf16)
    ybd_ref[0, :, GROUP_WIDTH:] = _rms(o_d, gbd_ref[:, GROUP_WIDTH:]).astype(bf16)


def _proj(x3, gain, wt, w, seg, gq, gk, conv_w, pool_w, pool_scale, gbd, tt=512):
    b, t, d = x3.shape
    tok = lambda w_: pl.BlockSpec((1, tt, w_), lambda i, j: (i, j, 0))
    feat = lambda r: pl.BlockSpec((1, r, tt), lambda i, j: (i, 0, j))
    consts = [gain, wt, w, seg, gq, gk, conv_w, pool_w, pool_scale, gbd]
    return pl.pallas_call(
        _proj_kernel,
        grid=(b, t // tt),
        in_specs=[tok(d)] + [_const_spec(c.shape) for c in consts],
        out_specs=[feat(QT_ROWS), tok(K_COLS),
                   pl.BlockSpec((1, tt // LANES, VT_ROWS, LANES), lambda i, j: (i, j, 0, 0)),
                   tok(KC_COLS), feat(GT_ROWS), tok(2 * GROUP_WIDTH)],
        out_shape=[jax.ShapeDtypeStruct((b, QT_ROWS, t), bf16),
                   jax.ShapeDtypeStruct((b, t, K_COLS), bf16),
                   jax.ShapeDtypeStruct((b, t // LANES, VT_ROWS, LANES), bf16),
                   jax.ShapeDtypeStruct((b, t, KC_COLS), f32),
                   jax.ShapeDtypeStruct((b, GT_ROWS, t), f32),
                   jax.ShapeDtypeStruct((b, t, 2 * GROUP_WIDTH), bf16)],
        scratch_shapes=[pltpu.VMEM((CONV_HALO, GROUP_WIDTH), f32), pltpu.VMEM((POOL_HALO, D_COLS), f32)],
        compiler_params=pltpu.CompilerParams(dimension_semantics=("parallel", "arbitrary"),
                                             vmem_limit_bytes=VMEM_LIMIT),
        name="proj",
    )(x3, *consts)


def _compress_kernel(kvc_ref, pk_ref, w1k_ref, w2k_ref, gk_ref, pv_ref, w1v_ref, w2vt_ref,
                     kc_ref, vct_ref):
    nch = kvc_ref.shape[1] // CMP_STRIDE
    half = CMP_STRIDE * HEAD_DIM
    rows = [kvc_ref[0, pl.ds(r, nch, stride=CMP_STRIDE), :] for r in range(CMP_STRIDE)]
    k_chunks = jnp.concatenate([x[:, :HEAD_DIM] for x in rows], axis=1)
    v_chunks = jnp.concatenate([x[:, HEAD_DIM:] for x in rows], axis=1)

    def hidden(c, pos_ref, w1_ref):
        ha = _dot((c + pos_ref[0:1, :]).astype(bf16), w1_ref[0:half, :])
        hb = _dot((c + pos_ref[1:2, :]).astype(bf16), w1_ref[half:2 * half, :])
        return jax.nn.gelu(ha + pltpu.roll(hb, nch - 1, 0)).astype(bf16)

    kc_ref[0] = _rms(_dot(hidden(k_chunks, pk_ref, w1k_ref), w2k_ref[...]), gk_ref[...]).astype(bf16)
    vct_ref[0] = _dot_nt(w2vt_ref[...], hidden(v_chunks, pv_ref, w1v_ref)).astype(bf16)


def _compress(kvc, pk, w1k, w2k, gk, pv, w1v, w2vt):
    b, t, width = kvc.shape
    nch = t // CMP_STRIDE
    consts = [pk, w1k, w2k, gk, pv, w1v, w2vt]
    return pl.pallas_call(
        _compress_kernel,
        grid=(b,),
        in_specs=[pl.BlockSpec((1, t, width), lambda i: (i, 0, 0))] + [_const_spec(c.shape) for c in consts],
        out_specs=[pl.BlockSpec((1, nch, HEAD_DIM), lambda i: (i, 0, 0)),
                   pl.BlockSpec((1, HEAD_DIM, nch), lambda i: (i, 0, 0))],
        out_shape=[jax.ShapeDtypeStruct((b, nch, HEAD_DIM), bf16),
                   jax.ShapeDtypeStruct((b, HEAD_DIM, nch), bf16)],
        compiler_params=pltpu.CompilerParams(dimension_semantics=("parallel",),
                                             vmem_limit_bytes=VMEM_LIMIT),
        name="compress",
    )(kvc, *consts)


def _attn_kernel(qt_ref, kk_ref, vt_ref, kc_ref, vct_ref, gt_ref, sink_ref, gac_ref, ovt_ref,
                 y_ref, negsel_ref):
    tile = qt_ref.shape[2]
    sq = SUBQ
    subs = range(tile // sq)
    t_len = kk_ref.shape[1]
    nch = kc_ref.shape[1]
    n_sel = ovt_ref.shape[0]
    seg = min(SLC_SEG, t_len)
    rep = SWA_HEADS // SWA_KV_HEADS
    kv_groups = range(SWA_KV_HEADS)
    tile0 = pl.program_id(1) * tile
    t0 = [tile0 + u * sq for u in subs]
    lane_head = lambda a, h: a[:, h * sq:(h + 1) * sq]

    def stack_heads(u, first, count):
        return jnp.concatenate(
            [qt_ref[0, (first + h) * HEAD_DIM:(first + h + 1) * HEAD_DIM, u * sq:(u + 1) * sq]
             for h in range(count)], axis=1)

    def values(chunk0, n_chunks, head):
        rows = slice(head * VHEAD, (head + 1) * VHEAD)
        return jnp.concatenate([vt_ref[0, chunk0 + c, rows, :] for c in range(n_chunks)], axis=1)

    def finish(acc, extra=None):
        den = acc[HEAD_DIM:]
        if extra is not None:
            den = den + extra
        return acc[:HEAD_DIM] / den

    def band_bias(u, s0, span, window):
        diff = (t0[u] + lax.broadcasted_iota(jnp.int32, (span, sq), 1)
                - (s0 + lax.broadcasted_iota(jnp.int32, (span, sq), 0)))
        return jnp.where((diff >= 0) & (diff < window), 0.0, NEG)

    def emit(u, gi, o):
        cols = slice(gi * GROUP_WIDTH, (gi + 1) * GROUP_WIDTH)
        y_ref[0, u * sq:(u + 1) * sq, cols] = _rms(o.T, gac_ref[:, cols]).astype(bf16)

    sink = lambda g, r: sink_ref[:, g * rep + r:g * rep + r + 1]
    span_w = min(NSA_WINDOW + sq, t_len)
    span_c = min(SWA_WINDOW + sq, t_len)
    s0_w = [pl.multiple_of(jnp.maximum(t0[u] - NSA_WINDOW, 0), sq) for u in subs]
    s0_c = [pl.multiple_of(jnp.maximum(t0[u] - SWA_WINDOW, 0), sq) for u in subs]

    qa = [stack_heads(u, 0, NSA_HEADS) for u in subs]
    s_cmp = [_dot(kc_ref[0], qa[u]) for u in subs]
    s_w = [_dot(kk_ref[0, pl.ds(s0_w[u], span_w), HEAD_DIM:2 * HEAD_DIM], qa[u]) for u in subs]
    s_c = [[_dot(kk_ref[0, pl.ds(s0_c[u], span_c), (2 + g) * HEAD_DIM:(3 + g) * HEAD_DIM],
                 stack_heads(u, NSA_HEADS + g * rep, rep)) for g in kv_groups] for u in subs]

    n_i = lax.broadcasted_iota(jnp.int32, (nch, NSA_HEADS * sq), 0)
    c_lane = lax.broadcasted_iota(jnp.int32, (nch, NSA_HEADS * sq), 1) & (sq - 1)
    c_valid = [(n_i * CMP_STRIDE + (CMP_LEN - 1) <= t0[u] + c_lane) & (n_i < nch - 1) for u in subs]
    z_cmp = [jnp.where(c_valid[u], s_cmp[u], NEG) for u in subs]
    m_cmp = [jnp.max(z_cmp[u], axis=0, keepdims=True) for u in subs]
    z_w, m_w, z_c, m_c = [], [], [], []
    for u in subs:
        bias = band_bias(u, s0_w[u], span_w, NSA_WINDOW)
        z_w.append(jnp.concatenate([lane_head(s_w[u], h) + bias for h in range(NSA_HEADS)], axis=1))
        m_w.append(jnp.max(z_w[u], axis=0, keepdims=True))
        bias = band_bias(u, s0_c[u], span_c, SWA_WINDOW)
        z_c.append([[lane_head(s_c[u][g], r) + bias for r in range(rep)] for g in kv_groups])
        m_c.append([[jnp.maximum(jnp.max(z_c[u][g][r], axis=0, keepdims=True), sink(g, r))
                     for r in range(rep)] for g in kv_groups])

    e_cmp = [jnp.exp2(z_cmp[u] - m_cmp[u]) for u in subs]
    p_w = [jnp.exp2(z_w[u] - m_w[u]).astype(bf16) for u in subs]
    p_c = [[jnp.concatenate([jnp.exp2(z_c[u][g][r] - m_c[u][g][r]).astype(bf16) for r in range(rep)], axis=1)
            for g in kv_groups] for u in subs]

    p_cmp = [jnp.where(c_valid[u], e_cmp[u] / jnp.sum(e_cmp[u], axis=0, keepdims=True), 0.0) for u in subs]
    o_cmp = [_dot(vct_ref[0], p_cmp[u].astype(bf16)) for u in subs]
    imp = []
    for u in subs:
        p_sum = lane_head(p_cmp[u], 0)
        for h in range(1, NSA_HEADS):
            p_sum = p_sum + lane_head(p_cmp[u], h)
        p_hi, p_lo = _split(p_sum)
        imp.append(_dot(ovt_ref[...], p_hi) + _dot(ovt_ref[...], p_lo))
    o_win = [finish(_dot(values(s0_w[u] // LANES, span_w // LANES, 1), p_w[u])) for u in subs]
    for u in subs:
        o_c = []
        for g in kv_groups:
            acc = _dot(values(s0_c[u] // LANES, span_c // LANES, 2 + g), p_c[u][g])
            o_c += [finish(lane_head(acc, r), jnp.exp2(sink(g, r) - m_c[u][g][r])) for r in range(rep)]
        emit(u, 1, jnp.concatenate(o_c, axis=0))

    j = lax.broadcasted_iota(jnp.int32, (n_sel, sq), 0)
    for u in subs:
        t = t0[u] + lax.broadcasted_iota(jnp.int32, (n_sel, sq), 1)
        cur = jnp.right_shift(t, SLC_SHIFT)
        forced = (j == 0) | (j == cur) | (j == cur - 1)
        score = jnp.where(forced, FORCE, jnp.where(j * SLC_BLOCK > t, -1.0, imp[u]))
        tiles = [slice(r * SUBLANES, (r + 1) * SUBLANES) for r in range(n_sel // SUBLANES)]
        rank = [jnp.zeros((SUBLANES, sq), f32) for _ in tiles]
        for jp in range(n_sel):
            other = score[jp:jp + 1, :]
            for r, rows in enumerate(tiles):
                mine = score[rows]
                if rows.stop <= jp:
                    ahead = other > mine
                elif rows.start > jp:
                    ahead = other >= mine
                else:
                    ahead = (other > mine) | ((other == mine) & (j[rows] > jp))
                rank[r] = rank[r] + jnp.where(ahead, 1.0, 0.0)
        negsel_ref[u] = jnp.where(jnp.concatenate(rank, axis=0) < float(min(SLC_TOP, n_sel)), 0.0, NEG)

    def selected(n):
        def scores(k):
            k_seg = kk_ref[0, k * seg:(k + 1) * seg, 0:HEAD_DIM]
            return [_dot(k_seg, qa[u]) for u in subs]

        def masked(k, s):
            z = []
            for u in subs:
                bias = jnp.concatenate(
                    [jnp.broadcast_to(negsel_ref[u, b:b + 1, :], (SLC_BLOCK, sq))
                     for b in range(k * (seg // SLC_BLOCK), (k + 1) * (seg // SLC_BLOCK))], axis=0)
                if k == n - 1:
                    causal = (k * seg + lax.broadcasted_iota(jnp.int32, (seg, sq), 0)
                              <= t0[u] + lax.broadcasted_iota(jnp.int32, (seg, sq), 1))
                    bias = jnp.where(causal, bias, NEG)
                z.append(jnp.concatenate([lane_head(s[u], h) + bias for h in range(NSA_HEADS)], axis=1))
            return z

        m = [jnp.full((1, NSA_HEADS * sq), NEG, f32) for _ in subs]
        acc = [jnp.zeros((VHEAD, NSA_HEADS * sq), f32) for _ in subs]
        s_next = scores(0)
        for k in range(n):
            s_cur = s_next
            if k + 1 < n:
                s_next = scores(k + 1)
            z = masked(k, s_cur)
            m_new = [jnp.maximum(m[u], jnp.max(z[u], axis=0, keepdims=True)) for u in subs]
            p = [jnp.exp2(z[u] - m_new[u]).astype(bf16) for u in subs]
            v_seg = values(k * (seg // LANES), seg // LANES, 0)
            acc = [jnp.exp2(m[u] - m_new[u]) * acc[u] + _dot(v_seg, p[u]) for u in subs]
            m = m_new
        for u in subs:
            o_slc = finish(acc[u])
            gt = gt_ref[0, :, u * sq:(u + 1) * sq]
            o_a = []
            for h in range(NSA_HEADS):
                g = lambda k: gt[NSA_BRANCHES * h + k:NSA_BRANCHES * h + k + 1, :]
                o_a.append(g(0) * lane_head(o_cmp[u], h) + g(1) * lane_head(o_slc, h)
                           + g(2) * lane_head(o_win[u], h))
            emit(u, 0, jnp.concatenate(o_a, axis=0))

    n_seg = (tile0 + tile - 1) // seg + 1
    for n in range(1, t_len // seg + 1):
        pl.when(n_seg == n)(functools.partial(selected, n))


def _attn(qt, kk, vt, kc, vct, gt, sinks, gac, ovt):
    b, _, t = qt.shape
    tq = Q_TILE
    full = lambda a: pl.BlockSpec((1,) + a.shape[1:], lambda i, j: (i,) + (0,) * (a.ndim - 1))
    feat = lambda a: pl.BlockSpec((1, a.shape[1], tq), lambda i, j: (i, 0, j))
    width = 2 * GROUP_WIDTH
    return pl.pallas_call(
        _attn_kernel,
        grid=(b, t // tq),
        in_specs=[feat(qt), full(kk), full(vt), full(kc), full(vct), feat(gt),
                  _const_spec(sinks.shape), _const_spec(gac.shape), _const_spec(ovt.shape)],
        out_specs=pl.BlockSpec((1, tq, width), lambda i, j: (i, j, 0)),
        out_shape=jax.ShapeDtypeStruct((b, t, width), bf16),
        scratch_shapes=[pltpu.VMEM((tq // SUBQ, ovt.shape[0], SUBQ), f32)],
        compiler_params=pltpu.CompilerParams(dimension_semantics=("parallel", "parallel"),
                                             vmem_limit_bytes=VMEM_LIMIT),
        name="attn",
    )(qt, kk, vt, kc, vct, gt, sinks, gac, ovt)


def _overlap_t(t_len):
    n_c = (t_len - CMP_LEN) // CMP_STRIDE + 1
    nch = t_len // CMP_STRIDE
    ci = np.arange(nch)[None, :] * CMP_STRIDE
    sj = np.arange(t_len // SLC_BLOCK)[:, None] * SLC_BLOCK
    ov = (ci <= sj + SLC_BLOCK - 1) & (ci + CMP_LEN - 1 >= sj) & (np.arange(nch)[None, :] < n_c)
    return jnp.asarray(ov, bf16)


def _mixer_weights(w_in, nsa_q, nsa_ks, nsa_kw, swa_q, swa_k):
    offs = np.concatenate([[0], np.cumsum(SPLIT_SIZES)])
    (a_q, a_kc, a_vc, a_ks, a_vs, a_kw, a_vw, a_g,
     b_b, b_c, b_x, c_q, c_k, c_v, d_v) = [w_in[:, offs[k]:offs[k + 1]] for k in range(len(SPLIT_SIZES))]
    a_g = jnp.pad(a_g, ((0, 0), (0, GT_ROWS - a_g.shape[1])))
    wt = jnp.concatenate([a_q, c_q, a_vs, a_vw, c_v, a_g], axis=1).T.astype(bf16)
    w = jnp.concatenate([a_ks, a_kw, c_k, a_kc, a_vc, b_b, b_c, b_x, d_v], axis=1).astype(bf16)
    q_scale = ATTN_SCALE * LOG2E
    gq = jnp.concatenate([jnp.tile(nsa_q * q_scale, NSA_HEADS), jnp.tile(swa_q * q_scale, SWA_HEADS)])[:, None]
    gk = jnp.concatenate([nsa_ks, nsa_kw, jnp.tile(swa_k, SWA_KV_HEADS)])[None, :]
    return wt, w, gq, gk


def _block_diag(blocks):
    n, r, c = blocks.shape
    eye = jnp.eye(n, dtype=blocks.dtype)
    return (eye[:, None, :, None] * blocks[:, :, None, :]).reshape(n * r, n * c)


def kernel(x, ffn1_norm, ffn1_w1, ffn1_w3, ffn1_w2, mix_norm, w_in, nsa_q_norm, nsa_kc_norm, nsa_ks_norm, nsa_kw_norm, cmp_pos_k, cmp_w1_k, cmp_w2_k, cmp_pos_v, cmp_w1_v, cmp_w2_v, conv_w, swa_q_norm, swa_k_norm, swa_sinks, pool_w, pool_scale, group_norm, w_out, ffn2_norm, ffn2_w1, ffn2_w3, ffn2_w2):
    b, t, d = x.shape
    depth = w_in.shape[0]
    ovt = _overlap_t(t)
    seg = jnp.asarray(np.kron(np.eye(K_COLS // HEAD_DIM), np.ones((HEAD_DIM, HEAD_DIM))), bf16)
    row = lambda v: v[None, :].astype(f32)
    cb = lambda v: v.astype(bf16)
    groups = lambda v, ids: jnp.concatenate([v[i * GROUP_WIDTH:(i + 1) * GROUP_WIDTH] for i in ids], axis=0)

    x = x.reshape(b * t, d)
    for l in range(depth):
        x = _ffn(x, row(ffn1_norm[l]), cb(ffn1_w1[l]), cb(ffn1_w3[l]), cb(ffn1_w2[l]))

        wt, w, gq, gk = _mixer_weights(w_in[l], nsa_q_norm[l], nsa_ks_norm[l], nsa_kw_norm[l],
                                       swa_q_norm[l], swa_k_norm[l])
        qt, kk, vt, kvc, gt, ybd = _proj(
            x.reshape(b, t, d), row(mix_norm[l]), wt, w, seg, gq, gk, conv_w[l],
            cb(_block_diag(pool_w[l])), row(pool_scale[l]), row(groups(group_norm[l], (1, 3))))

        pos = lambda p: p.reshape(CMP_LEN // CMP_STRIDE, CMP_STRIDE * HEAD_DIM)
        kc, vct = _compress(kvc, pos(cmp_pos_k[l]), cb(cmp_w1_k[l]), cb(cmp_w2_k[l]), row(nsa_kc_norm[l]),
                            pos(cmp_pos_v[l]), cb(cmp_w1_v[l]), cb(cmp_w2_v[l].T))

        yac = _attn(qt, kk, vt, kc, vct, gt, row(swa_sinks[l] * LOG2E),
                    row(groups(group_norm[l], (0, 2))), ovt)

        mix = (yac.reshape(b * t, -1), ybd.reshape(b * t, -1),
               cb(groups(w_out[l], (0, 2))), cb(groups(w_out[l], (1, 3))))
        x = _ffn(x, row(ffn2_norm[l]), cb(ffn2_w1[l]), cb(ffn2_w3[l]), cb(ffn2_w2[l]), mix=mix)
    return x.reshape(b, t, d)
```

```python
import functools

import numpy as np
import jax
import jax.numpy as jnp
from jax import lax
from jax.experimental import pallas as pl
from jax.experimental.pallas import tpu as pltpu

HEAD_DIM = 64
GROUP_WIDTH = 256
N_GROUPS = 4
NSA_HEADS = 4
NSA_BRANCHES = 3
CMP_LEN = 32
CMP_STRIDE = 16
SLC_BLOCK = 64
SLC_TOP = 8
NSA_WINDOW = 512
SWA_HEADS = 4
SWA_KV_HEADS = 2
SWA_WINDOW = 128
POOL_WINDOWS = (2, 4, 8, 16)
Q_TILE = 256
SUBQ = 128
EPS = 1e-6
NEG = -1e30
FORCE = 1e4
ATTN_SCALE = HEAD_DIM ** -0.5
LOG2E = float(np.log2(np.e))
SLC_SEG = 512
SLC_SHIFT = SLC_BLOCK.bit_length() - 1
POOL_SHIFT = (GROUP_WIDTH // len(POOL_WINDOWS)).bit_length() - 1

LANES = 128
SUBLANES = 8
CONV_HALO = 8
POOL_HALO = 16
VMEM_LIMIT = 56 * 2 ** 20

QT_ROWS = 512
V_HEADS = 4
VHEAD = 2 * HEAD_DIM
VT_ROWS = V_HEADS * VHEAD
GT_ROWS = 2 * SUBLANES
T_ROWS = QT_ROWS + V_HEADS * HEAD_DIM + GT_ROWS
K_COLS = 256
KC_COLS = 128
B_COLS = 768
D_COLS = 256
OFF_K = 0
OFF_KC = OFF_K + K_COLS
OFF_B = OFF_KC + KC_COLS
OFF_D = OFF_B + B_COLS
W_COLS = OFF_D + D_COLS

SPLIT_SIZES = (256, 64, 64, 64, 64, 64, 64, 12, 256, 256, 256, 256, 128, 128, 256)

f32 = jnp.float32
bf16 = jnp.bfloat16


def _rms(x, g):
    return x * lax.rsqrt(jnp.mean(x * x, axis=-1, keepdims=True) + EPS) * g


def _dot(a, b):
    return jnp.dot(a, b, preferred_element_type=f32)


def _dot_nt(a, b):
    return lax.dot_general(a, b, (((1,), (1,)), ((), ())), preferred_element_type=f32)


def _split(a):
    hi = a.astype(bf16)
    return hi, (a - hi.astype(f32)).astype(bf16)


def _const_spec(shape):
    nd = len(shape)
    return pl.BlockSpec(shape, lambda *_: (0,) * nd, pipeline_mode=pl.Buffered(1))


def _layer_spec(stacked, layer):
    return pl.BlockSpec((None,) + stacked.shape[1:], lambda *_: (layer, 0, 0), pipeline_mode=pl.Buffered(1))


def _cast_kernel(w_ref, o_ref):
    o_ref[...] = w_ref[...].astype(bf16)


def _to_bf16(w, rows):
    depth, r, c = w.shape
    spec = pl.BlockSpec((1, rows, c), lambda l, i: (l, i, 0))
    return pl.pallas_call(
        _cast_kernel,
        grid=(depth, r // rows),
        in_specs=[spec],
        out_specs=spec,
        out_shape=jax.ShapeDtypeStruct(w.shape, bf16),
        compiler_params=pltpu.CompilerParams(dimension_semantics=("parallel", "parallel"),
                                             vmem_limit_bytes=VMEM_LIMIT),
        name="cast",
    )(w)


def _ffn_kernel(*refs, with_mix):
    if with_mix:
        x_ref, ya_ref, yb_ref, wo_ref, g_ref, w1_ref, w3_ref, w2_ref, o_ref = refs
        x = x_ref[...]
        for gi, y_ref in ((0, ya_ref), (1, yb_ref)):
            for half in range(2):
                rows = slice((gi + 2 * half) * GROUP_WIDTH, (gi + 2 * half + 1) * GROUP_WIDTH)
                x = x + _dot(y_ref[:, half * GROUP_WIDTH:(half + 1) * GROUP_WIDTH], wo_ref[rows, :])
    else:
        x_ref, g_ref, w1_ref, w3_ref, w2_ref, o_ref = refs
        x = x_ref[...]
    h = _rms(x, g_ref[...]).astype(bf16)
    a = _dot(h, w1_ref[...])
    b = _dot(h, w3_ref[...])
    act = (a * jax.nn.sigmoid(a) * b).astype(bf16)
    o_ref[...] = x + 0.5 * _dot(act, w2_ref[...])


def _ffn(x, gain, w1, w3, w2, layer, mix=None, tm=512):
    n, d = x.shape
    row = lambda w: pl.BlockSpec((tm, w), lambda i: (i, 0))
    args, specs = [x], [row(d)]
    if mix is not None:
        ya, yb, wo = mix
        args += [ya, yb, wo]
        specs += [row(ya.shape[1]), row(yb.shape[1]), _layer_spec(wo, layer)]
    args += [gain, w1, w3, w2]
    specs += [_const_spec((1, d))] + [_layer_spec(w, layer) for w in (w1, w3, w2)]
    return pl.pallas_call(
        functools.partial(_ffn_kernel, with_mix=mix is not None),
        grid=(n // tm,),
        in_specs=specs,
        out_specs=row(d),
        out_shape=jax.ShapeDtypeStruct((n, d), f32),
        compiler_params=pltpu.CompilerParams(dimension_semantics=("parallel",),
                                             vmem_limit_bytes=VMEM_LIMIT),
        name="ffn" if mix is None else "ffn_mix",
    )(*args)


def _proj_kernel(x_ref, g_ref, wt_ref, w_ref, seg_ref, gq_ref, gk_ref, cw_ref, pw_ref, ps_ref, gbd_ref,
                 qt_ref, kk_ref, vt_ref, kc_ref, gt_ref, ybd_ref,
                 zhalo_ref, vhalo_ref):
    tt = x_ref.shape[1]
    ti = pl.program_id(1)

    @pl.when(ti == 0)
    def _():
        zhalo_ref[...] = jnp.zeros_like(zhalo_ref)
        vhalo_ref[...] = jnp.zeros_like(vhalo_ref)

    h = _rms(x_ref[0], g_ref[...]).astype(bf16)

    ut = _dot_nt(wt_ref[...], h)
    u_k = _dot(h, w_ref[:, OFF_K:OFF_K + K_COLS])
    ub = _dot(h, w_ref[:, OFF_B:OFF_B + B_COLS])
    v = _dot(h, w_ref[:, OFF_D:OFF_D + D_COLS])
    kc_ref[0] = _dot(h, w_ref[:, OFF_KC:OFF_KC + KC_COLS])

    for hd in range(QT_ROWS // HEAD_DIM):
        rows = slice(hd * HEAD_DIM, (hd + 1) * HEAD_DIM)
        u = ut[rows]
        ms = jnp.mean(u * u, axis=0, keepdims=True)
        qt_ref[0, rows, :] = (u * lax.rsqrt(ms + EPS) * gq_ref[rows, :]).astype(bf16)
    ones = jnp.ones((HEAD_DIM, LANES), bf16)
    for hv in range(V_HEADS):
        vh = ut[QT_ROWS + hv * HEAD_DIM:QT_ROWS + (hv + 1) * HEAD_DIM].astype(bf16)
        for c in range(tt // LANES):
            vt_ref[0, c, hv * VHEAD:hv * VHEAD + HEAD_DIM, :] = vh[:, c * LANES:(c + 1) * LANES]
            vt_ref[0, c, hv * VHEAD + HEAD_DIM:(hv + 1) * VHEAD, :] = ones
    gt_ref[0] = jax.nn.sigmoid(ut[QT_ROWS + V_HEADS * HEAD_DIM:])

    hi, lo = _split(u_k * u_k)
    ss = _dot(hi, seg_ref[...]) + _dot(lo, seg_ref[...])
    kk_ref[0] = (u_k * lax.rsqrt(ss * (1.0 / HEAD_DIM) + EPS) * gk_ref[...]).astype(bf16)

    s = jnp.concatenate([vhalo_ref[...], v], axis=0)
    grp = jnp.right_shift(lax.broadcasted_iota(jnp.int32, (1, D_COLS), 1), POOL_SHIFT)
    acc = None
    for k, w in enumerate(POOL_WINDOWS):
        s = s + pltpu.roll(s, w // 2, 0)
        acc = s if acc is None else jnp.where(grp >= k, s, acc)
    win = jnp.where(grp == 0, POOL_WINDOWS[0],
                    jnp.where(grp == 1, POOL_WINDOWS[1],
                              jnp.where(grp == 2, POOL_WINDOWS[2], POOL_WINDOWS[3])))
    t = ti * tt + lax.broadcasted_iota(jnp.int32, (tt, 1), 0)
    cnt = jnp.minimum(t + 1, win).astype(f32)
    d = acc[POOL_HALO:] / cnt - v
    o_d = _dot(d.astype(bf16), pw_ref[...]) * ps_ref[...]
    vhalo_ref[...] = v[tt - POOL_HALO:]

    z = ub[:, GROUP_WIDTH:2 * GROUP_WIDTH] * ub[:, 2 * GROUP_WIDTH:]
    ze = jnp.concatenate([zhalo_ref[...], z], axis=0)
    z1 = pltpu.roll(ze, 1, 0)[CONV_HALO:]
    z2 = pltpu.roll(ze, 2, 0)[CONV_HALO:]
    conv = cw_ref[0:1, :] * z2 + cw_ref[1:2, :] * z1 + cw_ref[2:3, :] * z
    o_b = ub[:, :GROUP_WIDTH] * conv
    zhalo_ref[...] = z[tt - CONV_HALO:]

    ybd_ref[0, :, :GROUP_WIDTH] = _rms(o_b, gbd_ref[:, :GROUP_WIDTH]).astype(bf16)
    ybd_ref[0, :, GROUP_WIDTH:] = _rms(o_d, gbd_ref[:, GROUP_WIDTH:]).astype(bf16)


def _proj(x3, gain, wt, w, seg, gq, gk, conv_w, pool_w, pool_scale, gbd, tt=1024):
    b, t, d = x3.shape
    tok = lambda w_: pl.BlockSpec((1, tt, w_), lambda i, j: (i, j, 0))
    feat = lambda r: pl.BlockSpec((1, r, tt), lambda i, j: (i, 0, j))
    consts = [gain, wt, w, seg, gq, gk, conv_w, pool_w, pool_scale, gbd]
    return pl.pallas_call(
        _proj_kernel,
        grid=(b, t // tt),
        in_specs=[tok(d)] + [_const_spec(c.shape) for c in consts],
        out_specs=[feat(QT_ROWS), tok(K_COLS),
                   pl.BlockSpec((1, tt // LANES, VT_ROWS, LANES), lambda i, j: (i, j, 0, 0)),
                   tok(KC_COLS), feat(GT_ROWS), tok(2 * GROUP_WIDTH)],
        out_shape=[jax.ShapeDtypeStruct((b, QT_ROWS, t), bf16),
                   jax.ShapeDtypeStruct((b, t, K_COLS), bf16),
                   jax.ShapeDtypeStruct((b, t // LANES, VT_ROWS, LANES), bf16),
                   jax.ShapeDtypeStruct((b, t, KC_COLS), f32),
                   jax.ShapeDtypeStruct((b, GT_ROWS, t), f32),
                   jax.ShapeDtypeStruct((b, t, 2 * GROUP_WIDTH), bf16)],
        scratch_shapes=[pltpu.VMEM((CONV_HALO, GROUP_WIDTH), f32), pltpu.VMEM((POOL_HALO, D_COLS), f32)],
        compiler_params=pltpu.CompilerParams(dimension_semantics=("parallel", "arbitrary"),
                                             vmem_limit_bytes=VMEM_LIMIT),
        name="proj",
    )(x3, *consts)


def _compress_kernel(kvc_ref, pk_ref, w1k_ref, w2k_ref, gk_ref, pv_ref, w1v_ref, w2vt_ref,
                     kc_ref, vct_ref):
    nch = kvc_ref.shape[1] // CMP_STRIDE
    half = CMP_STRIDE * HEAD_DIM
    rows = [kvc_ref[0, pl.ds(r, nch, stride=CMP_STRIDE), :] for r in range(CMP_STRIDE)]
    k_chunks = jnp.concatenate([x[:, :HEAD_DIM] for x in rows], axis=1)
    v_chunks = jnp.concatenate([x[:, HEAD_DIM:] for x in rows], axis=1)

    def hidden(c, pos_ref, w1_ref):
        ha = _dot((c + pos_ref[0:1, :]).astype(bf16), w1_ref[0:half, :])
        hb = _dot((c + pos_ref[1:2, :]).astype(bf16), w1_ref[half:2 * half, :])
        return jax.nn.gelu(ha + pltpu.roll(hb, nch - 1, 0)).astype(bf16)

    kc_ref[0] = _rms(_dot(hidden(k_chunks, pk_ref, w1k_ref), w2k_ref[...]), gk_ref[...]).astype(bf16)
    vct_ref[0] = _dot_nt(w2vt_ref[...], hidden(v_chunks, pv_ref, w1v_ref)).astype(bf16)


def _compress(kvc, pk, w1k, w2k, gk, pv, w1v, w2vt):
    b, t, width = kvc.shape
    nch = t // CMP_STRIDE
    consts = [pk, w1k, w2k, gk, pv, w1v, w2vt]
    return pl.pallas_call(
        _compress_kernel,
        grid=(b,),
        in_specs=[pl.BlockSpec((1, t, width), lambda i: (i, 0, 0))] + [_const_spec(c.shape) for c in consts],
        out_specs=[pl.BlockSpec((1, nch, HEAD_DIM), lambda i: (i, 0, 0)),
                   pl.BlockSpec((1, HEAD_DIM, nch), lambda i: (i, 0, 0))],
        out_shape=[jax.ShapeDtypeStruct((b, nch, HEAD_DIM), bf16),
                   jax.ShapeDtypeStruct((b, HEAD_DIM, nch), bf16)],
        compiler_params=pltpu.CompilerParams(dimension_semantics=("parallel",),
                                             vmem_limit_bytes=VMEM_LIMIT),
        name="compress",
    )(kvc, *consts)


def _attn_kernel(qt_ref, kk_ref, vt_ref, kc_ref, vct_ref, gt_ref, sink_ref, gac_ref, ovt_ref,
                 y_ref, negsel_ref):
    tile = qt_ref.shape[2]
    sq = SUBQ
    subs = range(tile // sq)
    t_len = kk_ref.shape[1]
    nch = kc_ref.shape[1]
    n_sel = ovt_ref.shape[0]
    seg = min(SLC_SEG, t_len)
    rep = SWA_HEADS // SWA_KV_HEADS
    kv_groups = range(SWA_KV_HEADS)
    tile0 = pl.program_id(1) * tile
    t0 = [tile0 + u * sq for u in subs]
    lane_head = lambda a, h: a[:, h * sq:(h + 1) * sq]

    def stack_heads(u, first, count):
        return jnp.concatenate(
            [qt_ref[0, (first + h) * HEAD_DIM:(first + h + 1) * HEAD_DIM, u * sq:(u + 1) * sq]
             for h in range(count)], axis=1)

    def values(chunk0, n_chunks, head):
        rows = slice(head * VHEAD, (head + 1) * VHEAD)
        return jnp.concatenate([vt_ref[0, chunk0 + c, rows, :] for c in range(n_chunks)], axis=1)

    def finish(acc, extra=None):
        den = acc[HEAD_DIM:]
        if extra is not None:
            den = den + extra
        return acc[:HEAD_DIM] / den

    def band_bias(u, s0, span, window):
        diff = (t0[u] + lax.broadcasted_iota(jnp.int32, (span, sq), 1)
                - (s0 + lax.broadcasted_iota(jnp.int32, (span, sq), 0)))
        return jnp.where((diff >= 0) & (diff < window), 0.0, NEG)

    def emit(u, gi, o):
        cols = slice(gi * GROUP_WIDTH, (gi + 1) * GROUP_WIDTH)
        y_ref[0, u * sq:(u + 1) * sq, cols] = _rms(o.T, gac_ref[:, cols]).astype(bf16)

    sink = lambda g, r: sink_ref[:, g * rep + r:g * rep + r + 1]
    span_w = min(NSA_WINDOW + sq, t_len)
    span_c = min(SWA_WINDOW + sq, t_len)
    s0_w = [pl.multiple_of(jnp.maximum(t0[u] - NSA_WINDOW, 0), sq) for u in subs]
    s0_c = [pl.multiple_of(jnp.maximum(t0[u] - SWA_WINDOW, 0), sq) for u in subs]

    qa = [stack_heads(u, 0, NSA_HEADS) for u in subs]
    s_cmp = [_dot(kc_ref[0], qa[u]) for u in subs]
    s_w = [_dot(kk_ref[0, pl.ds(s0_w[u], span_w), HEAD_DIM:2 * HEAD_DIM], qa[u]) for u in subs]
    s_c = [[_dot(kk_ref[0, pl.ds(s0_c[u], span_c), (2 + g) * HEAD_DIM:(3 + g) * HEAD_DIM],
                 stack_heads(u, NSA_HEADS + g * rep, rep)) for g in kv_groups] for u in subs]

    n_i = lax.broadcasted_iota(jnp.int32, (nch, NSA_HEADS * sq), 0)
    c_lane = lax.broadcasted_iota(jnp.int32, (nch, NSA_HEADS * sq), 1) & (sq - 1)
    c_valid = [(n_i * CMP_STRIDE + (CMP_LEN - 1) <= t0[u] + c_lane) & (n_i < nch - 1) for u in subs]
    z_cmp = [jnp.where(c_valid[u], s_cmp[u], NEG) for u in subs]
    m_cmp = [jnp.max(z_cmp[u], axis=0, keepdims=True) for u in subs]
    z_w, m_w, z_c, m_c = [], [], [], []
    for u in subs:
        bias = band_bias(u, s0_w[u], span_w, NSA_WINDOW)
        z_w.append(jnp.concatenate([lane_head(s_w[u], h) + bias for h in range(NSA_HEADS)], axis=1))
        m_w.append(jnp.max(z_w[u], axis=0, keepdims=True))
        bias = band_bias(u, s0_c[u], span_c, SWA_WINDOW)
        z_c.append([[lane_head(s_c[u][g], r) + bias for r in range(rep)] for g in kv_groups])
        m_c.append([[jnp.maximum(jnp.max(z_c[u][g][r], axis=0, keepdims=True), sink(g, r))
                     for r in range(rep)] for g in kv_groups])

    e_cmp = [jnp.exp2(z_cmp[u] - m_cmp[u]) for u in subs]
    p_w = [jnp.exp2(z_w[u] - m_w[u]).astype(bf16) for u in subs]
    p_c = [[jnp.concatenate([jnp.exp2(z_c[u][g][r] - m_c[u][g][r]).astype(bf16) for r in range(rep)], axis=1)
            for g in kv_groups] for u in subs]

    p_cmp = [jnp.where(c_valid[u], e_cmp[u] / jnp.sum(e_cmp[u], axis=0, keepdims=True), 0.0) for u in subs]
    o_cmp = [_dot(vct_ref[0], p_cmp[u].astype(bf16)) for u in subs]
    imp = []
    for u in subs:
        p_sum = lane_head(p_cmp[u], 0)
        for h in range(1, NSA_HEADS):
            p_sum = p_sum + lane_head(p_cmp[u], h)
        p_hi, p_lo = _split(p_sum)
        imp.append(_dot(ovt_ref[...], p_hi) + _dot(ovt_ref[...], p_lo))
    o_win = [finish(_dot(values(s0_w[u] // LANES, span_w // LANES, 1), p_w[u])) for u in subs]
    for u in subs:
        o_c = []
        for g in kv_groups:
            acc = _dot(values(s0_c[u] // LANES, span_c // LANES, 2 + g), p_c[u][g])
            o_c += [finish(lane_head(acc, r), jnp.exp2(sink(g, r) - m_c[u][g][r])) for r in range(rep)]
        emit(u, 1, jnp.concatenate(o_c, axis=0))

    j = lax.broadcasted_iota(jnp.int32, (n_sel, sq), 0)
    for u in subs:
        t = t0[u] + lax.broadcasted_iota(jnp.int32, (n_sel, sq), 1)
        cur = jnp.right_shift(t, SLC_SHIFT)
        forced = (j == 0) | (j == cur) | (j == cur - 1)
        score = jnp.where(forced, FORCE, jnp.where(j * SLC_BLOCK > t, -1.0, imp[u]))
        tiles = [slice(r * SUBLANES, (r + 1) * SUBLANES) for r in range(n_sel // SUBLANES)]
        rank = [jnp.zeros((SUBLANES, sq), f32) for _ in tiles]
        for jp in range(n_sel):
            other = score[jp:jp + 1, :]
            for r, rows in enumerate(tiles):
                mine = score[rows]
                if rows.stop <= jp:
                    ahead = other > mine
                elif rows.start > jp:
                    ahead = other >= mine
                else:
                    ahead = (other > mine) | ((other == mine) & (j[rows] > jp))
                rank[r] = rank[r] + jnp.where(ahead, 1.0, 0.0)
        negsel_ref[u] = jnp.where(jnp.concatenate(rank, axis=0) < float(min(SLC_TOP, n_sel)), 0.0, NEG)

    def selected(n):
        def scores(k):
            k_seg = kk_ref[0, k * seg:(k + 1) * seg, 0:HEAD_DIM]
            return [_dot(k_seg, qa[u]) for u in subs]

        def masked(k, s):
            z = []
            for u in subs:
                bias = jnp.concatenate(
                    [jnp.broadcast_to(negsel_ref[u, b:b + 1, :], (SLC_BLOCK, sq))
                     for b in range(k * (seg // SLC_BLOCK), (k + 1) * (seg // SLC_BLOCK))], axis=0)
                if k == n - 1:
                    causal = (k * seg + lax.broadcasted_iota(jnp.int32, (seg, sq), 0)
                              <= t0[u] + lax.broadcasted_iota(jnp.int32, (seg, sq), 1))
                    bias = jnp.where(causal, bias, NEG)
                z.append(jnp.concatenate([lane_head(s[u], h) + bias for h in range(NSA_HEADS)], axis=1))
            return z

        m = [jnp.full((1, NSA_HEADS * sq), NEG, f32) for _ in subs]
        acc = [jnp.zeros((VHEAD, NSA_HEADS * sq), f32) for _ in subs]
        s_next = scores(0)
        for k in range(n):
            s_cur = s_next
            if k + 1 < n:
                s_next = scores(k + 1)
            z = masked(k, s_cur)
            m_new = [jnp.maximum(m[u], jnp.max(z[u], axis=0, keepdims=True)) for u in subs]
            p = [jnp.exp2(z[u] - m_new[u]).astype(bf16) for u in subs]
            v_seg = values(k * (seg // LANES), seg // LANES, 0)
            acc = [jnp.exp2(m[u] - m_new[u]) * acc[u] + _dot(v_seg, p[u]) for u in subs]
            m = m_new
        for u in subs:
            o_slc = finish(acc[u])
            gt = gt_ref[0, :, u * sq:(u + 1) * sq]
            o_a = []
            for h in range(NSA_HEADS):
                g = lambda k: gt[NSA_BRANCHES * h + k:NSA_BRANCHES * h + k + 1, :]
                o_a.append(g(0) * lane_head(o_cmp[u], h) + g(1) * lane_head(o_slc, h)
                           + g(2) * lane_head(o_win[u], h))
            emit(u, 0, jnp.concatenate(o_a, axis=0))

    n_seg = (tile0 + tile - 1) // seg + 1
    for n in range(1, t_len // seg + 1):
        pl.when(n_seg == n)(functools.partial(selected, n))


def _attn(qt, kk, vt, kc, vct, gt, sinks, gac, ovt):
    b, _, t = qt.shape
    tq = Q_TILE
    full = lambda a: pl.BlockSpec((1,) + a.shape[1:], lambda i, j: (i,) + (0,) * (a.ndim - 1))
    feat = lambda a: pl.BlockSpec((1, a.shape[1], tq), lambda i, j: (i, 0, j))
    width = 2 * GROUP_WIDTH
    return pl.pallas_call(
        _attn_kernel,
        grid=(b, t // tq),
        in_specs=[feat(qt), full(kk), full(vt), full(kc), full(vct), feat(gt),
                  _const_spec(sinks.shape), _const_spec(gac.shape), _const_spec(ovt.shape)],
        out_specs=pl.BlockSpec((1, tq, width), lambda i, j: (i, j, 0)),
        out_shape=jax.ShapeDtypeStruct((b, t, width), bf16),
        scratch_shapes=[pltpu.VMEM((tq // SUBQ, ovt.shape[0], SUBQ), f32)],
        compiler_params=pltpu.CompilerParams(dimension_semantics=("parallel", "parallel"),
                                             vmem_limit_bytes=VMEM_LIMIT),
        name="attn",
    )(qt, kk, vt, kc, vct, gt, sinks, gac, ovt)


def _overlap_t(t_len):
    n_c = (t_len - CMP_LEN) // CMP_STRIDE + 1
    nch = t_len // CMP_STRIDE
    ci = np.arange(nch)[None, :] * CMP_STRIDE
    sj = np.arange(t_len // SLC_BLOCK)[:, None] * SLC_BLOCK
    ov = (ci <= sj + SLC_BLOCK - 1) & (ci + CMP_LEN - 1 >= sj) & (np.arange(nch)[None, :] < n_c)
    return jnp.asarray(ov, bf16)


def _mixer_weights(w_in, nsa_q, nsa_ks, nsa_kw, swa_q, swa_k):
    offs = np.concatenate([[0], np.cumsum(SPLIT_SIZES)])
    (a_q, a_kc, a_vc, a_ks, a_vs, a_kw, a_vw, a_g,
     b_b, b_c, b_x, c_q, c_k, c_v, d_v) = [w_in[:, offs[k]:offs[k + 1]] for k in range(len(SPLIT_SIZES))]
    a_g = jnp.pad(a_g, ((0, 0), (0, GT_ROWS - a_g.shape[1])))
    wt = jnp.concatenate([a_q, c_q, a_vs, a_vw, c_v, a_g], axis=1).T.astype(bf16)
    w = jnp.concatenate([a_ks, a_kw, c_k, a_kc, a_vc, b_b, b_c, b_x, d_v], axis=1).astype(bf16)
    q_scale = ATTN_SCALE * LOG2E
    gq = jnp.concatenate([jnp.tile(nsa_q * q_scale, NSA_HEADS), jnp.tile(swa_q * q_scale, SWA_HEADS)])[:, None]
    gk = jnp.concatenate([nsa_ks, nsa_kw, jnp.tile(swa_k, SWA_KV_HEADS)])[None, :]
    return wt, w, gq, gk


def _block_diag(blocks):
    n, r, c = blocks.shape
    eye = jnp.eye(n, dtype=blocks.dtype)
    return (eye[:, None, :, None] * blocks[:, :, None, :]).reshape(n * r, n * c)


def kernel(x, ffn1_norm, ffn1_w1, ffn1_w3, ffn1_w2, mix_norm, w_in, nsa_q_norm, nsa_kc_norm, nsa_ks_norm, nsa_kw_norm, cmp_pos_k, cmp_w1_k, cmp_w2_k, cmp_pos_v, cmp_w1_v, cmp_w2_v, conv_w, swa_q_norm, swa_k_norm, swa_sinks, pool_w, pool_scale, group_norm, w_out, ffn2_norm, ffn2_w1, ffn2_w3, ffn2_w2):
    b, t, d = x.shape
    depth = w_in.shape[0]
    ovt = _overlap_t(t)
    seg = jnp.asarray(np.kron(np.eye(K_COLS // HEAD_DIM), np.ones((HEAD_DIM, HEAD_DIM))), bf16)
    row = lambda v: v[None, :].astype(f32)
    cb = lambda v: v.astype(bf16)
    groups = lambda v, ids: jnp.concatenate([v[i * GROUP_WIDTH:(i + 1) * GROUP_WIDTH] for i in ids], axis=0)

    up_rows, down_rows = 256, ffn1_w2.shape[1] // 4
    ffn1 = (_to_bf16(ffn1_w1, up_rows), _to_bf16(ffn1_w3, up_rows), _to_bf16(ffn1_w2, down_rows))
    ffn2 = (_to_bf16(ffn2_w1, up_rows), _to_bf16(ffn2_w3, up_rows), _to_bf16(ffn2_w2, down_rows))
    wo = _to_bf16(w_out, up_rows)

    x = x.reshape(b * t, d)
    for l in range(depth):
        x = _ffn(x, row(ffn1_norm[l]), *ffn1, layer=l)

        wt, w, gq, gk = _mixer_weights(w_in[l], nsa_q_norm[l], nsa_ks_norm[l], nsa_kw_norm[l],
                                       swa_q_norm[l], swa_k_norm[l])
        qt, kk, vt, kvc, gt, ybd = _proj(
            x.reshape(b, t, d), row(mix_norm[l]), wt, w, seg, gq, gk, conv_w[l],
            cb(_block_diag(pool_w[l])), row(pool_scale[l]), row(groups(group_norm[l], (1, 3))))

        pos = lambda p: p.reshape(CMP_LEN // CMP_STRIDE, CMP_STRIDE * HEAD_DIM)
        kc, vct = _compress(kvc, pos(cmp_pos_k[l]), cb(cmp_w1_k[l]), cb(cmp_w2_k[l]), row(nsa_kc_norm[l]),
                            pos(cmp_pos_v[l]), cb(cmp_w1_v[l]), cb(cmp_w2_v[l].T))

        yac = _attn(qt, kk, vt, kc, vct, gt, row(swa_sinks[l] * LOG2E),
                    row(groups(group_norm[l], (0, 2))), ovt)

        mix = (yac.reshape(b * t, -1), ybd.reshape(b * t, -1), wo)
        x = _ffn(x, row(ffn2_norm[l]), *ffn2, layer=l, mix=mix)
    return x.reshape(b, t, d)
```

```python
import functools

import numpy as np
import jax
import jax.numpy as jnp
from jax import lax
from jax.experimental import pallas as pl
from jax.experimental.pallas import tpu as pltpu

HEAD_DIM = 64
GROUP_WIDTH = 256
N_GROUPS = 4
NSA_HEADS = 4
NSA_BRANCHES = 3
CMP_LEN = 32
CMP_STRIDE = 16
SLC_BLOCK = 64
SLC_TOP = 8
NSA_WINDOW = 512
SWA_HEADS = 4
SWA_KV_HEADS = 2
SWA_WINDOW = 128
POOL_WINDOWS = (2, 4, 8, 16)
Q_TILE = 256
SUBQ = 128
EPS = 1e-6
NEG = -1e30
FORCE = 1e4
ATTN_SCALE = HEAD_DIM ** -0.5
LOG2E = float(np.log2(np.e))
SLC_SEG = 512
SLC_SHIFT = SLC_BLOCK.bit_length() - 1
POOL_SHIFT = (GROUP_WIDTH // len(POOL_WINDOWS)).bit_length() - 1

LANES = 128
SUBLANES = 8
CONV_HALO = 8
POOL_HALO = 16
VMEM_LIMIT = 56 * 2 ** 20

QT_ROWS = 512
V_HEADS = 4
VHEAD = 2 * HEAD_DIM
VT_ROWS = V_HEADS * VHEAD
GT_ROWS = 2 * SUBLANES
T_ROWS = QT_ROWS + V_HEADS * HEAD_DIM + GT_ROWS
K_COLS = 256
KC_COLS = 128
B_COLS = 768
D_COLS = 256
OFF_K = 0
OFF_KC = OFF_K + K_COLS
OFF_B = OFF_KC + KC_COLS
OFF_D = OFF_B + B_COLS
W_COLS = OFF_D + D_COLS

SPLIT_SIZES = (256, 64, 64, 64, 64, 64, 64, 12, 256, 256, 256, 256, 128, 128, 256)

f32 = jnp.float32
bf16 = jnp.bfloat16


def _rms(x, g):
    return x * lax.rsqrt(jnp.mean(x * x, axis=-1, keepdims=True) + EPS) * g


def _dot(a, b):
    return jnp.dot(a, b, preferred_element_type=f32)


def _dot_nt(a, b):
    return lax.dot_general(a, b, (((1,), (1,)), ((), ())), preferred_element_type=f32)


def _split(a):
    hi = a.astype(bf16)
    return hi, (a - hi.astype(f32)).astype(bf16)


def _const_spec(shape):
    nd = len(shape)
    return pl.BlockSpec(shape, lambda *_: (0,) * nd, pipeline_mode=pl.Buffered(1))


def _layer_spec(stacked, layer):
    return pl.BlockSpec((None,) + stacked.shape[1:], lambda *_: (layer, 0, 0), pipeline_mode=pl.Buffered(1))


def _cast_kernel(w_ref, o_ref):
    o_ref[...] = w_ref[...].astype(bf16)


def _to_bf16(w, rows):
    depth, r, c = w.shape
    spec = pl.BlockSpec((1, rows, c), lambda l, i: (l, i, 0))
    return pl.pallas_call(
        _cast_kernel,
        grid=(depth, r // rows),
        in_specs=[spec],
        out_specs=spec,
        out_shape=jax.ShapeDtypeStruct(w.shape, bf16),
        compiler_params=pltpu.CompilerParams(dimension_semantics=("parallel", "parallel"),
                                             vmem_limit_bytes=VMEM_LIMIT),
        name="cast",
    )(w)


def _ffn_kernel(*refs, with_mix):
    if with_mix:
        x_ref, ya_ref, yb_ref, wo_ref, g_ref, w1_ref, w3_ref, w2_ref, o_ref = refs
        x = x_ref[0]
        for gi, y_ref in ((0, ya_ref), (1, yb_ref)):
            for half in range(2):
                rows = slice((gi + 2 * half) * GROUP_WIDTH, (gi + 2 * half + 1) * GROUP_WIDTH)
                x = x + _dot(y_ref[0, :, half * GROUP_WIDTH:(half + 1) * GROUP_WIDTH], wo_ref[rows, :])
    else:
        x_ref, g_ref, w1_ref, w3_ref, w2_ref, o_ref = refs
        x = x_ref[0]
    h = _rms(x, g_ref[...]).astype(bf16)
    a = _dot(h, w1_ref[...])
    b = _dot(h, w3_ref[...])
    act = (a * jax.nn.sigmoid(a) * b).astype(bf16)
    o_ref[0] = x + 0.5 * _dot(act, w2_ref[...])


def _ffn(x, gain, w1, w3, w2, layer, mix=None, tm=512):
    b, t, d = x.shape
    row = lambda w: pl.BlockSpec((1, tm, w), lambda i, j: (i, j, 0))
    args, specs = [x], [row(d)]
    if mix is not None:
        ya, yb, wo = mix
        args += [ya, yb, wo]
        specs += [row(ya.shape[2]), row(yb.shape[2]), _layer_spec(wo, layer)]
    args += [gain, w1, w3, w2]
    specs += [_const_spec((1, d))] + [_layer_spec(w, layer) for w in (w1, w3, w2)]
    return pl.pallas_call(
        functools.partial(_ffn_kernel, with_mix=mix is not None),
        grid=(b, t // tm),
        in_specs=specs,
        out_specs=row(d),
        out_shape=jax.ShapeDtypeStruct((b, t, d), f32),
        compiler_params=pltpu.CompilerParams(dimension_semantics=("parallel", "parallel"),
                                             vmem_limit_bytes=VMEM_LIMIT),
        name="ffn" if mix is None else "ffn_mix",
    )(*args)


def _proj_kernel(x_ref, g_ref, wt_ref, w_ref, seg_ref, gq_ref, gk_ref, cw_ref, pw_ref, ps_ref, gbd_ref,
                 qt_ref, kk_ref, vt_ref, kc_ref, gt_ref, ybd_ref,
                 zhalo_ref, vhalo_ref):
    tt = x_ref.shape[1]
    ti = pl.program_id(1)

    @pl.when(ti == 0)
    def _():
        zhalo_ref[...] = jnp.zeros_like(zhalo_ref)
        vhalo_ref[...] = jnp.zeros_like(vhalo_ref)

    h = _rms(x_ref[0], g_ref[...]).astype(bf16)

    ut = _dot_nt(wt_ref[...], h)
    u_k = _dot(h, w_ref[:, OFF_K:OFF_K + K_COLS])
    ub = _dot(h, w_ref[:, OFF_B:OFF_B + B_COLS])
    v = _dot(h, w_ref[:, OFF_D:OFF_D + D_COLS])
    kc_ref[0] = _dot(h, w_ref[:, OFF_KC:OFF_KC + KC_COLS])

    for hd in range(QT_ROWS // HEAD_DIM):
        rows = slice(hd * HEAD_DIM, (hd + 1) * HEAD_DIM)
        u = ut[rows]
        ms = jnp.mean(u * u, axis=0, keepdims=True)
        qt_ref[0, rows, :] = (u * lax.rsqrt(ms + EPS) * gq_ref[rows, :]).astype(bf16)
    ones = jnp.ones((HEAD_DIM, LANES), bf16)
    for hv in range(V_HEADS):
        vh = ut[QT_ROWS + hv * HEAD_DIM:QT_ROWS + (hv + 1) * HEAD_DIM].astype(bf16)
        for c in range(tt // LANES):
            vt_ref[0, c, hv * VHEAD:hv * VHEAD + HEAD_DIM, :] = vh[:, c * LANES:(c + 1) * LANES]
            vt_ref[0, c, hv * VHEAD + HEAD_DIM:(hv + 1) * VHEAD, :] = ones
    gt_ref[0] = jax.nn.sigmoid(ut[QT_ROWS + V_HEADS * HEAD_DIM:])

    hi, lo = _split(u_k * u_k)
    ss = _dot(hi, seg_ref[...]) + _dot(lo, seg_ref[...])
    kk_ref[0] = (u_k * lax.rsqrt(ss * (1.0 / HEAD_DIM) + EPS) * gk_ref[...]).astype(bf16)

    s = jnp.concatenate([vhalo_ref[...], v], axis=0)
    grp = jnp.right_shift(lax.broadcasted_iota(jnp.int32, (1, D_COLS), 1), POOL_SHIFT)
    acc = None
    for k, w in enumerate(POOL_WINDOWS):
        s = s + pltpu.roll(s, w // 2, 0)
        acc = s if acc is None else jnp.where(grp >= k, s, acc)
    win = jnp.where(grp == 0, POOL_WINDOWS[0],
                    jnp.where(grp == 1, POOL_WINDOWS[1],
                              jnp.where(grp == 2, POOL_WINDOWS[2], POOL_WINDOWS[3])))
    t = ti * tt + lax.broadcasted_iota(jnp.int32, (tt, 1), 0)
    cnt = jnp.minimum(t + 1, win).astype(f32)
    d = acc[POOL_HALO:] / cnt - v
    o_d = _dot(d.astype(bf16), pw_ref[...]) * ps_ref[...]
    vhalo_ref[...] = v[tt - POOL_HALO:]

    z = ub[:, GROUP_WIDTH:2 * GROUP_WIDTH] * ub[:, 2 * GROUP_WIDTH:]
    ze = jnp.concatenate([zhalo_ref[...], z], axis=0)
    z1 = pltpu.roll(ze, 1, 0)[CONV_HALO:]
    z2 = pltpu.roll(ze, 2, 0)[CONV_HALO:]
    conv = cw_ref[0:1, :] * z2 + cw_ref[1:2, :] * z1 + cw_ref[2:3, :] * z
    o_b = ub[:, :GROUP_WIDTH] * conv
    zhalo_ref[...] = z[tt - CONV_HALO:]

    ybd_ref[0, :, :GROUP_WIDTH] = _rms(o_b, gbd_ref[:, :GROUP_WIDTH]).astype(bf16)
    ybd_ref[0, :, GROUP_WIDTH:] = _rms(o_d, gbd_ref[:, GROUP_WIDTH:]).astype(bf16)


def _proj(x3, gain, wt, w, seg, gq, gk, conv_w, pool_w, pool_scale, gbd, tt=1024):
    b, t, d = x3.shape
    tok = lambda w_: pl.BlockSpec((1, tt, w_), lambda i, j: (i, j, 0))
    feat = lambda r: pl.BlockSpec((1, r, tt), lambda i, j: (i, 0, j))
    consts = [gain, wt, w, seg, gq, gk, conv_w, pool_w, pool_scale, gbd]
    return pl.pallas_call(
        _proj_kernel,
        grid=(b, t // tt),
        in_specs=[tok(d)] + [_const_spec(c.shape) for c in consts],
        out_specs=[feat(QT_ROWS), tok(K_COLS),
                   pl.BlockSpec((1, tt // LANES, VT_ROWS, LANES), lambda i, j: (i, j, 0, 0)),
                   tok(KC_COLS), feat(GT_ROWS), tok(2 * GROUP_WIDTH)],
        out_shape=[jax.ShapeDtypeStruct((b, QT_ROWS, t), bf16),
                   jax.ShapeDtypeStruct((b, t, K_COLS), bf16),
                   jax.ShapeDtypeStruct((b, t // LANES, VT_ROWS, LANES), bf16),
                   jax.ShapeDtypeStruct((b, t, KC_COLS), f32),
                   jax.ShapeDtypeStruct((b, GT_ROWS, t), f32),
                   jax.ShapeDtypeStruct((b, t, 2 * GROUP_WIDTH), bf16)],
        scratch_shapes=[pltpu.VMEM((CONV_HALO, GROUP_WIDTH), f32), pltpu.VMEM((POOL_HALO, D_COLS), f32)],
        compiler_params=pltpu.CompilerParams(dimension_semantics=("parallel", "arbitrary"),
                                             vmem_limit_bytes=VMEM_LIMIT),
        name="proj",
    )(x3, *consts)


def _compress_kernel(kvc_ref, pk_ref, w1k_ref, w2k_ref, gk_ref, pv_ref, w1v_ref, w2vt_ref,
                     kc_ref, vct_ref):
    nch = kvc_ref.shape[1] // CMP_STRIDE
    half = CMP_STRIDE * HEAD_DIM
    rows = [kvc_ref[0, pl.ds(r, nch, stride=CMP_STRIDE), :] for r in range(CMP_STRIDE)]
    k_chunks = jnp.concatenate([x[:, :HEAD_DIM] for x in rows], axis=1)
    v_chunks = jnp.concatenate([x[:, HEAD_DIM:] for x in rows], axis=1)

    def hidden(c, pos_ref, w1_ref):
        ha = _dot((c + pos_ref[0:1, :]).astype(bf16), w1_ref[0:half, :])
        hb = _dot((c + pos_ref[1:2, :]).astype(bf16), w1_ref[half:2 * half, :])
        return jax.nn.gelu(ha + pltpu.roll(hb, nch - 1, 0)).astype(bf16)

    kc_ref[0] = _rms(_dot(hidden(k_chunks, pk_ref, w1k_ref), w2k_ref[...]), gk_ref[...]).astype(bf16)
    vct_ref[0] = _dot_nt(w2vt_ref[...], hidden(v_chunks, pv_ref, w1v_ref)).astype(bf16)


def _compress(kvc, pk, w1k, w2k, gk, pv, w1v, w2vt):
    b, t, width = kvc.shape
    nch = t // CMP_STRIDE
    consts = [pk, w1k, w2k, gk, pv, w1v, w2vt]
    return pl.pallas_call(
        _compress_kernel,
        grid=(b,),
        in_specs=[pl.BlockSpec((1, t, width), lambda i: (i, 0, 0))] + [_const_spec(c.shape) for c in consts],
        out_specs=[pl.BlockSpec((1, nch, HEAD_DIM), lambda i: (i, 0, 0)),
                   pl.BlockSpec((1, HEAD_DIM, nch), lambda i: (i, 0, 0))],
        out_shape=[jax.ShapeDtypeStruct((b, nch, HEAD_DIM), bf16),
                   jax.ShapeDtypeStruct((b, HEAD_DIM, nch), bf16)],
        compiler_params=pltpu.CompilerParams(dimension_semantics=("parallel",),
                                             vmem_limit_bytes=VMEM_LIMIT),
        name="compress",
    )(kvc, *consts)


def _attn_kernel(qt_ref, kk_ref, vt_ref, kc_ref, vct_ref, gt_ref, sink_ref, gac_ref, ovt_ref,
                 y_ref, negsel_ref):
    tile = qt_ref.shape[2]
    sq = SUBQ
    subs = range(tile // sq)
    t_len = kk_ref.shape[1]
    nch = kc_ref.shape[1]
    n_sel = ovt_ref.shape[0]
    seg = min(SLC_SEG, t_len)
    rep = SWA_HEADS // SWA_KV_HEADS
    kv_groups = range(SWA_KV_HEADS)
    tile0 = pl.program_id(1) * tile
    t0 = [tile0 + u * sq for u in subs]
    lane_head = lambda a, h: a[:, h * sq:(h + 1) * sq]

    def stack_heads(u, first, count):
        return jnp.concatenate(
            [qt_ref[0, (first + h) * HEAD_DIM:(first + h + 1) * HEAD_DIM, u * sq:(u + 1) * sq]
             for h in range(count)], axis=1)

    def values(chunk0, n_chunks, head):
        rows = slice(head * VHEAD, (head + 1) * VHEAD)
        return jnp.concatenate([vt_ref[0, chunk0 + c, rows, :] for c in range(n_chunks)], axis=1)

    def finish(acc, extra=None):
        den = acc[HEAD_DIM:]
        if extra is not None:
            den = den + extra
        return acc[:HEAD_DIM] / den

    def band_bias(u, s0, span, window):
        diff = (t0[u] + lax.broadcasted_iota(jnp.int32, (span, sq), 1)
                - (s0 + lax.broadcasted_iota(jnp.int32, (span, sq), 0)))
        return jnp.where((diff >= 0) & (diff < window), 0.0, NEG)

    def emit(u, gi, o):
        cols = slice(gi * GROUP_WIDTH, (gi + 1) * GROUP_WIDTH)
        y_ref[0, u * sq:(u + 1) * sq, cols] = _rms(o.T, gac_ref[:, cols]).astype(bf16)

    sink = lambda g, r: sink_ref[:, g * rep + r:g * rep + r + 1]
    span_w = min(NSA_WINDOW + sq, t_len)
    span_c = min(SWA_WINDOW + sq, t_len)
    s0_w = [pl.multiple_of(jnp.maximum(t0[u] - NSA_WINDOW, 0), sq) for u in subs]
    s0_c = [pl.multiple_of(jnp.maximum(t0[u] - SWA_WINDOW, 0), sq) for u in subs]

    qa = [stack_heads(u, 0, NSA_HEADS) for u in subs]
    s_cmp = [_dot(kc_ref[0], qa[u]) for u in subs]
    s_w = [_dot(kk_ref[0, pl.ds(s0_w[u], span_w), HEAD_DIM:2 * HEAD_DIM], qa[u]) for u in subs]
    s_c = [[_dot(kk_ref[0, pl.ds(s0_c[u], span_c), (2 + g) * HEAD_DIM:(3 + g) * HEAD_DIM],
                 stack_heads(u, NSA_HEADS + g * rep, rep)) for g in kv_groups] for u in subs]

    n_i = lax.broadcasted_iota(jnp.int32, (nch, NSA_HEADS * sq), 0)
    c_lane = lax.broadcasted_iota(jnp.int32, (nch, NSA_HEADS * sq), 1) & (sq - 1)
    c_valid = [(n_i * CMP_STRIDE + (CMP_LEN - 1) <= t0[u] + c_lane) & (n_i < nch - 1) for u in subs]
    z_cmp = [jnp.where(c_valid[u], s_cmp[u], NEG) for u in subs]
    m_cmp = [jnp.max(z_cmp[u], axis=0, keepdims=True) for u in subs]
    z_w, m_w, z_c, m_c = [], [], [], []
    for u in subs:
        bias = band_bias(u, s0_w[u], span_w, NSA_WINDOW)
        z_w.append(jnp.concatenate([lane_head(s_w[u], h) + bias for h in range(NSA_HEADS)], axis=1))
        m_w.append(jnp.max(z_w[u], axis=0, keepdims=True))
        bias = band_bias(u, s0_c[u], span_c, SWA_WINDOW)
        z_c.append([[lane_head(s_c[u][g], r) + bias for r in range(rep)] for g in kv_groups])
        m_c.append([[jnp.maximum(jnp.max(z_c[u][g][r], axis=0, keepdims=True), sink(g, r))
                     for r in range(rep)] for g in kv_groups])

    e_cmp = [jnp.exp2(z_cmp[u] - m_cmp[u]) for u in subs]
    p_w = [jnp.exp2(z_w[u] - m_w[u]).astype(bf16) for u in subs]
    p_c = [[jnp.concatenate([jnp.exp2(z_c[u][g][r] - m_c[u][g][r]).astype(bf16) for r in range(rep)], axis=1)
            for g in kv_groups] for u in subs]

    p_cmp = [jnp.where(c_valid[u], e_cmp[u] / jnp.sum(e_cmp[u], axis=0, keepdims=True), 0.0) for u in subs]
    o_cmp = [_dot(vct_ref[0], p_cmp[u].astype(bf16)) for u in subs]
    imp = []
    for u in subs:
        p_sum = lane_head(p_cmp[u], 0)
        for h in range(1, NSA_HEADS):
            p_sum = p_sum + lane_head(p_cmp[u], h)
        p_hi, p_lo = _split(p_sum)
        imp.append(_dot(ovt_ref[...], p_hi) + _dot(ovt_ref[...], p_lo))
    o_win = [finish(_dot(values(s0_w[u] // LANES, span_w // LANES, 1), p_w[u])) for u in subs]
    for u in subs:
        o_c = []
        for g in kv_groups:
            acc = _dot(values(s0_c[u] // LANES, span_c // LANES, 2 + g), p_c[u][g])
            o_c += [finish(lane_head(acc, r), jnp.exp2(sink(g, r) - m_c[u][g][r])) for r in range(rep)]
        emit(u, 1, jnp.concatenate(o_c, axis=0))

    j = lax.broadcasted_iota(jnp.int32, (n_sel, sq), 0)
    for u in subs:
        t = t0[u] + lax.broadcasted_iota(jnp.int32, (n_sel, sq), 1)
        cur = jnp.right_shift(t, SLC_SHIFT)
        forced = (j == 0) | (j == cur) | (j == cur - 1)
        score = jnp.where(forced, FORCE, jnp.where(j * SLC_BLOCK > t, -1.0, imp[u]))
        tiles = [slice(r * SUBLANES, (r + 1) * SUBLANES) for r in range(n_sel // SUBLANES)]
        rank = [jnp.zeros((SUBLANES, sq), f32) for _ in tiles]
        for jp in range(n_sel):
            other = score[jp:jp + 1, :]
            for r, rows in enumerate(tiles):
                mine = score[rows]
                if rows.stop <= jp:
                    ahead = other > mine
                elif rows.start > jp:
                    ahead = other >= mine
                else:
                    ahead = (other > mine) | ((other == mine) & (j[rows] > jp))
                rank[r] = rank[r] + jnp.where(ahead, 1.0, 0.0)
        negsel_ref[u] = jnp.where(jnp.concatenate(rank, axis=0) < float(min(SLC_TOP, n_sel)), 0.0, NEG)

    def selected(n):
        def scores(k):
            k_seg = kk_ref[0, k * seg:(k + 1) * seg, 0:HEAD_DIM]
            return [_dot(k_seg, qa[u]) for u in subs]

        def masked(k, s):
            z = []
            for u in subs:
                bias = jnp.concatenate(
                    [jnp.broadcast_to(negsel_ref[u, b:b + 1, :], (SLC_BLOCK, sq))
                     for b in range(k * (seg // SLC_BLOCK), (k + 1) * (seg // SLC_BLOCK))], axis=0)
                if k == n - 1:
                    causal = (k * seg + lax.broadcasted_iota(jnp.int32, (seg, sq), 0)
                              <= t0[u] + lax.broadcasted_iota(jnp.int32, (seg, sq), 1))
                    bias = jnp.where(causal, bias, NEG)
                z.append(jnp.concatenate([lane_head(s[u], h) + bias for h in range(NSA_HEADS)], axis=1))
            return z

        m = [jnp.full((1, NSA_HEADS * sq), NEG, f32) for _ in subs]
        acc = [jnp.zeros((VHEAD, NSA_HEADS * sq), f32) for _ in subs]
        s_next = scores(0)
        for k in range(n):
            s_cur = s_next
            if k + 1 < n:
                s_next = scores(k + 1)
            z = masked(k, s_cur)
            m_new = [jnp.maximum(m[u], jnp.max(z[u], axis=0, keepdims=True)) for u in subs]
            p = [jnp.exp2(z[u] - m_new[u]).astype(bf16) for u in subs]
            v_seg = values(k * (seg // LANES), seg // LANES, 0)
            acc = [jnp.exp2(m[u] - m_new[u]) * acc[u] + _dot(v_seg, p[u]) for u in subs]
            m = m_new
        for u in subs:
            o_slc = finish(acc[u])
            gt = gt_ref[0, :, u * sq:(u + 1) * sq]
            o_a = []
            for h in range(NSA_HEADS):
                g = lambda k: gt[NSA_BRANCHES * h + k:NSA_BRANCHES * h + k + 1, :]
                o_a.append(g(0) * lane_head(o_cmp[u], h) + g(1) * lane_head(o_slc, h)
                           + g(2) * lane_head(o_win[u], h))
            emit(u, 0, jnp.concatenate(o_a, axis=0))

    n_seg = (tile0 + tile - 1) // seg + 1
    for n in range(1, t_len // seg + 1):
        pl.when(n_seg == n)(functools.partial(selected, n))


def _attn(qt, kk, vt, kc, vct, gt, sinks, gac, ovt):
    b, _, t = qt.shape
    tq = Q_TILE
    full = lambda a: pl.BlockSpec((1,) + a.shape[1:], lambda i, j: (i,) + (0,) * (a.ndim - 1))
    feat = lambda a: pl.BlockSpec((1, a.shape[1], tq), lambda i, j: (i, 0, j))
    width = 2 * GROUP_WIDTH
    return pl.pallas_call(
        _attn_kernel,
        grid=(b, t // tq),
        in_specs=[feat(qt), full(kk), full(vt), full(kc), full(vct), feat(gt),
                  _const_spec(sinks.shape), _const_spec(gac.shape), _const_spec(ovt.shape)],
        out_specs=pl.BlockSpec((1, tq, width), lambda i, j: (i, j, 0)),
        out_shape=jax.ShapeDtypeStruct((b, t, width), bf16),
        scratch_shapes=[pltpu.VMEM((tq // SUBQ, ovt.shape[0], SUBQ), f32)],
        compiler_params=pltpu.CompilerParams(dimension_semantics=("parallel", "parallel"),
                                             vmem_limit_bytes=VMEM_LIMIT),
        name="attn",
    )(qt, kk, vt, kc, vct, gt, sinks, gac, ovt)


def _overlap_t(t_len):
    n_c = (t_len - CMP_LEN) // CMP_STRIDE + 1
    nch = t_len // CMP_STRIDE
    ci = np.arange(nch)[None, :] * CMP_STRIDE
    sj = np.arange(t_len // SLC_BLOCK)[:, None] * SLC_BLOCK
    ov = (ci <= sj + SLC_BLOCK - 1) & (ci + CMP_LEN - 1 >= sj) & (np.arange(nch)[None, :] < n_c)
    return jnp.asarray(ov, bf16)


def _mixer_weights(w_in, nsa_q, nsa_ks, nsa_kw, swa_q, swa_k):
    offs = np.concatenate([[0], np.cumsum(SPLIT_SIZES)])
    (a_q, a_kc, a_vc, a_ks, a_vs, a_kw, a_vw, a_g,
     b_b, b_c, b_x, c_q, c_k, c_v, d_v) = [w_in[:, offs[k]:offs[k + 1]] for k in range(len(SPLIT_SIZES))]
    a_g = jnp.pad(a_g, ((0, 0), (0, GT_ROWS - a_g.shape[1])))
    wt = jnp.concatenate([a_q, c_q, a_vs, a_vw, c_v, a_g], axis=1).T.astype(bf16)
    w = jnp.concatenate([a_ks, a_kw, c_k, a_kc, a_vc, b_b, b_c, b_x, d_v], axis=1).astype(bf16)
    q_scale = ATTN_SCALE * LOG2E
    gq = jnp.concatenate([jnp.tile(nsa_q * q_scale, NSA_HEADS), jnp.tile(swa_q * q_scale, SWA_HEADS)])[:, None]
    gk = jnp.concatenate([nsa_ks, nsa_kw, jnp.tile(swa_k, SWA_KV_HEADS)])[None, :]
    return wt, w, gq, gk


def _block_diag(blocks):
    n, r, c = blocks.shape
    eye = jnp.eye(n, dtype=blocks.dtype)
    return (eye[:, None, :, None] * blocks[:, :, None, :]).reshape(n * r, n * c)


def kernel(x, ffn1_norm, ffn1_w1, ffn1_w3, ffn1_w2, mix_norm, w_in, nsa_q_norm, nsa_kc_norm, nsa_ks_norm, nsa_kw_norm, cmp_pos_k, cmp_w1_k, cmp_w2_k, cmp_pos_v, cmp_w1_v, cmp_w2_v, conv_w, swa_q_norm, swa_k_norm, swa_sinks, pool_w, pool_scale, group_norm, w_out, ffn2_norm, ffn2_w1, ffn2_w3, ffn2_w2):
    b, t, d = x.shape
    depth = w_in.shape[0]
    ovt = _overlap_t(t)
    seg = jnp.asarray(np.kron(np.eye(K_COLS // HEAD_DIM), np.ones((HEAD_DIM, HEAD_DIM))), bf16)
    row = lambda v: v[None, :].astype(f32)
    cb = lambda v: v.astype(bf16)
    groups = lambda v, ids: jnp.concatenate([v[i * GROUP_WIDTH:(i + 1) * GROUP_WIDTH] for i in ids], axis=0)

    up_rows, down_rows = 256, ffn1_w2.shape[1] // 4
    ffn1 = (_to_bf16(ffn1_w1, up_rows), _to_bf16(ffn1_w3, up_rows), _to_bf16(ffn1_w2, down_rows))
    ffn2 = (_to_bf16(ffn2_w1, up_rows), _to_bf16(ffn2_w3, up_rows), _to_bf16(ffn2_w2, down_rows))
    wo = _to_bf16(w_out, up_rows)

    for l in range(depth):
        x = _ffn(x, row(ffn1_norm[l]), *ffn1, layer=l)

        wt, w, gq, gk = _mixer_weights(w_in[l], nsa_q_norm[l], nsa_ks_norm[l], nsa_kw_norm[l],
                                       swa_q_norm[l], swa_k_norm[l])
        qt, kk, vt, kvc, gt, ybd = _proj(
            x, row(mix_norm[l]), wt, w, seg, gq, gk, conv_w[l],
            cb(_block_diag(pool_w[l])), row(pool_scale[l]), row(groups(group_norm[l], (1, 3))))

        pos = lambda p: p.reshape(CMP_LEN // CMP_STRIDE, CMP_STRIDE * HEAD_DIM)
        kc, vct = _compress(kvc, pos(cmp_pos_k[l]), cb(cmp_w1_k[l]), cb(cmp_w2_k[l]), row(nsa_kc_norm[l]),
                            pos(cmp_pos_v[l]), cb(cmp_w1_v[l]), cb(cmp_w2_v[l].T))

        yac = _attn(qt, kk, vt, kc, vct, gt, row(swa_sinks[l] * LOG2E),
                    row(groups(group_norm[l], (0, 2))), ovt)

        mix = (yac, ybd, wo)
        x = _ffn(x, row(ffn2_norm[l]), *ffn2, layer=l, mix=mix)
    return x
```

```python
import functools

import numpy as np
import jax
import jax.numpy as jnp
from jax import lax
from jax.experimental import pallas as pl
from jax.experimental.pallas import tpu as pltpu

HEAD_DIM = 64
GROUP_WIDTH = 256
N_GROUPS = 4
NSA_HEADS = 4
NSA_BRANCHES = 3
CMP_LEN = 32
CMP_STRIDE = 16
SLC_BLOCK = 64
SLC_TOP = 8
NSA_WINDOW = 512
SWA_HEADS = 4
SWA_KV_HEADS = 2
SWA_WINDOW = 128
POOL_WINDOWS = (2, 4, 8, 16)
Q_TILE = 256
SUBQ = 128
EPS = 1e-6
NEG = -1e30
FORCE = 1e4
ATTN_SCALE = HEAD_DIM ** -0.5
LOG2E = float(np.log2(np.e))
SLC_SEG = 512
SLC_SHIFT = SLC_BLOCK.bit_length() - 1
POOL_SHIFT = (GROUP_WIDTH // len(POOL_WINDOWS)).bit_length() - 1

LANES = 128
SUBLANES = 8
CONV_HALO = 8
POOL_HALO = 16
VMEM_LIMIT = 56 * 2 ** 20

QT_ROWS = 512
V_HEADS = 4
VHEAD = 2 * HEAD_DIM
VT_ROWS = V_HEADS * VHEAD
GT_ROWS = 2 * SUBLANES
T_ROWS = QT_ROWS + V_HEADS * HEAD_DIM + GT_ROWS
K_COLS = 256
KC_COLS = 128
B_COLS = 768
D_COLS = 256
OFF_K = 0
OFF_KC = OFF_K + K_COLS
OFF_B = OFF_KC + KC_COLS
OFF_D = OFF_B + B_COLS
W_COLS = OFF_D + D_COLS

SPLIT_SIZES = (256, 64, 64, 64, 64, 64, 64, 12, 256, 256, 256, 256, 128, 128, 256)

f32 = jnp.float32
bf16 = jnp.bfloat16


def _rms(x, g):
    return x * lax.rsqrt(jnp.mean(x * x, axis=-1, keepdims=True) + EPS) * g


def _dot(a, b):
    return jnp.dot(a, b, preferred_element_type=f32)


def _dot_nt(a, b):
    return lax.dot_general(a, b, (((1,), (1,)), ((), ())), preferred_element_type=f32)


def _split(a):
    hi = a.astype(bf16)
    return hi, (a - hi.astype(f32)).astype(bf16)


def _const_spec(shape):
    nd = len(shape)
    return pl.BlockSpec(shape, lambda *_: (0,) * nd, pipeline_mode=pl.Buffered(1))


def _layer_spec(stacked, layer):
    return pl.BlockSpec((None,) + stacked.shape[1:], lambda *_: (layer, 0, 0), pipeline_mode=pl.Buffered(1))


def _cast_kernel(w_ref, o_ref):
    o_ref[...] = w_ref[...].astype(bf16)


def _to_bf16(w, rows):
    depth, r, c = w.shape
    spec = pl.BlockSpec((1, rows, c), lambda l, i: (l, i, 0))
    return pl.pallas_call(
        _cast_kernel,
        grid=(depth, r // rows),
        in_specs=[spec],
        out_specs=spec,
        out_shape=jax.ShapeDtypeStruct(w.shape, bf16),
        compiler_params=pltpu.CompilerParams(dimension_semantics=("parallel", "parallel"),
                                             vmem_limit_bytes=VMEM_LIMIT),
        name="cast",
    )(w)


def _arrange_kernel(w_ref, tok_ref, feat_ref):
    w = w_ref[0]
    offs = np.concatenate([[0], np.cumsum(SPLIT_SIZES)])
    (a_q, a_kc, a_vc, a_ks, a_vs, a_kw, a_vw, a_g,
     b_b, b_c, b_x, c_q, c_k, c_v, d_v) = [w[:, offs[k]:offs[k + 1]] for k in range(len(SPLIT_SIZES))]
    tok_ref[0] = jnp.concatenate([a_ks, a_kw, c_k, a_kc, a_vc, b_b, b_c, b_x, d_v], axis=1).astype(bf16)
    feat = [a_q, c_q, a_vs, a_vw, c_v, a_g]
    width = sum(p.shape[1] for p in feat)
    padded = -(-width // LANES) * LANES
    feat = jnp.concatenate(feat + [jnp.zeros((w.shape[0], padded - width), f32)], axis=1)
    feat_ref[0] = feat.T[:T_ROWS].astype(bf16)


def _arrange_w_in(w_in):
    depth, d, width = w_in.shape
    return pl.pallas_call(
        _arrange_kernel,
        grid=(depth,),
        in_specs=[pl.BlockSpec((1, d, width), lambda l: (l, 0, 0))],
        out_specs=[pl.BlockSpec((1, d, W_COLS), lambda l: (l, 0, 0)),
                   pl.BlockSpec((1, T_ROWS, d), lambda l: (l, 0, 0))],
        out_shape=[jax.ShapeDtypeStruct((depth, d, W_COLS), bf16),
                   jax.ShapeDtypeStruct((depth, T_ROWS, d), bf16)],
        compiler_params=pltpu.CompilerParams(dimension_semantics=("parallel",),
                                             vmem_limit_bytes=VMEM_LIMIT),
        name="arrange_w_in",
    )(w_in)


def _ffn_kernel(*refs, with_mix):
    if with_mix:
        x_ref, ya_ref, yb_ref, wo_ref, g_ref, w1_ref, w3_ref, w2_ref, o_ref = refs
        x = x_ref[0]
        for gi, y_ref in ((0, ya_ref), (1, yb_ref)):
            for half in range(2):
                rows = slice((gi + 2 * half) * GROUP_WIDTH, (gi + 2 * half + 1) * GROUP_WIDTH)
                x = x + _dot(y_ref[0, :, half * GROUP_WIDTH:(half + 1) * GROUP_WIDTH], wo_ref[rows, :])
    else:
        x_ref, g_ref, w1_ref, w3_ref, w2_ref, o_ref = refs
        x = x_ref[0]
    halves = [slice(0, x.shape[0] // 2), slice(x.shape[0] // 2, x.shape[0])]
    hs = [_rms(x[r], g_ref[...]).astype(bf16) for r in halves]
    a = [_dot(h, w1_ref[...]) for h in hs]
    b = [_dot(h, w3_ref[...]) for h in hs]
    act = [(a[i] * jax.nn.sigmoid(a[i]) * b[i]).astype(bf16) for i in range(len(halves))]
    for i, r in enumerate(halves):
        o_ref[0, r, :] = x[r] + 0.5 * _dot(act[i], w2_ref[...])


def _ffn(x, gain, w1, w3, w2, layer, mix=None, tm=512):
    b, t, d = x.shape
    row = lambda w: pl.BlockSpec((1, tm, w), lambda i, j: (i, j, 0))
    args, specs = [x], [row(d)]
    if mix is not None:
        ya, yb, wo = mix
        args += [ya, yb, wo]
        specs += [row(ya.shape[2]), row(yb.shape[2]), _layer_spec(wo, layer)]
    args += [gain, w1, w3, w2]
    specs += [_const_spec((1, d))] + [_layer_spec(w, layer) for w in (w1, w3, w2)]
    return pl.pallas_call(
        functools.partial(_ffn_kernel, with_mix=mix is not None),
        grid=(b, t // tm),
        in_specs=specs,
        out_specs=row(d),
        out_shape=jax.ShapeDtypeStruct((b, t, d), f32),
        compiler_params=pltpu.CompilerParams(dimension_semantics=("parallel", "parallel"),
                                             vmem_limit_bytes=VMEM_LIMIT),
        name="ffn" if mix is None else "ffn_mix",
    )(*args)


def _proj_kernel(x_ref, g_ref, wt_ref, w_ref, seg_ref, gq_ref, gk_ref, cw_ref, pw_ref, ps_ref, gbd_ref,
                 qt_ref, kk_ref, vt_ref, kc_ref, gt_ref, ybd_ref,
                 zhalo_ref, vhalo_ref):
    tt = x_ref.shape[1]
    ti = pl.program_id(1)

    @pl.when(ti == 0)
    def _():
        zhalo_ref[...] = jnp.zeros_like(zhalo_ref)
        vhalo_ref[...] = jnp.zeros_like(vhalo_ref)

    h = _rms(x_ref[0], g_ref[...]).astype(bf16)

    ut = _dot_nt(wt_ref[...], h)
    u_k = _dot(h, w_ref[:, OFF_K:OFF_K + K_COLS])
    ub = _dot(h, w_ref[:, OFF_B:OFF_B + B_COLS])
    v = _dot(h, w_ref[:, OFF_D:OFF_D + D_COLS])
    kc_ref[0] = _dot(h, w_ref[:, OFF_KC:OFF_KC + KC_COLS])

    for hd in range(QT_ROWS // HEAD_DIM):
        rows = slice(hd * HEAD_DIM, (hd + 1) * HEAD_DIM)
        u = ut[rows]
        ms = jnp.mean(u * u, axis=0, keepdims=True)
        qt_ref[0, rows, :] = (u * lax.rsqrt(ms + EPS) * gq_ref[rows, :]).astype(bf16)
    ones = jnp.ones((HEAD_DIM, LANES), bf16)
    for hv in range(V_HEADS):
        vh = ut[QT_ROWS + hv * HEAD_DIM:QT_ROWS + (hv + 1) * HEAD_DIM].astype(bf16)
        for c in range(tt // LANES):
            vt_ref[0, c, hv * VHEAD:hv * VHEAD + HEAD_DIM, :] = vh[:, c * LANES:(c + 1) * LANES]
            vt_ref[0, c, hv * VHEAD + HEAD_DIM:(hv + 1) * VHEAD, :] = ones
    gt_ref[0] = jax.nn.sigmoid(ut[QT_ROWS + V_HEADS * HEAD_DIM:])

    hi, lo = _split(u_k * u_k)
    ss = _dot(hi, seg_ref[...]) + _dot(lo, seg_ref[...])
    kk_ref[0] = (u_k * lax.rsqrt(ss * (1.0 / HEAD_DIM) + EPS) * gk_ref[...]).astype(bf16)

    s = jnp.concatenate([vhalo_ref[...], v], axis=0)
    grp = jnp.right_shift(lax.broadcasted_iota(jnp.int32, (1, D_COLS), 1), POOL_SHIFT)
    acc = None
    for k, w in enumerate(POOL_WINDOWS):
        s = s + pltpu.roll(s, w // 2, 0)
        acc = s if acc is None else jnp.where(grp >= k, s, acc)
    win = jnp.where(grp == 0, POOL_WINDOWS[0],
                    jnp.where(grp == 1, POOL_WINDOWS[1],
                              jnp.where(grp == 2, POOL_WINDOWS[2], POOL_WINDOWS[3])))
    t = ti * tt + lax.broadcasted_iota(jnp.int32, (tt, 1), 0)
    cnt = jnp.minimum(t + 1, win).astype(f32)
    d = acc[POOL_HALO:] / cnt - v
    o_d = _dot(d.astype(bf16), pw_ref[...]) * ps_ref[...]
    vhalo_ref[...] = v[tt - POOL_HALO:]

    z = ub[:, GROUP_WIDTH:2 * GROUP_WIDTH] * ub[:, 2 * GROUP_WIDTH:]
    ze = jnp.concatenate([zhalo_ref[...], z], axis=0)
    z1 = pltpu.roll(ze, 1, 0)[CONV_HALO:]
    z2 = pltpu.roll(ze, 2, 0)[CONV_HALO:]
    conv = cw_ref[0:1, :] * z2 + cw_ref[1:2, :] * z1 + cw_ref[2:3, :] * z
    o_b = ub[:, :GROUP_WIDTH] * conv
    zhalo_ref[...] = z[tt - CONV_HALO:]

    ybd_ref[0, :, :GROUP_WIDTH] = _rms(o_b, gbd_ref[:, :GROUP_WIDTH]).astype(bf16)
    ybd_ref[0, :, GROUP_WIDTH:] = _rms(o_d, gbd_ref[:, GROUP_WIDTH:]).astype(bf16)


def _proj(x3, gain, wt, w, layer, seg, gq, gk, conv_w, pool_w, pool_scale, gbd, tt=1024):
    b, t, d = x3.shape
    tok = lambda w_: pl.BlockSpec((1, tt, w_), lambda i, j: (i, j, 0))
    feat = lambda r: pl.BlockSpec((1, r, tt), lambda i, j: (i, 0, j))
    consts = [seg, gq, gk, conv_w, pool_w, pool_scale, gbd]
    return pl.pallas_call(
        _proj_kernel,
        grid=(b, t // tt),
        in_specs=([tok(d), _const_spec(gain.shape), _layer_spec(wt, layer), _layer_spec(w, layer)]
                  + [_const_spec(c.shape) for c in consts]),
        out_specs=[feat(QT_ROWS), tok(K_COLS),
                   pl.BlockSpec((1, tt // LANES, VT_ROWS, LANES), lambda i, j: (i, j, 0, 0)),
                   tok(KC_COLS), feat(GT_ROWS), tok(2 * GROUP_WIDTH)],
        out_shape=[jax.ShapeDtypeStruct((b, QT_ROWS, t), bf16),
                   jax.ShapeDtypeStruct((b, t, K_COLS), bf16),
                   jax.ShapeDtypeStruct((b, t // LANES, VT_ROWS, LANES), bf16),
                   jax.ShapeDtypeStruct((b, t, KC_COLS), f32),
                   jax.ShapeDtypeStruct((b, GT_ROWS, t), f32),
                   jax.ShapeDtypeStruct((b, t, 2 * GROUP_WIDTH), bf16)],
        scratch_shapes=[pltpu.VMEM((CONV_HALO, GROUP_WIDTH), f32), pltpu.VMEM((POOL_HALO, D_COLS), f32)],
        compiler_params=pltpu.CompilerParams(dimension_semantics=("parallel", "arbitrary"),
                                             vmem_limit_bytes=VMEM_LIMIT),
        name="proj",
    )(x3, gain, wt, w, *consts)


def _compress_kernel(kvc_ref, pk_ref, w1k_ref, w2k_ref, gk_ref, pv_ref, w1v_ref, w2vt_ref,
                     kc_ref, vct_ref):
    nch = kvc_ref.shape[1] // CMP_STRIDE
    half = CMP_STRIDE * HEAD_DIM
    rows = [kvc_ref[0, pl.ds(r, nch, stride=CMP_STRIDE), :] for r in range(CMP_STRIDE)]
    k_chunks = jnp.concatenate([x[:, :HEAD_DIM] for x in rows], axis=1)
    v_chunks = jnp.concatenate([x[:, HEAD_DIM:] for x in rows], axis=1)

    def hidden(c, pos_ref, w1_ref):
        ha = _dot((c + pos_ref[0:1, :]).astype(bf16), w1_ref[0:half, :])
        hb = _dot((c + pos_ref[1:2, :]).astype(bf16), w1_ref[half:2 * half, :])
        return jax.nn.gelu(ha + pltpu.roll(hb, nch - 1, 0)).astype(bf16)

    kc_ref[0] = _rms(_dot(hidden(k_chunks, pk_ref, w1k_ref), w2k_ref[...]), gk_ref[...]).astype(bf16)
    vct_ref[0] = _dot_nt(w2vt_ref[...], hidden(v_chunks, pv_ref, w1v_ref)).astype(bf16)


def _compress(kvc, pk, w1k, w2k, gk, pv, w1v, w2vt):
    b, t, width = kvc.shape
    nch = t // CMP_STRIDE
    consts = [pk, w1k, w2k, gk, pv, w1v, w2vt]
    return pl.pallas_call(
        _compress_kernel,
        grid=(b,),
        in_specs=[pl.BlockSpec((1, t, width), lambda i: (i, 0, 0))] + [_const_spec(c.shape) for c in consts],
        out_specs=[pl.BlockSpec((1, nch, HEAD_DIM), lambda i: (i, 0, 0)),
                   pl.BlockSpec((1, HEAD_DIM, nch), lambda i: (i, 0, 0))],
        out_shape=[jax.ShapeDtypeStruct((b, nch, HEAD_DIM), bf16),
                   jax.ShapeDtypeStruct((b, HEAD_DIM, nch), bf16)],
        compiler_params=pltpu.CompilerParams(dimension_semantics=("parallel",),
                                             vmem_limit_bytes=VMEM_LIMIT),
        name="compress",
    )(kvc, *consts)


def _attn_kernel(qt_ref, kk_ref, vt_ref, kc_ref, vct_ref, gt_ref, sink_ref, gac_ref, ovt_ref,
                 y_ref, negsel_ref):
    tile = qt_ref.shape[2]
    sq = SUBQ
    subs = range(tile // sq)
    t_len = kk_ref.shape[1]
    nch = kc_ref.shape[1]
    n_sel = ovt_ref.shape[0]
    seg = min(SLC_SEG, t_len)
    rep = SWA_HEADS // SWA_KV_HEADS
    kv_groups = range(SWA_KV_HEADS)
    tile0 = pl.program_id(1) * tile
    t0 = [tile0 + u * sq for u in subs]
    lane_head = lambda a, h: a[:, h * sq:(h + 1) * sq]

    def stack_heads(u, first, count):
        return jnp.concatenate(
            [qt_ref[0, (first + h) * HEAD_DIM:(first + h + 1) * HEAD_DIM, u * sq:(u + 1) * sq]
             for h in range(count)], axis=1)

    def values(chunk0, n_chunks, head):
        rows = slice(head * VHEAD, (head + 1) * VHEAD)
        return jnp.concatenate([vt_ref[0, chunk0 + c, rows, :] for c in range(n_chunks)], axis=1)

    def finish(acc, extra=None):
        den = acc[HEAD_DIM:]
        if extra is not None:
            den = den + extra
        return acc[:HEAD_DIM] / den

    def band_bias(u, s0, span, window):
        diff = (t0[u] + lax.broadcasted_iota(jnp.int32, (span, sq), 1)
                - (s0 + lax.broadcasted_iota(jnp.int32, (span, sq), 0)))
        return jnp.where((diff >= 0) & (diff < window), 0.0, NEG)

    def emit(u, gi, o):
        cols = slice(gi * GROUP_WIDTH, (gi + 1) * GROUP_WIDTH)
        y_ref[0, u * sq:(u + 1) * sq, cols] = _rms(o.T, gac_ref[:, cols]).astype(bf16)

    sink = lambda g, r: sink_ref[:, g * rep + r:g * rep + r + 1]
    span_w = min(NSA_WINDOW + sq, t_len)
    span_c = min(SWA_WINDOW + sq, t_len)
    s0_w = [pl.multiple_of(jnp.maximum(t0[u] - NSA_WINDOW, 0), sq) for u in subs]
    s0_c = [pl.multiple_of(jnp.maximum(t0[u] - SWA_WINDOW, 0), sq) for u in subs]

    qa = [stack_heads(u, 0, NSA_HEADS) for u in subs]
    s_cmp = [_dot(kc_ref[0], qa[u]) for u in subs]
    s_w = [_dot(kk_ref[0, pl.ds(s0_w[u], span_w), HEAD_DIM:2 * HEAD_DIM], qa[u]) for u in subs]
    s_c = [[_dot(kk_ref[0, pl.ds(s0_c[u], span_c), (2 + g) * HEAD_DIM:(3 + g) * HEAD_DIM],
                 stack_heads(u, NSA_HEADS + g * rep, rep)) for g in kv_groups] for u in subs]

    n_i = lax.broadcasted_iota(jnp.int32, (nch, NSA_HEADS * sq), 0)
    c_lane = lax.broadcasted_iota(jnp.int32, (nch, NSA_HEADS * sq), 1) & (sq - 1)
    c_valid = [(n_i * CMP_STRIDE + (CMP_LEN - 1) <= t0[u] + c_lane) & (n_i < nch - 1) for u in subs]
    z_cmp = [jnp.where(c_valid[u], s_cmp[u], NEG) for u in subs]
    m_cmp = [jnp.max(z_cmp[u], axis=0, keepdims=True) for u in subs]
    z_w, m_w, z_c, m_c = [], [], [], []
    for u in subs:
        bias = band_bias(u, s0_w[u], span_w, NSA_WINDOW)
        z_w.append(jnp.concatenate([lane_head(s_w[u], h) + bias for h in range(NSA_HEADS)], axis=1))
        m_w.append(jnp.max(z_w[u], axis=0, keepdims=True))
        bias = band_bias(u, s0_c[u], span_c, SWA_WINDOW)
        z_c.append([[lane_head(s_c[u][g], r) + bias for r in range(rep)] for g in kv_groups])
        m_c.append([[jnp.maximum(jnp.max(z_c[u][g][r], axis=0, keepdims=True), sink(g, r))
                     for r in range(rep)] for g in kv_groups])

    e_cmp = [jnp.exp2(z_cmp[u] - m_cmp[u]) for u in subs]
    p_w = [jnp.exp2(z_w[u] - m_w[u]).astype(bf16) for u in subs]
    p_c = [[jnp.concatenate([jnp.exp2(z_c[u][g][r] - m_c[u][g][r]).astype(bf16) for r in range(rep)], axis=1)
            for g in kv_groups] for u in subs]

    p_cmp = [jnp.where(c_valid[u], e_cmp[u] / jnp.sum(e_cmp[u], axis=0, keepdims=True), 0.0) for u in subs]
    o_cmp = [_dot(vct_ref[0], p_cmp[u].astype(bf16)) for u in subs]
    imp = []
    for u in subs:
        p_sum = lane_head(p_cmp[u], 0)
        for h in range(1, NSA_HEADS):
            p_sum = p_sum + lane_head(p_cmp[u], h)
        p_hi, p_lo = _split(p_sum)
        imp.append(_dot(ovt_ref[...], p_hi) + _dot(ovt_ref[...], p_lo))
    o_win = [finish(_dot(values(s0_w[u] // LANES, span_w // LANES, 1), p_w[u])) for u in subs]
    for u in subs:
        o_c = []
        for g in kv_groups:
            acc = _dot(values(s0_c[u] // LANES, span_c // LANES, 2 + g), p_c[u][g])
            o_c += [finish(lane_head(acc, r), jnp.exp2(sink(g, r) - m_c[u][g][r])) for r in range(rep)]
        emit(u, 1, jnp.concatenate(o_c, axis=0))

    j = lax.broadcasted_iota(jnp.int32, (n_sel, sq), 0)
    for u in subs:
        t = t0[u] + lax.broadcasted_iota(jnp.int32, (n_sel, sq), 1)
        cur = jnp.right_shift(t, SLC_SHIFT)
        forced = (j == 0) | (j == cur) | (j == cur - 1)
        score = jnp.where(forced, FORCE, jnp.where(j * SLC_BLOCK > t, -1.0, imp[u]))
        tiles = [slice(r * SUBLANES, (r + 1) * SUBLANES) for r in range(n_sel // SUBLANES)]
        rank = [jnp.zeros((SUBLANES, sq), f32) for _ in tiles]
        for jp in range(n_sel):
            other = score[jp:jp + 1, :]
            for r, rows in enumerate(tiles):
                mine = score[rows]
                if rows.stop <= jp:
                    ahead = other > mine
                elif rows.start > jp:
                    ahead = other >= mine
                else:
                    ahead = (other > mine) | ((other == mine) & (j[rows] > jp))
                rank[r] = rank[r] + jnp.where(ahead, 1.0, 0.0)
        negsel_ref[u] = jnp.where(jnp.concatenate(rank, axis=0) < float(min(SLC_TOP, n_sel)), 0.0, NEG)

    def selected(n):
        def scores(k):
            k_seg = kk_ref[0, k * seg:(k + 1) * seg, 0:HEAD_DIM]
            return [_dot(k_seg, qa[u]) for u in subs]

        def masked(k, s):
            z = []
            for u in subs:
                bias = jnp.concatenate(
                    [jnp.broadcast_to(negsel_ref[u, b:b + 1, :], (SLC_BLOCK, sq))
                     for b in range(k * (seg // SLC_BLOCK), (k + 1) * (seg // SLC_BLOCK))], axis=0)
                if k == n - 1:
                    causal = (k * seg + lax.broadcasted_iota(jnp.int32, (seg, sq), 0)
                              <= t0[u] + lax.broadcasted_iota(jnp.int32, (seg, sq), 1))
                    bias = jnp.where(causal, bias, NEG)
                z.append(jnp.concatenate([lane_head(s[u], h) + bias for h in range(NSA_HEADS)], axis=1))
            return z

        m = [jnp.full((1, NSA_HEADS * sq), NEG, f32) for _ in subs]
        acc = [jnp.zeros((VHEAD, NSA_HEADS * sq), f32) for _ in subs]
        s_next = scores(0)
        for k in range(n):
            s_cur = s_next
            if k + 1 < n:
                s_next = scores(k + 1)
            z = masked(k, s_cur)
            m_new = [jnp.maximum(m[u], jnp.max(z[u], axis=0, keepdims=True)) for u in subs]
            p = [jnp.exp2(z[u] - m_new[u]).astype(bf16) for u in subs]
            v_seg = values(k * (seg // LANES), seg // LANES, 0)
            acc = [jnp.exp2(m[u] - m_new[u]) * acc[u] + _dot(v_seg, p[u]) for u in subs]
            m = m_new
        for u in subs:
            o_slc = finish(acc[u])
            gt = gt_ref[0, :, u * sq:(u + 1) * sq]
            o_a = []
            for h in range(NSA_HEADS):
                g = lambda k: gt[NSA_BRANCHES * h + k:NSA_BRANCHES * h + k + 1, :]
                o_a.append(g(0) * lane_head(o_cmp[u], h) + g(1) * lane_head(o_slc, h)
                           + g(2) * lane_head(o_win[u], h))
            emit(u, 0, jnp.concatenate(o_a, axis=0))

    n_seg = (tile0 + tile - 1) // seg + 1
    for n in range(1, t_len // seg + 1):
        pl.when(n_seg == n)(functools.partial(selected, n))


def _attn(qt, kk, vt, kc, vct, gt, sinks, gac, ovt):
    b, _, t = qt.shape
    tq = Q_TILE
    full = lambda a: pl.BlockSpec((1,) + a.shape[1:], lambda i, j: (i,) + (0,) * (a.ndim - 1))
    feat = lambda a: pl.BlockSpec((1, a.shape[1], tq), lambda i, j: (i, 0, j))
    width = 2 * GROUP_WIDTH
    return pl.pallas_call(
        _attn_kernel,
        grid=(b, t // tq),
        in_specs=[feat(qt), full(kk), full(vt), full(kc), full(vct), feat(gt),
                  _const_spec(sinks.shape), _const_spec(gac.shape), _const_spec(ovt.shape)],
        out_specs=pl.BlockSpec((1, tq, width), lambda i, j: (i, j, 0)),
        out_shape=jax.ShapeDtypeStruct((b, t, width), bf16),
        scratch_shapes=[pltpu.VMEM((tq // SUBQ, ovt.shape[0], SUBQ), f32)],
        compiler_params=pltpu.CompilerParams(dimension_semantics=("parallel", "parallel"),
                                             vmem_limit_bytes=VMEM_LIMIT),
        name="attn",
    )(qt, kk, vt, kc, vct, gt, sinks, gac, ovt)


def _overlap_t(t_len):
    n_c = (t_len - CMP_LEN) // CMP_STRIDE + 1
    nch = t_len // CMP_STRIDE
    ci = np.arange(nch)[None, :] * CMP_STRIDE
    sj = np.arange(t_len // SLC_BLOCK)[:, None] * SLC_BLOCK
    ov = (ci <= sj + SLC_BLOCK - 1) & (ci + CMP_LEN - 1 >= sj) & (np.arange(nch)[None, :] < n_c)
    return jnp.asarray(ov, bf16)


def _qk_gains(nsa_q, nsa_ks, nsa_kw, swa_q, swa_k):
    q_scale = ATTN_SCALE * LOG2E
    gq = jnp.concatenate([jnp.tile(nsa_q * q_scale, NSA_HEADS), jnp.tile(swa_q * q_scale, SWA_HEADS)])[:, None]
    gk = jnp.concatenate([nsa_ks, nsa_kw, jnp.tile(swa_k, SWA_KV_HEADS)])[None, :]
    return gq, gk


def _block_diag(blocks):
    n, r, c = blocks.shape
    eye = jnp.eye(n, dtype=blocks.dtype)
    return (eye[:, None, :, None] * blocks[:, :, None, :]).reshape(n * r, n * c)


def kernel(x, ffn1_norm, ffn1_w1, ffn1_w3, ffn1_w2, mix_norm, w_in, nsa_q_norm, nsa_kc_norm, nsa_ks_norm, nsa_kw_norm, cmp_pos_k, cmp_w1_k, cmp_w2_k, cmp_pos_v, cmp_w1_v, cmp_w2_v, conv_w, swa_q_norm, swa_k_norm, swa_sinks, pool_w, pool_scale, group_norm, w_out, ffn2_norm, ffn2_w1, ffn2_w3, ffn2_w2):
    b, t, d = x.shape
    depth = w_in.shape[0]
    ovt = _overlap_t(t)
    seg = jnp.asarray(np.kron(np.eye(K_COLS // HEAD_DIM), np.ones((HEAD_DIM, HEAD_DIM))), bf16)
    row = lambda v: v[None, :].astype(f32)
    cb = lambda v: v.astype(bf16)
    groups = lambda v, ids: jnp.concatenate([v[i * GROUP_WIDTH:(i + 1) * GROUP_WIDTH] for i in ids], axis=0)

    up_rows, down_rows = 256, ffn1_w2.shape[1] // 4
    ffn1 = (_to_bf16(ffn1_w1, up_rows), _to_bf16(ffn1_w3, up_rows), _to_bf16(ffn1_w2, down_rows))
    ffn2 = (_to_bf16(ffn2_w1, up_rows), _to_bf16(ffn2_w3, up_rows), _to_bf16(ffn2_w2, down_rows))
    wo = _to_bf16(w_out, up_rows)
    w_tok, w_feat = _arrange_w_in(w_in)

    for l in range(depth):
        x = _ffn(x, row(ffn1_norm[l]), *ffn1, layer=l)

        gq, gk = _qk_gains(nsa_q_norm[l], nsa_ks_norm[l], nsa_kw_norm[l], swa_q_norm[l], swa_k_norm[l])
        qt, kk, vt, kvc, gt, ybd = _proj(
            x, row(mix_norm[l]), w_feat, w_tok, l, seg, gq, gk, conv_w[l],
            cb(_block_diag(pool_w[l])), row(pool_scale[l]), row(groups(group_norm[l], (1, 3))))

        pos = lambda p: p.reshape(CMP_LEN // CMP_STRIDE, CMP_STRIDE * HEAD_DIM)
        kc, vct = _compress(kvc, pos(cmp_pos_k[l]), cb(cmp_w1_k[l]), cb(cmp_w2_k[l]), row(nsa_kc_norm[l]),
                            pos(cmp_pos_v[l]), cb(cmp_w1_v[l]), cb(cmp_w2_v[l].T))

        yac = _attn(qt, kk, vt, kc, vct, gt, row(swa_sinks[l] * LOG2E),
                    row(groups(group_norm[l], (0, 2))), ovt)

        mix = (yac, ybd, wo)
        x = _ffn(x, row(ffn2_norm[l]), *ffn2, layer=l, mix=mix)
    return x
```

```python
import functools

import numpy as np
import jax
import jax.numpy as jnp
from jax import lax
from jax.experimental import pallas as pl
from jax.experimental.pallas import tpu as pltpu

HEAD_DIM = 64
GROUP_WIDTH = 256
N_GROUPS = 4
NSA_HEADS = 4
NSA_BRANCHES = 3
CMP_LEN = 32
CMP_STRIDE = 16
SLC_BLOCK = 64
SLC_TOP = 8
NSA_WINDOW = 512
SWA_HEADS = 4
SWA_KV_HEADS = 2
SWA_WINDOW = 128
POOL_WINDOWS = (2, 4, 8, 16)
Q_TILE = 256
SUBQ = 128
EPS = 1e-6
NEG = -1e30
FORCE = 1e4
ATTN_SCALE = HEAD_DIM ** -0.5
LOG2E = float(np.log2(np.e))
SLC_SEG = 512
SLC_SHIFT = SLC_BLOCK.bit_length() - 1
POOL_SHIFT = (GROUP_WIDTH // len(POOL_WINDOWS)).bit_length() - 1

LANES = 128
SUBLANES = 8
CONV_HALO = 8
POOL_HALO = 16
VMEM_LIMIT = 56 * 2 ** 20

QT_ROWS = 512
V_HEADS = 4
VHEAD = 2 * HEAD_DIM
VT_ROWS = V_HEADS * VHEAD
GT_ROWS = 2 * SUBLANES
T_ROWS = QT_ROWS + V_HEADS * HEAD_DIM + GT_ROWS
K_COLS = 256
KC_COLS = 128
B_COLS = 768
D_COLS = 256
OFF_K = 0
OFF_KC = OFF_K + K_COLS
OFF_B = OFF_KC + KC_COLS
OFF_D = OFF_B + B_COLS
W_COLS = OFF_D + D_COLS

SPLIT_SIZES = (256, 64, 64, 64, 64, 64, 64, 12, 256, 256, 256, 256, 128, 128, 256)

f32 = jnp.float32
bf16 = jnp.bfloat16


def _rms(x, g):
    return x * lax.rsqrt(jnp.mean(x * x, axis=-1, keepdims=True) + EPS) * g


def _dot(a, b):
    return jnp.dot(a, b, preferred_element_type=f32)


def _dot_nt(a, b):
    return lax.dot_general(a, b, (((1,), (1,)), ((), ())), preferred_element_type=f32)


def _split(a):
    hi = a.astype(bf16)
    return hi, (a - hi.astype(f32)).astype(bf16)


def _const_spec(shape):
    nd = len(shape)
    return pl.BlockSpec(shape, lambda *_: (0,) * nd, pipeline_mode=pl.Buffered(1))


def _layer_spec(stacked, layer):
    return pl.BlockSpec((None,) + stacked.shape[1:], lambda *_: (layer, 0, 0), pipeline_mode=pl.Buffered(1))


def _cast_kernel(w_ref, o_ref):
    o_ref[...] = w_ref[...].astype(bf16)


def _to_bf16(w, rows):
    depth, r, c = w.shape
    spec = pl.BlockSpec((1, rows, c), lambda l, i: (l, i, 0))
    return pl.pallas_call(
        _cast_kernel,
        grid=(depth, r // rows),
        in_specs=[spec],
        out_specs=spec,
        out_shape=jax.ShapeDtypeStruct(w.shape, bf16),
        compiler_params=pltpu.CompilerParams(dimension_semantics=("parallel", "parallel"),
                                             vmem_limit_bytes=VMEM_LIMIT),
        name="cast",
    )(w)


def _arrange_kernel(w_ref, tok_ref, feat_ref):
    w = w_ref[0]
    offs = np.concatenate([[0], np.cumsum(SPLIT_SIZES)])
    (a_q, a_kc, a_vc, a_ks, a_vs, a_kw, a_vw, a_g,
     b_b, b_c, b_x, c_q, c_k, c_v, d_v) = [w[:, offs[k]:offs[k + 1]] for k in range(len(SPLIT_SIZES))]
    tok_ref[0] = jnp.concatenate([a_ks, a_kw, c_k, a_kc, a_vc, b_b, b_c, b_x, d_v], axis=1).astype(bf16)
    feat = [a_q, c_q, a_vs, a_vw, c_v, a_g]
    width = sum(p.shape[1] for p in feat)
    padded = -(-width // LANES) * LANES
    feat = jnp.concatenate(feat + [jnp.zeros((w.shape[0], padded - width), f32)], axis=1)
    feat_ref[0] = feat.T[:T_ROWS].astype(bf16)


def _arrange_w_in(w_in):
    depth, d, width = w_in.shape
    return pl.pallas_call(
        _arrange_kernel,
        grid=(depth,),
        in_specs=[pl.BlockSpec((1, d, width), lambda l: (l, 0, 0))],
        out_specs=[pl.BlockSpec((1, d, W_COLS), lambda l: (l, 0, 0)),
                   pl.BlockSpec((1, T_ROWS, d), lambda l: (l, 0, 0))],
        out_shape=[jax.ShapeDtypeStruct((depth, d, W_COLS), bf16),
                   jax.ShapeDtypeStruct((depth, T_ROWS, d), bf16)],
        compiler_params=pltpu.CompilerParams(dimension_semantics=("parallel",),
                                             vmem_limit_bytes=VMEM_LIMIT),
        name="arrange_w_in",
    )(w_in)


def _ffn_kernel(*refs, with_mix):
    if with_mix:
        x_ref, ya_ref, yb_ref, wo_ref, g_ref, w1_ref, w3_ref, w2_ref, o_ref = refs
        x = x_ref[0]
        for gi, y_ref in ((0, ya_ref), (1, yb_ref)):
            for half in range(2):
                rows = slice((gi + 2 * half) * GROUP_WIDTH, (gi + 2 * half + 1) * GROUP_WIDTH)
                x = x + _dot(y_ref[0, :, half * GROUP_WIDTH:(half + 1) * GROUP_WIDTH], wo_ref[rows, :])
    else:
        x_ref, g_ref, w1_ref, w3_ref, w2_ref, o_ref = refs
        x = x_ref[0]
    halves = [slice(0, x.shape[0] // 2), slice(x.shape[0] // 2, x.shape[0])]
    hs = [_rms(x[r], g_ref[...]).astype(bf16) for r in halves]
    a = [_dot(h, w1_ref[...]) for h in hs]
    b = [_dot(h, w3_ref[...]) for h in hs]
    act = [(a[i] * jax.nn.sigmoid(a[i]) * b[i]).astype(bf16) for i in range(len(halves))]
    for i, r in enumerate(halves):
        o_ref[0, r, :] = x[r] + 0.5 * _dot(act[i], w2_ref[...])


def _ffn(x, gain, w1, w3, w2, layer, mix=None, tm=512):
    b, t, d = x.shape
    row = lambda w: pl.BlockSpec((1, tm, w), lambda i, j: (i, j, 0))
    args, specs = [x], [row(d)]
    if mix is not None:
        ya, yb, wo = mix
        args += [ya, yb, wo]
        specs += [row(ya.shape[2]), row(yb.shape[2]), _layer_spec(wo, layer)]
    args += [gain, w1, w3, w2]
    specs += [_const_spec((1, d))] + [_layer_spec(w, layer) for w in (w1, w3, w2)]
    return pl.pallas_call(
        functools.partial(_ffn_kernel, with_mix=mix is not None),
        grid=(b, t // tm),
        in_specs=specs,
        out_specs=row(d),
        out_shape=jax.ShapeDtypeStruct((b, t, d), f32),
        compiler_params=pltpu.CompilerParams(dimension_semantics=("parallel", "parallel"),
                                             vmem_limit_bytes=VMEM_LIMIT),
        name="ffn" if mix is None else "ffn_mix",
    )(*args)


def _proj_kernel(x_ref, g_ref, wt_ref, w_ref, seg_ref, gq_ref, gk_ref, cw_ref, pw_ref, ps_ref, gbd_ref,
                 qt_ref, kk_ref, vt_ref, kc_ref, gt_ref, ybd_ref,
                 zhalo_ref, vhalo_ref):
    tt = x_ref.shape[1]
    ti = pl.program_id(1)

    @pl.when(ti == 0)
    def _():
        zhalo_ref[...] = jnp.zeros_like(zhalo_ref)
        vhalo_ref[...] = jnp.zeros_like(vhalo_ref)

    halves = [slice(0, tt // 2), slice(tt // 2, tt)]
    hs = [_rms(x_ref[0, r, :], g_ref[...]).astype(bf16) for r in halves]
    ut = jnp.concatenate([_dot_nt(wt_ref[...], h) for h in hs], axis=1)
    cols = lambda off, n: jnp.concatenate([_dot(h, w_ref[:, off:off + n]) for h in hs], axis=0)
    u_k = cols(OFF_K, K_COLS)
    ub = cols(OFF_B, B_COLS)
    v = cols(OFF_D, D_COLS)
    kc_ref[0] = cols(OFF_KC, KC_COLS)

    for hd in range(QT_ROWS // HEAD_DIM):
        rows = slice(hd * HEAD_DIM, (hd + 1) * HEAD_DIM)
        u = ut[rows]
        ms = jnp.mean(u * u, axis=0, keepdims=True)
        qt_ref[0, rows, :] = (u * lax.rsqrt(ms + EPS) * gq_ref[rows, :]).astype(bf16)
    ones = jnp.ones((HEAD_DIM, LANES), bf16)
    for hv in range(V_HEADS):
        vh = ut[QT_ROWS + hv * HEAD_DIM:QT_ROWS + (hv + 1) * HEAD_DIM].astype(bf16)
        for c in range(tt // LANES):
            vt_ref[0, c, hv * VHEAD:hv * VHEAD + HEAD_DIM, :] = vh[:, c * LANES:(c + 1) * LANES]
            vt_ref[0, c, hv * VHEAD + HEAD_DIM:(hv + 1) * VHEAD, :] = ones
    gt_ref[0] = jax.nn.sigmoid(ut[QT_ROWS + V_HEADS * HEAD_DIM:])

    hi, lo = _split(u_k * u_k)
    ss = _dot(hi, seg_ref[...]) + _dot(lo, seg_ref[...])
    kk_ref[0] = (u_k * lax.rsqrt(ss * (1.0 / HEAD_DIM) + EPS) * gk_ref[...]).astype(bf16)

    s = jnp.concatenate([vhalo_ref[...], v], axis=0)
    grp = jnp.right_shift(lax.broadcasted_iota(jnp.int32, (1, D_COLS), 1), POOL_SHIFT)
    acc = None
    for k, w in enumerate(POOL_WINDOWS):
        s = s + pltpu.roll(s, w // 2, 0)
        acc = s if acc is None else jnp.where(grp >= k, s, acc)
    win = jnp.where(grp == 0, POOL_WINDOWS[0],
                    jnp.where(grp == 1, POOL_WINDOWS[1],
                              jnp.where(grp == 2, POOL_WINDOWS[2], POOL_WINDOWS[3])))
    t = ti * tt + lax.broadcasted_iota(jnp.int32, (tt, 1), 0)
    cnt = jnp.minimum(t + 1, win).astype(f32)
    d = acc[POOL_HALO:] / cnt - v
    o_d = _dot(d.astype(bf16), pw_ref[...]) * ps_ref[...]
    vhalo_ref[...] = v[tt - POOL_HALO:]

    z = ub[:, GROUP_WIDTH:2 * GROUP_WIDTH] * ub[:, 2 * GROUP_WIDTH:]
    ze = jnp.concatenate([zhalo_ref[...], z], axis=0)
    z1 = pltpu.roll(ze, 1, 0)[CONV_HALO:]
    z2 = pltpu.roll(ze, 2, 0)[CONV_HALO:]
    conv = cw_ref[0:1, :] * z2 + cw_ref[1:2, :] * z1 + cw_ref[2:3, :] * z
    o_b = ub[:, :GROUP_WIDTH] * conv
    zhalo_ref[...] = z[tt - CONV_HALO:]

    ybd_ref[0, :, :GROUP_WIDTH] = _rms(o_b, gbd_ref[:, :GROUP_WIDTH]).astype(bf16)
    ybd_ref[0, :, GROUP_WIDTH:] = _rms(o_d, gbd_ref[:, GROUP_WIDTH:]).astype(bf16)


def _proj(x3, gain, wt, w, layer, seg, gq, gk, conv_w, pool_w, pool_scale, gbd, tt=1024):
    b, t, d = x3.shape
    tok = lambda w_: pl.BlockSpec((1, tt, w_), lambda i, j: (i, j, 0))
    feat = lambda r: pl.BlockSpec((1, r, tt), lambda i, j: (i, 0, j))
    consts = [seg, gq, gk, conv_w, pool_w, pool_scale, gbd]
    return pl.pallas_call(
        _proj_kernel,
        grid=(b, t // tt),
        in_specs=([tok(d), _const_spec(gain.shape), _layer_spec(wt, layer), _layer_spec(w, layer)]
                  + [_const_spec(c.shape) for c in consts]),
        out_specs=[feat(QT_ROWS), tok(K_COLS),
                   pl.BlockSpec((1, tt // LANES, VT_ROWS, LANES), lambda i, j: (i, j, 0, 0)),
                   tok(KC_COLS), feat(GT_ROWS), tok(2 * GROUP_WIDTH)],
        out_shape=[jax.ShapeDtypeStruct((b, QT_ROWS, t), bf16),
                   jax.ShapeDtypeStruct((b, t, K_COLS), bf16),
                   jax.ShapeDtypeStruct((b, t // LANES, VT_ROWS, LANES), bf16),
                   jax.ShapeDtypeStruct((b, t, KC_COLS), f32),
                   jax.ShapeDtypeStruct((b, GT_ROWS, t), f32),
                   jax.ShapeDtypeStruct((b, t, 2 * GROUP_WIDTH), bf16)],
        scratch_shapes=[pltpu.VMEM((CONV_HALO, GROUP_WIDTH), f32), pltpu.VMEM((POOL_HALO, D_COLS), f32)],
        compiler_params=pltpu.CompilerParams(dimension_semantics=("parallel", "arbitrary"),
                                             vmem_limit_bytes=VMEM_LIMIT),
        name="proj",
    )(x3, gain, wt, w, *consts)


def _compress_kernel(kvc_ref, pk_ref, w1k_ref, w2k_ref, gk_ref, pv_ref, w1v_ref, w2vt_ref,
                     kc_ref, vct_ref):
    nch = kvc_ref.shape[1] // CMP_STRIDE
    half = CMP_STRIDE * HEAD_DIM
    rows = [kvc_ref[0, pl.ds(r, nch, stride=CMP_STRIDE), :] for r in range(CMP_STRIDE)]
    k_chunks = jnp.concatenate([x[:, :HEAD_DIM] for x in rows], axis=1)
    v_chunks = jnp.concatenate([x[:, HEAD_DIM:] for x in rows], axis=1)

    def hidden(c, pos_ref, w1_ref):
        ha = _dot((c + pos_ref[0:1, :]).astype(bf16), w1_ref[0:half, :])
        hb = _dot((c + pos_ref[1:2, :]).astype(bf16), w1_ref[half:2 * half, :])
        return jax.nn.gelu(ha + pltpu.roll(hb, nch - 1, 0)).astype(bf16)

    kc_ref[0] = _rms(_dot(hidden(k_chunks, pk_ref, w1k_ref), w2k_ref[...]), gk_ref[...]).astype(bf16)
    vct_ref[0] = _dot_nt(w2vt_ref[...], hidden(v_chunks, pv_ref, w1v_ref)).astype(bf16)


def _compress(kvc, pk, w1k, w2k, gk, pv, w1v, w2vt):
    b, t, width = kvc.shape
    nch = t // CMP_STRIDE
    consts = [pk, w1k, w2k, gk, pv, w1v, w2vt]
    return pl.pallas_call(
        _compress_kernel,
        grid=(b,),
        in_specs=[pl.BlockSpec((1, t, width), lambda i: (i, 0, 0))] + [_const_spec(c.shape) for c in consts],
        out_specs=[pl.BlockSpec((1, nch, HEAD_DIM), lambda i: (i, 0, 0)),
                   pl.BlockSpec((1, HEAD_DIM, nch), lambda i: (i, 0, 0))],
        out_shape=[jax.ShapeDtypeStruct((b, nch, HEAD_DIM), bf16),
                   jax.ShapeDtypeStruct((b, HEAD_DIM, nch), bf16)],
        compiler_params=pltpu.CompilerParams(dimension_semantics=("parallel",),
                                             vmem_limit_bytes=VMEM_LIMIT),
        name="compress",
    )(kvc, *consts)


def _attn_kernel(qt_ref, kk_ref, vt_ref, kc_ref, vct_ref, gt_ref, sink_ref, gac_ref, ovt_ref,
                 cwin_ref, cswa_ref, csel_ref, y_ref, negsel_ref):
    tile = qt_ref.shape[2]
    sq = SUBQ
    subs = range(tile // sq)
    t_len = kk_ref.shape[1]
    nch = kc_ref.shape[1]
    n_sel = ovt_ref.shape[0]
    seg = min(SLC_SEG, t_len)
    rep = SWA_HEADS // SWA_KV_HEADS
    kv_groups = range(SWA_KV_HEADS)
    tile0 = pl.program_id(1) * tile
    t0 = [tile0 + u * sq for u in subs]
    lane_head = lambda a, h: a[:, h * sq:(h + 1) * sq]

    def stack_heads(u, first, count):
        return jnp.concatenate(
            [qt_ref[0, (first + h) * HEAD_DIM:(first + h + 1) * HEAD_DIM, u * sq:(u + 1) * sq]
             for h in range(count)], axis=1)

    def values(chunk0, n_chunks, head):
        rows = slice(head * VHEAD, (head + 1) * VHEAD)
        return jnp.concatenate([vt_ref[0, chunk0 + c, rows, :] for c in range(n_chunks)], axis=1)

    def finish(acc, extra=None):
        den = acc[HEAD_DIM:]
        if extra is not None:
            den = den + extra
        return acc[:HEAD_DIM] / den

    def mask_rows(table_ref, u, s0, span, r_lo):
        return table_ref[pl.ds(pl.multiple_of(s0 - t0[u] - r_lo, sq), span), :]

    def emit(u, gi, o):
        cols = slice(gi * GROUP_WIDTH, (gi + 1) * GROUP_WIDTH)
        y_ref[0, u * sq:(u + 1) * sq, cols] = _rms(o.T, gac_ref[:, cols]).astype(bf16)

    sink = lambda g, r: sink_ref[:, g * rep + r:g * rep + r + 1]
    span_w = min(NSA_WINDOW + sq, t_len)
    span_c = min(SWA_WINDOW + sq, t_len)
    s0_w = [pl.multiple_of(jnp.maximum(t0[u] - NSA_WINDOW, 0), sq) for u in subs]
    s0_c = [pl.multiple_of(jnp.maximum(t0[u] - SWA_WINDOW, 0), sq) for u in subs]

    qa = [stack_heads(u, 0, NSA_HEADS) for u in subs]
    s_cmp = [_dot(kc_ref[0], qa[u]) for u in subs]
    s_w = [_dot(kk_ref[0, pl.ds(s0_w[u], span_w), HEAD_DIM:2 * HEAD_DIM], qa[u]) for u in subs]
    s_c = [[_dot(kk_ref[0, pl.ds(s0_c[u], span_c), (2 + g) * HEAD_DIM:(3 + g) * HEAD_DIM],
                 stack_heads(u, NSA_HEADS + g * rep, rep)) for g in kv_groups] for u in subs]

    n_i = lax.broadcasted_iota(jnp.int32, (nch, NSA_HEADS * sq), 0)
    c_lane = lax.broadcasted_iota(jnp.int32, (nch, NSA_HEADS * sq), 1) & (sq - 1)
    c_valid = [(n_i * CMP_STRIDE + (CMP_LEN - 1) <= t0[u] + c_lane) & (n_i < nch - 1) for u in subs]
    z_cmp = [jnp.where(c_valid[u], s_cmp[u], NEG) for u in subs]
    m_cmp = [jnp.max(z_cmp[u], axis=0, keepdims=True) for u in subs]
    z_w, m_w, z_c, m_c = [], [], [], []
    for u in subs:
        bias = mask_rows(cwin_ref, u, s0_w[u], span_w, -NSA_WINDOW)
        z_w.append(jnp.concatenate([lane_head(s_w[u], h) + bias for h in range(NSA_HEADS)], axis=1))
        m_w.append(jnp.max(z_w[u], axis=0, keepdims=True))
        bias = mask_rows(cswa_ref, u, s0_c[u], span_c, -SWA_WINDOW)
        z_c.append([[lane_head(s_c[u][g], r) + bias for r in range(rep)] for g in kv_groups])
        m_c.append([[jnp.maximum(jnp.max(z_c[u][g][r], axis=0, keepdims=True), sink(g, r))
                     for r in range(rep)] for g in kv_groups])

    e_cmp = [jnp.exp2(z_cmp[u] - m_cmp[u]) for u in subs]
    p_w = [jnp.exp2(z_w[u] - m_w[u]).astype(bf16) for u in subs]
    p_c = [[jnp.concatenate([jnp.exp2(z_c[u][g][r] - m_c[u][g][r]).astype(bf16) for r in range(rep)], axis=1)
            for g in kv_groups] for u in subs]

    p_cmp = [jnp.where(c_valid[u], e_cmp[u] / jnp.sum(e_cmp[u], axis=0, keepdims=True), 0.0) for u in subs]
    o_cmp = [_dot(vct_ref[0], p_cmp[u].astype(bf16)) for u in subs]
    imp = []
    for u in subs:
        p_sum = lane_head(p_cmp[u], 0)
        for h in range(1, NSA_HEADS):
            p_sum = p_sum + lane_head(p_cmp[u], h)
        p_hi, p_lo = _split(p_sum)
        imp.append(_dot(ovt_ref[...], p_hi) + _dot(ovt_ref[...], p_lo))
    o_win = [finish(_dot(values(s0_w[u] // LANES, span_w // LANES, 1), p_w[u])) for u in subs]
    for u in subs:
        o_c = []
        for g in kv_groups:
            acc = _dot(values(s0_c[u] // LANES, span_c // LANES, 2 + g), p_c[u][g])
            o_c += [finish(lane_head(acc, r), jnp.exp2(sink(g, r) - m_c[u][g][r])) for r in range(rep)]
        emit(u, 1, jnp.concatenate(o_c, axis=0))

    j = lax.broadcasted_iota(jnp.int32, (n_sel, sq), 0)
    for u in subs:
        t = t0[u] + lax.broadcasted_iota(jnp.int32, (n_sel, sq), 1)
        cur = jnp.right_shift(t, SLC_SHIFT)
        forced = (j == 0) | (j == cur) | (j == cur - 1)
        score = jnp.where(forced, FORCE, jnp.where(j * SLC_BLOCK > t, -1.0, imp[u]))
        tiles = [slice(r * SUBLANES, (r + 1) * SUBLANES) for r in range(n_sel // SUBLANES)]
        rank = [jnp.zeros((SUBLANES, sq), f32) for _ in tiles]
        for jp in range(n_sel):
            other = score[jp:jp + 1, :]
            for r, rows in enumerate(tiles):
                mine = score[rows]
                if rows.stop <= jp:
                    ahead = other > mine
                elif rows.start > jp:
                    ahead = other >= mine
                else:
                    ahead = (other > mine) | ((other == mine) & (j[rows] > jp))
                rank[r] = rank[r] + jnp.where(ahead, 1.0, 0.0)
        negsel_ref[u] = jnp.where(jnp.concatenate(rank, axis=0) < float(min(SLC_TOP, n_sel)), 0.0, NEG)

    def selected(n):
        def scores(k):
            k_seg = kk_ref[0, k * seg:(k + 1) * seg, 0:HEAD_DIM]
            return [_dot(k_seg, qa[u]) for u in subs]

        def masked(k, s):
            z = []
            for u in subs:
                bias = jnp.concatenate(
                    [jnp.broadcast_to(negsel_ref[u, b:b + 1, :], (SLC_BLOCK, sq))
                     for b in range(k * (seg // SLC_BLOCK), (k + 1) * (seg // SLC_BLOCK))], axis=0)
                if k == n - 1:
                    bias = bias + mask_rows(csel_ref, u, k * seg, seg, -seg)
                z.append(jnp.concatenate([lane_head(s[u], h) + bias for h in range(NSA_HEADS)], axis=1))
            return z

        m = [jnp.full((1, NSA_HEADS * sq), NEG, f32) for _ in subs]
        acc = [jnp.zeros((VHEAD, NSA_HEADS * sq), f32) for _ in subs]
        s_next = scores(0)
        for k in range(n):
            s_cur = s_next
            if k + 1 < n:
                s_next = scores(k + 1)
            z = masked(k, s_cur)
            m_new = [jnp.maximum(m[u], jnp.max(z[u], axis=0, keepdims=True)) for u in subs]
            p = [jnp.exp2(z[u] - m_new[u]).astype(bf16) for u in subs]
            v_seg = values(k * (seg // LANES), seg // LANES, 0)
            acc = [jnp.exp2(m[u] - m_new[u]) * acc[u] + _dot(v_seg, p[u]) for u in subs]
            m = m_new
        for u in subs:
            o_slc = finish(acc[u])
            gt = gt_ref[0, :, u * sq:(u + 1) * sq]
            o_a = []
            for h in range(NSA_HEADS):
                g = lambda k: gt[NSA_BRANCHES * h + k:NSA_BRANCHES * h + k + 1, :]
                o_a.append(g(0) * lane_head(o_cmp[u], h) + g(1) * lane_head(o_slc, h)
                           + g(2) * lane_head(o_win[u], h))
            emit(u, 0, jnp.concatenate(o_a, axis=0))

    n_seg = (tile0 + tile - 1) // seg + 1
    for n in range(1, t_len // seg + 1):
        pl.when(n_seg == n)(functools.partial(selected, n))


def _attn(qt, kk, vt, kc, vct, gt, sinks, gac, ovt, tables):
    b, _, t = qt.shape
    tq = Q_TILE
    full = lambda a: pl.BlockSpec((1,) + a.shape[1:], lambda i, j: (i,) + (0,) * (a.ndim - 1))
    feat = lambda a: pl.BlockSpec((1, a.shape[1], tq), lambda i, j: (i, 0, j))
    width = 2 * GROUP_WIDTH
    return pl.pallas_call(
        _attn_kernel,
        grid=(b, t // tq),
        in_specs=[feat(qt), full(kk), full(vt), full(kc), full(vct), feat(gt),
                  _const_spec(sinks.shape), _const_spec(gac.shape), _const_spec(ovt.shape)]
                 + [_const_spec(c.shape) for c in tables],
        out_specs=pl.BlockSpec((1, tq, width), lambda i, j: (i, j, 0)),
        out_shape=jax.ShapeDtypeStruct((b, t, width), bf16),
        scratch_shapes=[pltpu.VMEM((tq // SUBQ, ovt.shape[0], SUBQ), f32)],
        compiler_params=pltpu.CompilerParams(dimension_semantics=("parallel", "parallel"),
                                             vmem_limit_bytes=VMEM_LIMIT),
        name="attn",
    )(qt, kk, vt, kc, vct, gt, sinks, gac, ovt, *tables)


def _overlap_t(t_len):
    n_c = (t_len - CMP_LEN) // CMP_STRIDE + 1
    nch = t_len // CMP_STRIDE
    ci = np.arange(nch)[None, :] * CMP_STRIDE
    sj = np.arange(t_len // SLC_BLOCK)[:, None] * SLC_BLOCK
    ov = (ci <= sj + SLC_BLOCK - 1) & (ci + CMP_LEN - 1 >= sj) & (np.arange(nch)[None, :] < n_c)
    return jnp.asarray(ov, bf16)


def _mask_tables(t_len):
    seg = min(SLC_SEG, t_len)
    c = np.arange(SUBQ)[None, :]

    def table(r_lo, r_hi, visible):
        r = np.arange(r_lo, r_hi)[:, None]
        return jnp.asarray(np.where(visible(r, c), 0.0, NEG), f32)

    band = lambda w: (lambda r, c: (r <= c) & (r > c - w))
    return (table(-NSA_WINDOW, min(NSA_WINDOW + SUBQ, t_len), band(NSA_WINDOW)),
            table(-SWA_WINDOW, min(SWA_WINDOW + SUBQ, t_len), band(SWA_WINDOW)),
            table(-seg, seg, lambda r, c: r <= c))


def _qk_gains(nsa_q, nsa_ks, nsa_kw, swa_q, swa_k):
    q_scale = ATTN_SCALE * LOG2E
    gq = jnp.concatenate([jnp.tile(nsa_q * q_scale, NSA_HEADS), jnp.tile(swa_q * q_scale, SWA_HEADS)])[:, None]
    gk = jnp.concatenate([nsa_ks, nsa_kw, jnp.tile(swa_k, SWA_KV_HEADS)])[None, :]
    return gq, gk


def _block_diag(blocks):
    n, r, c = blocks.shape
    eye = jnp.eye(n, dtype=blocks.dtype)
    return (eye[:, None, :, None] * blocks[:, :, None, :]).reshape(n * r, n * c)


def kernel(x, ffn1_norm, ffn1_w1, ffn1_w3, ffn1_w2, mix_norm, w_in, nsa_q_norm, nsa_kc_norm, nsa_ks_norm, nsa_kw_norm, cmp_pos_k, cmp_w1_k, cmp_w2_k, cmp_pos_v, cmp_w1_v, cmp_w2_v, conv_w, swa_q_norm, swa_k_norm, swa_sinks, pool_w, pool_scale, group_norm, w_out, ffn2_norm, ffn2_w1, ffn2_w3, ffn2_w2):
    b, t, d = x.shape
    depth = w_in.shape[0]
    ovt = _overlap_t(t)
    tables = _mask_tables(t)
    seg = jnp.asarray(np.kron(np.eye(K_COLS // HEAD_DIM), np.ones((HEAD_DIM, HEAD_DIM))), bf16)
    row = lambda v: v[None, :].astype(f32)
    cb = lambda v: v.astype(bf16)
    groups = lambda v, ids: jnp.concatenate([v[i * GROUP_WIDTH:(i + 1) * GROUP_WIDTH] for i in ids], axis=0)

    up_rows, down_rows = 256, ffn1_w2.shape[1] // 4
    ffn1 = (_to_bf16(ffn1_w1, up_rows), _to_bf16(ffn1_w3, up_rows), _to_bf16(ffn1_w2, down_rows))
    ffn2 = (_to_bf16(ffn2_w1, up_rows), _to_bf16(ffn2_w3, up_rows), _to_bf16(ffn2_w2, down_rows))
    wo = _to_bf16(w_out, up_rows)
    w_tok, w_feat = _arrange_w_in(w_in)

    for l in range(depth):
        x = _ffn(x, row(ffn1_norm[l]), *ffn1, layer=l)

        gq, gk = _qk_gains(nsa_q_norm[l], nsa_ks_norm[l], nsa_kw_norm[l], swa_q_norm[l], swa_k_norm[l])
        qt, kk, vt, kvc, gt, ybd = _proj(
            x, row(mix_norm[l]), w_feat, w_tok, l, seg, gq, gk, conv_w[l],
            cb(_block_diag(pool_w[l])), row(pool_scale[l]), row(groups(group_norm[l], (1, 3))))

        pos = lambda p: p.reshape(CMP_LEN // CMP_STRIDE, CMP_STRIDE * HEAD_DIM)
        kc, vct = _compress(kvc, pos(cmp_pos_k[l]), cb(cmp_w1_k[l]), cb(cmp_w2_k[l]), row(nsa_kc_norm[l]),
                            pos(cmp_pos_v[l]), cb(cmp_w1_v[l]), cb(cmp_w2_v[l].T))

        yac = _attn(qt, kk, vt, kc, vct, gt, row(swa_sinks[l] * LOG2E),
                    row(groups(group_norm[l], (0, 2))), ovt, tables)

        mix = (yac, ybd, wo)
        x = _ffn(x, row(ffn2_norm[l]), *ffn2, layer=l, mix=mix)
    return x
```

```python
import functools

import numpy as np
import jax
import jax.numpy as jnp
from jax import lax
from jax.experimental import pallas as pl
from jax.experimental.pallas import tpu as pltpu

HEAD_DIM = 64
GROUP_WIDTH = 256
N_GROUPS = 4
NSA_HEADS = 4
NSA_BRANCHES = 3
CMP_LEN = 32
CMP_STRIDE = 16
SLC_BLOCK = 64
SLC_TOP = 8
NSA_WINDOW = 512
SWA_HEADS = 4
SWA_KV_HEADS = 2
SWA_WINDOW = 128
POOL_WINDOWS = (2, 4, 8, 16)
Q_TILE = 512
SUBQ = 128
EPS = 1e-6
NEG = -1e30
FORCE = 1e4
ATTN_SCALE = HEAD_DIM ** -0.5
LOG2E = float(np.log2(np.e))
SLC_SEG = 512
SLC_SHIFT = SLC_BLOCK.bit_length() - 1
POOL_SHIFT = (GROUP_WIDTH // len(POOL_WINDOWS)).bit_length() - 1

LANES = 128
SUBLANES = 8
CONV_HALO = 8
POOL_HALO = 16
VMEM_LIMIT = 56 * 2 ** 20

QT_ROWS = 512
V_HEADS = 4
VHEAD = 2 * HEAD_DIM
VT_ROWS = V_HEADS * VHEAD
GT_ROWS = 2 * SUBLANES
T_ROWS = QT_ROWS + V_HEADS * HEAD_DIM + GT_ROWS
K_COLS = 256
KC_COLS = 128
B_COLS = 768
D_COLS = 256
OFF_K = 0
OFF_KC = OFF_K + K_COLS
OFF_B = OFF_KC + KC_COLS
OFF_D = OFF_B + B_COLS
W_COLS = OFF_D + D_COLS

SPLIT_SIZES = (256, 64, 64, 64, 64, 64, 64, 12, 256, 256, 256, 256, 128, 128, 256)

f32 = jnp.float32
bf16 = jnp.bfloat16


def _rms(x, g):
    return x * lax.rsqrt(jnp.mean(x * x, axis=-1, keepdims=True) + EPS) * g


def _dot(a, b):
    return jnp.dot(a, b, preferred_element_type=f32)


def _dot_nt(a, b):
    return lax.dot_general(a, b, (((1,), (1,)), ((), ())), preferred_element_type=f32)


def _split(a):
    hi = a.astype(bf16)
    return hi, (a - hi.astype(f32)).astype(bf16)


def _const_spec(shape):
    nd = len(shape)
    return pl.BlockSpec(shape, lambda *_: (0,) * nd, pipeline_mode=pl.Buffered(1))


def _layer_spec(stacked, layer):
    return pl.BlockSpec((None,) + stacked.shape[1:], lambda *_: (layer, 0, 0), pipeline_mode=pl.Buffered(1))


def _cast_kernel(w_ref, o_ref):
    o_ref[...] = w_ref[...].astype(bf16)


def _to_bf16(w, rows):
    depth, r, c = w.shape
    spec = pl.BlockSpec((1, rows, c), lambda l, i: (l, i, 0))
    return pl.pallas_call(
        _cast_kernel,
        grid=(depth, r // rows),
        in_specs=[spec],
        out_specs=spec,
        out_shape=jax.ShapeDtypeStruct(w.shape, bf16),
        compiler_params=pltpu.CompilerParams(dimension_semantics=("parallel", "parallel"),
                                             vmem_limit_bytes=VMEM_LIMIT),
        name="cast",
    )(w)


def _arrange_kernel(w_ref, tok_ref, feat_ref):
    w = w_ref[0]
    offs = np.concatenate([[0], np.cumsum(SPLIT_SIZES)])
    (a_q, a_kc, a_vc, a_ks, a_vs, a_kw, a_vw, a_g,
     b_b, b_c, b_x, c_q, c_k, c_v, d_v) = [w[:, offs[k]:offs[k + 1]] for k in range(len(SPLIT_SIZES))]
    tok_ref[0] = jnp.concatenate([a_ks, a_kw, c_k, a_kc, a_vc, b_b, b_c, b_x, d_v], axis=1).astype(bf16)
    feat = [a_q, c_q, a_vs, a_vw, c_v, a_g]
    width = sum(p.shape[1] for p in feat)
    padded = -(-width // LANES) * LANES
    feat = jnp.concatenate(feat + [jnp.zeros((w.shape[0], padded - width), f32)], axis=1)
    feat_ref[0] = feat.T[:T_ROWS].astype(bf16)


def _arrange_w_in(w_in):
    depth, d, width = w_in.shape
    return pl.pallas_call(
        _arrange_kernel,
        grid=(depth,),
        in_specs=[pl.BlockSpec((1, d, width), lambda l: (l, 0, 0))],
        out_specs=[pl.BlockSpec((1, d, W_COLS), lambda l: (l, 0, 0)),
                   pl.BlockSpec((1, T_ROWS, d), lambda l: (l, 0, 0))],
        out_shape=[jax.ShapeDtypeStruct((depth, d, W_COLS), bf16),
                   jax.ShapeDtypeStruct((depth, T_ROWS, d), bf16)],
        compiler_params=pltpu.CompilerParams(dimension_semantics=("parallel",),
                                             vmem_limit_bytes=VMEM_LIMIT),
        name="arrange_w_in",
    )(w_in)


def _ffn_kernel(*refs, with_mix):
    if with_mix:
        x_ref, ya_ref, yb_ref, wo_ref, g_ref, w1_ref, w3_ref, w2_ref, o_ref = refs
        x = x_ref[0]
        for gi, y_ref in ((0, ya_ref), (1, yb_ref)):
            for half in range(2):
                rows = slice((gi + 2 * half) * GROUP_WIDTH, (gi + 2 * half + 1) * GROUP_WIDTH)
                x = x + _dot(y_ref[0, :, half * GROUP_WIDTH:(half + 1) * GROUP_WIDTH], wo_ref[rows, :])
    else:
        x_ref, g_ref, w1_ref, w3_ref, w2_ref, o_ref = refs
        x = x_ref[0]
    halves = [slice(0, x.shape[0] // 2), slice(x.shape[0] // 2, x.shape[0])]
    hs = [_rms(x[r], g_ref[...]).astype(bf16) for r in halves]
    a = [_dot(h, w1_ref[...]) for h in hs]
    b = [_dot(h, w3_ref[...]) for h in hs]
    act = [(a[i] * jax.nn.sigmoid(a[i]) * b[i]).astype(bf16) for i in range(len(halves))]
    for i, r in enumerate(halves):
        o_ref[0, r, :] = x[r] + 0.5 * _dot(act[i], w2_ref[...])


def _ffn(x, gain, w1, w3, w2, layer, mix=None, tm=512):
    b, t, d = x.shape
    row = lambda w: pl.BlockSpec((1, tm, w), lambda i, j: (i, j, 0))
    args, specs = [x], [row(d)]
    if mix is not None:
        ya, yb, wo = mix
        args += [ya, yb, wo]
        specs += [row(ya.shape[2]), row(yb.shape[2]), _layer_spec(wo, layer)]
    args += [gain, w1, w3, w2]
    specs += [_const_spec((1, d))] + [_layer_spec(w, layer) for w in (w1, w3, w2)]
    return pl.pallas_call(
        functools.partial(_ffn_kernel, with_mix=mix is not None),
        grid=(b, t // tm),
        in_specs=specs,
        out_specs=row(d),
        out_shape=jax.ShapeDtypeStruct((b, t, d), f32),
        compiler_params=pltpu.CompilerParams(dimension_semantics=("parallel", "parallel"),
                                             vmem_limit_bytes=VMEM_LIMIT),
        name="ffn" if mix is None else "ffn_mix",
    )(*args)


def _proj_kernel(x_ref, g_ref, wt_ref, w_ref, seg_ref, gq_ref, gk_ref, cw_ref, pw_ref, ps_ref, gbd_ref,
                 qt_ref, kk_ref, vt_ref, kc_ref, gt_ref, ybd_ref,
                 zhalo_ref, vhalo_ref):
    tt = x_ref.shape[1]
    ti = pl.program_id(1)

    @pl.when(ti == 0)
    def _():
        zhalo_ref[...] = jnp.zeros_like(zhalo_ref)
        vhalo_ref[...] = jnp.zeros_like(vhalo_ref)

    halves = [slice(0, tt // 2), slice(tt // 2, tt)]
    hs = [_rms(x_ref[0, r, :], g_ref[...]).astype(bf16) for r in halves]
    ut = jnp.concatenate([_dot_nt(wt_ref[...], h) for h in hs], axis=1)
    cols = lambda off, n: jnp.concatenate([_dot(h, w_ref[:, off:off + n]) for h in hs], axis=0)
    u_k = cols(OFF_K, K_COLS)
    ub = cols(OFF_B, B_COLS)
    v = cols(OFF_D, D_COLS)
    kc_ref[0] = cols(OFF_KC, KC_COLS)

    for hd in range(QT_ROWS // HEAD_DIM):
        rows = slice(hd * HEAD_DIM, (hd + 1) * HEAD_DIM)
        u = ut[rows]
        ms = jnp.mean(u * u, axis=0, keepdims=True)
        qt_ref[0, rows, :] = (u * lax.rsqrt(ms + EPS) * gq_ref[rows, :]).astype(bf16)
    ones = jnp.ones((HEAD_DIM, LANES), bf16)
    for hv in range(V_HEADS):
        vh = ut[QT_ROWS + hv * HEAD_DIM:QT_ROWS + (hv + 1) * HEAD_DIM].astype(bf16)
        for c in range(tt // LANES):
            vt_ref[0, c, hv * VHEAD:hv * VHEAD + HEAD_DIM, :] = vh[:, c * LANES:(c + 1) * LANES]
            vt_ref[0, c, hv * VHEAD + HEAD_DIM:(hv + 1) * VHEAD, :] = ones
    gt_ref[0] = jax.nn.sigmoid(ut[QT_ROWS + V_HEADS * HEAD_DIM:])

    hi, lo = _split(u_k * u_k)
    ss = _dot(hi, seg_ref[...]) + _dot(lo, seg_ref[...])
    kk_ref[0] = (u_k * lax.rsqrt(ss * (1.0 / HEAD_DIM) + EPS) * gk_ref[...]).astype(bf16)

    s = jnp.concatenate([vhalo_ref[...], v], axis=0)
    grp = jnp.right_shift(lax.broadcasted_iota(jnp.int32, (1, D_COLS), 1), POOL_SHIFT)
    acc = None
    for k, w in enumerate(POOL_WINDOWS):
        s = s + pltpu.roll(s, w // 2, 0)
        acc = s if acc is None else jnp.where(grp >= k, s, acc)
    win = jnp.where(grp == 0, POOL_WINDOWS[0],
                    jnp.where(grp == 1, POOL_WINDOWS[1],
                              jnp.where(grp == 2, POOL_WINDOWS[2], POOL_WINDOWS[3])))
    t = ti * tt + lax.broadcasted_iota(jnp.int32, (tt, 1), 0)
    cnt = jnp.minimum(t + 1, win).astype(f32)
    d = acc[POOL_HALO:] / cnt - v
    o_d = _dot(d.astype(bf16), pw_ref[...]) * ps_ref[...]
    vhalo_ref[...] = v[tt - POOL_HALO:]

    z = ub[:, GROUP_WIDTH:2 * GROUP_WIDTH] * ub[:, 2 * GROUP_WIDTH:]
    ze = jnp.concatenate([zhalo_ref[...], z], axis=0)
    z1 = pltpu.roll(ze, 1, 0)[CONV_HALO:]
    z2 = pltpu.roll(ze, 2, 0)[CONV_HALO:]
    conv = cw_ref[0:1, :] * z2 + cw_ref[1:2, :] * z1 + cw_ref[2:3, :] * z
    o_b = ub[:, :GROUP_WIDTH] * conv
    zhalo_ref[...] = z[tt - CONV_HALO:]

    ybd_ref[0, :, :GROUP_WIDTH] = _rms(o_b, gbd_ref[:, :GROUP_WIDTH]).astype(bf16)
    ybd_ref[0, :, GROUP_WIDTH:] = _rms(o_d, gbd_ref[:, GROUP_WIDTH:]).astype(bf16)


def _proj(x3, gain, wt, w, layer, seg, gq, gk, conv_w, pool_w, pool_scale, gbd, tt=1024):
    b, t, d = x3.shape
    tok = lambda w_: pl.BlockSpec((1, tt, w_), lambda i, j: (i, j, 0))
    feat = lambda r: pl.BlockSpec((1, r, tt), lambda i, j: (i, 0, j))
    consts = [seg, gq, gk, conv_w, pool_w, pool_scale, gbd]
    return pl.pallas_call(
        _proj_kernel,
        grid=(b, t // tt),
        in_specs=([tok(d), _const_spec(gain.shape), _layer_spec(wt, layer), _layer_spec(w, layer)]
                  + [_const_spec(c.shape) for c in consts]),
        out_specs=[feat(QT_ROWS), tok(K_COLS),
                   pl.BlockSpec((1, tt // LANES, VT_ROWS, LANES), lambda i, j: (i, j, 0, 0)),
                   tok(KC_COLS), feat(GT_ROWS), tok(2 * GROUP_WIDTH)],
        out_shape=[jax.ShapeDtypeStruct((b, QT_ROWS, t), bf16),
                   jax.ShapeDtypeStruct((b, t, K_COLS), bf16),
                   jax.ShapeDtypeStruct((b, t // LANES, VT_ROWS, LANES), bf16),
                   jax.ShapeDtypeStruct((b, t, KC_COLS), f32),
                   jax.ShapeDtypeStruct((b, GT_ROWS, t), f32),
                   jax.ShapeDtypeStruct((b, t, 2 * GROUP_WIDTH), bf16)],
        scratch_shapes=[pltpu.VMEM((CONV_HALO, GROUP_WIDTH), f32), pltpu.VMEM((POOL_HALO, D_COLS), f32)],
        compiler_params=pltpu.CompilerParams(dimension_semantics=("parallel", "arbitrary"),
                                             vmem_limit_bytes=VMEM_LIMIT),
        name="proj",
    )(x3, gain, wt, w, *consts)


def _compress_kernel(kvc_ref, pk_ref, w1k_ref, w2k_ref, gk_ref, pv_ref, w1v_ref, w2vt_ref,
                     kc_ref, vct_ref):
    nch = kvc_ref.shape[1] // CMP_STRIDE
    half = CMP_STRIDE * HEAD_DIM
    rows = [kvc_ref[0, pl.ds(r, nch, stride=CMP_STRIDE), :] for r in range(CMP_STRIDE)]
    k_chunks = jnp.concatenate([x[:, :HEAD_DIM] for x in rows], axis=1)
    v_chunks = jnp.concatenate([x[:, HEAD_DIM:] for x in rows], axis=1)

    def hidden(c, pos_ref, w1_ref):
        ha = _dot((c + pos_ref[0:1, :]).astype(bf16), w1_ref[0:half, :])
        hb = _dot((c + pos_ref[1:2, :]).astype(bf16), w1_ref[half:2 * half, :])
        return jax.nn.gelu(ha + pltpu.roll(hb, nch - 1, 0)).astype(bf16)

    kc_ref[0] = _rms(_dot(hidden(k_chunks, pk_ref, w1k_ref), w2k_ref[...]), gk_ref[...]).astype(bf16)
    vct_ref[0] = _dot_nt(w2vt_ref[...], hidden(v_chunks, pv_ref, w1v_ref)).astype(bf16)


def _compress(kvc, pk, w1k, w2k, gk, pv, w1v, w2vt):
    b, t, width = kvc.shape
    nch = t // CMP_STRIDE
    consts = [pk, w1k, w2k, gk, pv, w1v, w2vt]
    return pl.pallas_call(
        _compress_kernel,
        grid=(b,),
        in_specs=[pl.BlockSpec((1, t, width), lambda i: (i, 0, 0))] + [_const_spec(c.shape) for c in consts],
        out_specs=[pl.BlockSpec((1, nch, HEAD_DIM), lambda i: (i, 0, 0)),
                   pl.BlockSpec((1, HEAD_DIM, nch), lambda i: (i, 0, 0))],
        out_shape=[jax.ShapeDtypeStruct((b, nch, HEAD_DIM), bf16),
                   jax.ShapeDtypeStruct((b, HEAD_DIM, nch), bf16)],
        compiler_params=pltpu.CompilerParams(dimension_semantics=("parallel",),
                                             vmem_limit_bytes=VMEM_LIMIT),
        name="compress",
    )(kvc, *consts)


def _attn_kernel(qt_ref, kk_ref, vt_ref, kc_ref, vct_ref, gt_ref, sink_ref, gac_ref, ovt_ref,
                 cwin_ref, cswa_ref, csel_ref, y_ref, negsel_ref):
    tile = qt_ref.shape[2]
    sq = SUBQ
    subs = range(tile // sq)
    t_len = kk_ref.shape[1]
    nch = kc_ref.shape[1]
    n_sel = ovt_ref.shape[0]
    seg = min(SLC_SEG, t_len)
    rep = SWA_HEADS // SWA_KV_HEADS
    kv_groups = range(SWA_KV_HEADS)
    tile0 = pl.program_id(1) * tile
    t0 = [tile0 + u * sq for u in subs]
    lane_head = lambda a, h: a[:, h * sq:(h + 1) * sq]

    def stack_heads(u, first, count):
        return jnp.concatenate(
            [qt_ref[0, (first + h) * HEAD_DIM:(first + h + 1) * HEAD_DIM, u * sq:(u + 1) * sq]
             for h in range(count)], axis=1)

    def values(chunk0, n_chunks, head):
        rows = slice(head * VHEAD, (head + 1) * VHEAD)
        return jnp.concatenate([vt_ref[0, chunk0 + c, rows, :] for c in range(n_chunks)], axis=1)

    def finish(acc, extra=None):
        den = acc[HEAD_DIM:]
        if extra is not None:
            den = den + extra
        return acc[:HEAD_DIM] / den

    def mask_rows(table_ref, u, s0, span, r_lo):
        return table_ref[pl.ds(pl.multiple_of(s0 - t0[u] - r_lo, sq), span), :]

    def emit(u, gi, o):
        cols = slice(gi * GROUP_WIDTH, (gi + 1) * GROUP_WIDTH)
        y_ref[0, u * sq:(u + 1) * sq, cols] = _rms(o.T, gac_ref[:, cols]).astype(bf16)

    sink = lambda g, r: sink_ref[:, g * rep + r:g * rep + r + 1]
    span_w = min(NSA_WINDOW + sq, t_len)
    span_c = min(SWA_WINDOW + sq, t_len)
    s0_w = [pl.multiple_of(jnp.maximum(t0[u] - NSA_WINDOW, 0), sq) for u in subs]
    s0_c = [pl.multiple_of(jnp.maximum(t0[u] - SWA_WINDOW, 0), sq) for u in subs]

    qa = [stack_heads(u, 0, NSA_HEADS) for u in subs]
    s_cmp = [_dot(kc_ref[0], qa[u]) for u in subs]
    s_w = [_dot(kk_ref[0, pl.ds(s0_w[u], span_w), HEAD_DIM:2 * HEAD_DIM], qa[u]) for u in subs]
    s_c = [[_dot(kk_ref[0, pl.ds(s0_c[u], span_c), (2 + g) * HEAD_DIM:(3 + g) * HEAD_DIM],
                 stack_heads(u, NSA_HEADS + g * rep, rep)) for g in kv_groups] for u in subs]

    n_i = lax.broadcasted_iota(jnp.int32, (nch, NSA_HEADS * sq), 0)
    c_lane = lax.broadcasted_iota(jnp.int32, (nch, NSA_HEADS * sq), 1) & (sq - 1)
    c_valid = [(n_i * CMP_STRIDE + (CMP_LEN - 1) <= t0[u] + c_lane) & (n_i < nch - 1) for u in subs]
    z_cmp = [jnp.where(c_valid[u], s_cmp[u], NEG) for u in subs]
    m_cmp = [jnp.max(z_cmp[u], axis=0, keepdims=True) for u in subs]
    z_w, m_w, z_c, m_c = [], [], [], []
    for u in subs:
        bias = mask_rows(cwin_ref, u, s0_w[u], span_w, -NSA_WINDOW)
        z_w.append(jnp.concatenate([lane_head(s_w[u], h) + bias for h in range(NSA_HEADS)], axis=1))
        m_w.append(jnp.max(z_w[u], axis=0, keepdims=True))
        bias = mask_rows(cswa_ref, u, s0_c[u], span_c, -SWA_WINDOW)
        z_c.append([[lane_head(s_c[u][g], r) + bias for r in range(rep)] for g in kv_groups])
        m_c.append([[jnp.maximum(jnp.max(z_c[u][g][r], axis=0, keepdims=True), sink(g, r))
                     for r in range(rep)] for g in kv_groups])

    e_cmp = [jnp.exp2(z_cmp[u] - m_cmp[u]) for u in subs]
    p_w = [jnp.exp2(z_w[u] - m_w[u]).astype(bf16) for u in subs]
    p_c = [[jnp.concatenate([jnp.exp2(z_c[u][g][r] - m_c[u][g][r]).astype(bf16) for r in range(rep)], axis=1)
            for g in kv_groups] for u in subs]

    p_cmp = [jnp.where(c_valid[u], e_cmp[u] / jnp.sum(e_cmp[u], axis=0, keepdims=True), 0.0) for u in subs]
    o_cmp = [_dot(vct_ref[0], p_cmp[u].astype(bf16)) for u in subs]
    imp = []
    for u in subs:
        p_sum = lane_head(p_cmp[u], 0)
        for h in range(1, NSA_HEADS):
            p_sum = p_sum + lane_head(p_cmp[u], h)
        p_hi, p_lo = _split(p_sum)
        imp.append(_dot(ovt_ref[...], p_hi) + _dot(ovt_ref[...], p_lo))
    o_win = [finish(_dot(values(s0_w[u] // LANES, span_w // LANES, 1), p_w[u])) for u in subs]
    for u in subs:
        o_c = []
        for g in kv_groups:
            acc = _dot(values(s0_c[u] // LANES, span_c // LANES, 2 + g), p_c[u][g])
            o_c += [finish(lane_head(acc, r), jnp.exp2(sink(g, r) - m_c[u][g][r])) for r in range(rep)]
        emit(u, 1, jnp.concatenate(o_c, axis=0))

    j = lax.broadcasted_iota(jnp.int32, (n_sel, sq), 0)
    for u in subs:
        t = t0[u] + lax.broadcasted_iota(jnp.int32, (n_sel, sq), 1)
        cur = jnp.right_shift(t, SLC_SHIFT)
        forced = (j == 0) | (j == cur) | (j == cur - 1)
        score = jnp.where(forced, FORCE, jnp.where(j * SLC_BLOCK > t, -1.0, imp[u]))
        tiles = [slice(r * SUBLANES, (r + 1) * SUBLANES) for r in range(n_sel // SUBLANES)]
        rank = [jnp.zeros((SUBLANES, sq), f32) for _ in tiles]
        for jp in range(n_sel):
            other = score[jp:jp + 1, :]
            for r, rows in enumerate(tiles):
                mine = score[rows]
                if rows.stop <= jp:
                    ahead = other > mine
                elif rows.start > jp:
                    ahead = other >= mine
                else:
                    ahead = (other > mine) | ((other == mine) & (j[rows] > jp))
                rank[r] = rank[r] + jnp.where(ahead, 1.0, 0.0)
        negsel_ref[u] = jnp.where(jnp.concatenate(rank, axis=0) < float(min(SLC_TOP, n_sel)), 0.0, NEG)

    def selected(n):
        def scores(k):
            k_seg = kk_ref[0, k * seg:(k + 1) * seg, 0:HEAD_DIM]
            return [_dot(k_seg, qa[u]) for u in subs]

        def masked(k, s):
            z = []
            for u in subs:
                bias = jnp.concatenate(
                    [jnp.broadcast_to(negsel_ref[u, b:b + 1, :], (SLC_BLOCK, sq))
                     for b in range(k * (seg // SLC_BLOCK), (k + 1) * (seg // SLC_BLOCK))], axis=0)
                if k == n - 1:
                    bias = bias + mask_rows(csel_ref, u, k * seg, seg, -seg)
                z.append(jnp.concatenate([lane_head(s[u], h) + bias for h in range(NSA_HEADS)], axis=1))
            return z

        m = [jnp.full((1, NSA_HEADS * sq), NEG, f32) for _ in subs]
        acc = [jnp.zeros((VHEAD, NSA_HEADS * sq), f32) for _ in subs]
        s_next = scores(0)
        for k in range(n):
            s_cur = s_next
            if k + 1 < n:
                s_next = scores(k + 1)
            z = masked(k, s_cur)
            m_new = [jnp.maximum(m[u], jnp.max(z[u], axis=0, keepdims=True)) for u in subs]
            p = [jnp.exp2(z[u] - m_new[u]).astype(bf16) for u in subs]
            v_seg = values(k * (seg // LANES), seg // LANES, 0)
            acc = [jnp.exp2(m[u] - m_new[u]) * acc[u] + _dot(v_seg, p[u]) for u in subs]
            m = m_new
        for u in subs:
            o_slc = finish(acc[u])
            gt = gt_ref[0, :, u * sq:(u + 1) * sq]
            o_a = []
            for h in range(NSA_HEADS):
                g = lambda k: gt[NSA_BRANCHES * h + k:NSA_BRANCHES * h + k + 1, :]
                o_a.append(g(0) * lane_head(o_cmp[u], h) + g(1) * lane_head(o_slc, h)
                           + g(2) * lane_head(o_win[u], h))
            emit(u, 0, jnp.concatenate(o_a, axis=0))

    n_seg = (tile0 + tile - 1) // seg + 1
    for n in range(1, t_len // seg + 1):
        pl.when(n_seg == n)(functools.partial(selected, n))


def _attn(qt, kk, vt, kc, vct, gt, sinks, gac, ovt, tables):
    b, _, t = qt.shape
    tq = Q_TILE
    full = lambda a: pl.BlockSpec((1,) + a.shape[1:], lambda i, j: (i,) + (0,) * (a.ndim - 1))
    feat = lambda a: pl.BlockSpec((1, a.shape[1], tq), lambda i, j: (i, 0, j))
    width = 2 * GROUP_WIDTH
    return pl.pallas_call(
        _attn_kernel,
        grid=(b, t // tq),
        in_specs=[feat(qt), full(kk), full(vt), full(kc), full(vct), feat(gt),
                  _const_spec(sinks.shape), _const_spec(gac.shape), _const_spec(ovt.shape)]
                 + [_const_spec(c.shape) for c in tables],
        out_specs=pl.BlockSpec((1, tq, width), lambda i, j: (i, j, 0)),
        out_shape=jax.ShapeDtypeStruct((b, t, width), bf16),
        scratch_shapes=[pltpu.VMEM((tq // SUBQ, ovt.shape[0], SUBQ), f32)],
        compiler_params=pltpu.CompilerParams(dimension_semantics=("parallel", "parallel"),
                                             vmem_limit_bytes=VMEM_LIMIT),
        name="attn",
    )(qt, kk, vt, kc, vct, gt, sinks, gac, ovt, *tables)


def _overlap_t(t_len):
    n_c = (t_len - CMP_LEN) // CMP_STRIDE + 1
    nch = t_len // CMP_STRIDE
    ci = np.arange(nch)[None, :] * CMP_STRIDE
    sj = np.arange(t_len // SLC_BLOCK)[:, None] * SLC_BLOCK
    ov = (ci <= sj + SLC_BLOCK - 1) & (ci + CMP_LEN - 1 >= sj) & (np.arange(nch)[None, :] < n_c)
    return jnp.asarray(ov, bf16)


def _mask_tables(t_len):
    seg = min(SLC_SEG, t_len)
    c = np.arange(SUBQ)[None, :]

    def table(r_lo, r_hi, visible):
        r = np.arange(r_lo, r_hi)[:, None]
        return jnp.asarray(np.where(visible(r, c), 0.0, NEG), f32)

    band = lambda w: (lambda r, c: (r <= c) & (r > c - w))
    return (table(-NSA_WINDOW, min(NSA_WINDOW + SUBQ, t_len), band(NSA_WINDOW)),
            table(-SWA_WINDOW, min(SWA_WINDOW + SUBQ, t_len), band(SWA_WINDOW)),
            table(-seg, seg, lambda r, c: r <= c))


def _qk_gains(nsa_q, nsa_ks, nsa_kw, swa_q, swa_k):
    q_scale = ATTN_SCALE * LOG2E
    gq = jnp.concatenate([jnp.tile(nsa_q * q_scale, NSA_HEADS), jnp.tile(swa_q * q_scale, SWA_HEADS)])[:, None]
    gk = jnp.concatenate([nsa_ks, nsa_kw, jnp.tile(swa_k, SWA_KV_HEADS)])[None, :]
    return gq, gk


def _block_diag(blocks):
    n, r, c = blocks.shape
    eye = jnp.eye(n, dtype=blocks.dtype)
    return (eye[:, None, :, None] * blocks[:, :, None, :]).reshape(n * r, n * c)


def kernel(x, ffn1_norm, ffn1_w1, ffn1_w3, ffn1_w2, mix_norm, w_in, nsa_q_norm, nsa_kc_norm, nsa_ks_norm, nsa_kw_norm, cmp_pos_k, cmp_w1_k, cmp_w2_k, cmp_pos_v, cmp_w1_v, cmp_w2_v, conv_w, swa_q_norm, swa_k_norm, swa_sinks, pool_w, pool_scale, group_norm, w_out, ffn2_norm, ffn2_w1, ffn2_w3, ffn2_w2):
    b, t, d = x.shape
    depth = w_in.shape[0]
    ovt = _overlap_t(t)
    tables = _mask_tables(t)
    seg = jnp.asarray(np.kron(np.eye(K_COLS // HEAD_DIM), np.ones((HEAD_DIM, HEAD_DIM))), bf16)
    row = lambda v: v[None, :].astype(f32)
    cb = lambda v: v.astype(bf16)
    groups = lambda v, ids: jnp.concatenate([v[i * GROUP_WIDTH:(i + 1) * GROUP_WIDTH] for i in ids], axis=0)

    up_rows, down_rows = 256, ffn1_w2.shape[1] // 4
    ffn1 = (_to_bf16(ffn1_w1, up_rows), _to_bf16(ffn1_w3, up_rows), _to_bf16(ffn1_w2, down_rows))
    ffn2 = (_to_bf16(ffn2_w1, up_rows), _to_bf16(ffn2_w3, up_rows), _to_bf16(ffn2_w2, down_rows))
    wo = _to_bf16(w_out, up_rows)
    w_tok, w_feat = _arrange_w_in(w_in)

    for l in range(depth):
        x = _ffn(x, row(ffn1_norm[l]), *ffn1, layer=l)

        gq, gk = _qk_gains(nsa_q_norm[l], nsa_ks_norm[l], nsa_kw_norm[l], swa_q_norm[l], swa_k_norm[l])
        qt, kk, vt, kvc, gt, ybd = _proj(
            x, row(mix_norm[l]), w_feat, w_tok, l, seg, gq, gk, conv_w[l],
            cb(_block_diag(pool_w[l])), row(pool_scale[l]), row(groups(group_norm[l], (1, 3))))

        pos = lambda p: p.reshape(CMP_LEN // CMP_STRIDE, CMP_STRIDE * HEAD_DIM)
        kc, vct = _compress(kvc, pos(cmp_pos_k[l]), cb(cmp_w1_k[l]), cb(cmp_w2_k[l]), row(nsa_kc_norm[l]),
                            pos(cmp_pos_v[l]), cb(cmp_w1_v[l]), cb(cmp_w2_v[l].T))

        yac = _attn(qt, kk, vt, kc, vct, gt, row(swa_sinks[l] * LOG2E),
                    row(groups(group_norm[l], (0, 2))), ovt, tables)

        mix = (yac, ybd, wo)
        x = _ffn(x, row(ffn2_norm[l]), *ffn2, layer=l, mix=mix)
    return x
```

```python
import functools

import numpy as np
import jax
import jax.numpy as jnp
from jax import lax
from jax.experimental import pallas as pl
from jax.experimental.pallas import tpu as pltpu

HEAD_DIM = 64
GROUP_WIDTH = 256
N_MIX = 4
NSA_HEADS = 4
NSA_BRANCHES = 3
CMP_LEN = 32
CMP_STRIDE = 16
SLC_BLOCK = 64
SLC_TOP = 8
NSA_WINDOW = 512
SWA_HEADS = 4
SWA_KV_HEADS = 2
SWA_WINDOW = 128
POOL_WINDOWS = (2, 4, 8, 16)
FFN_TILE = 512
PROJ_TILE = 1024
Q_TILE = 512
SUBQ = 128
EPS = 1e-6
NEG = -1e30
FORCE = 1e4
TAKEN = -2.0
ATTN_SCALE = HEAD_DIM ** -0.5
LOG2E = float(np.log2(np.e))
SLC_SEG = 512
SLC_SHIFT = SLC_BLOCK.bit_length() - 1
POOL_SHIFT = (GROUP_WIDTH // len(POOL_WINDOWS)).bit_length() - 1

LANES = 128
SUBLANES = 8
CONV_HALO = 8
POOL_HALO = 16
VMEM_LIMIT = 56 * 2 ** 20

QT_ROWS = 512
V_HEADS = 4
VHEAD = 2 * HEAD_DIM
VT_ROWS = V_HEADS * VHEAD
GT_ROWS = 2 * SUBLANES
T_ROWS = QT_ROWS + V_HEADS * HEAD_DIM + GT_ROWS
K_COLS = 256
KC_COLS = 128
B_COLS = 768
D_COLS = 256
OFF_K = 0
OFF_KC = OFF_K + K_COLS
OFF_B = OFF_KC + KC_COLS
OFF_D = OFF_B + B_COLS
W_COLS = OFF_D + D_COLS

SPLIT_SIZES = (256, 64, 64, 64, 64, 64, 64, 12, 256, 256, 256, 256, 128, 128, 256)

f32 = jnp.float32
bf16 = jnp.bfloat16


def _rms(x, g):
    return x * lax.rsqrt(jnp.mean(x * x, axis=-1, keepdims=True) + EPS) * g


def _dot(a, b):
    return jnp.dot(a, b, preferred_element_type=f32)


def _dot_nt(a, b):
    return lax.dot_general(a, b, (((1,), (1,)), ((), ())), preferred_element_type=f32)


def _split(a):
    hi = a.astype(bf16)
    return hi, (a - hi.astype(f32)).astype(bf16)


def _const_spec(shape):
    nd = len(shape)
    return pl.BlockSpec(shape, lambda *_: (0,) * nd, pipeline_mode=pl.Buffered(1))


def _layer_spec(stacked, layer):
    return pl.BlockSpec((None,) + stacked.shape[1:], lambda *_: (layer, 0, 0), pipeline_mode=pl.Buffered(1))


def _cast_kernel(w_ref, o_ref):
    o_ref[...] = w_ref[...].astype(bf16)


def _to_bf16(w, rows):
    depth, r, c = w.shape
    spec = pl.BlockSpec((1, rows, c), lambda l, i: (l, i, 0))
    return pl.pallas_call(
        _cast_kernel,
        grid=(depth, r // rows),
        in_specs=[spec],
        out_specs=spec,
        out_shape=jax.ShapeDtypeStruct(w.shape, bf16),
        compiler_params=pltpu.CompilerParams(dimension_semantics=("parallel", "parallel"),
                                             vmem_limit_bytes=VMEM_LIMIT),
        name="cast",
    )(w)


def _arrange_kernel(w_ref, tok_ref, feat_ref):
    w = w_ref[0]
    offs = np.concatenate([[0], np.cumsum(SPLIT_SIZES)])
    (a_q, a_kc, a_vc, a_ks, a_vs, a_kw, a_vw, a_g,
     b_b, b_c, b_x, c_q, c_k, c_v, d_v) = [w[:, offs[k]:offs[k + 1]] for k in range(len(SPLIT_SIZES))]
    tok_ref[0] = jnp.concatenate([a_ks, a_kw, c_k, a_kc, a_vc, b_b, b_c, b_x, d_v], axis=1).astype(bf16)
    feat = [a_q, c_q, a_vs, a_vw, c_v, a_g]
    width = sum(p.shape[1] for p in feat)
    padded = -(-width // LANES) * LANES
    feat = jnp.concatenate(feat + [jnp.zeros((w.shape[0], padded - width), f32)], axis=1)
    feat_ref[0] = feat.T[:T_ROWS].astype(bf16)


def _arrange_w_in(w_in):
    depth, d, width = w_in.shape
    return pl.pallas_call(
        _arrange_kernel,
        grid=(depth,),
        in_specs=[pl.BlockSpec((1, d, width), lambda l: (l, 0, 0))],
        out_specs=[pl.BlockSpec((1, d, W_COLS), lambda l: (l, 0, 0)),
                   pl.BlockSpec((1, T_ROWS, d), lambda l: (l, 0, 0))],
        out_shape=[jax.ShapeDtypeStruct((depth, d, W_COLS), bf16),
                   jax.ShapeDtypeStruct((depth, T_ROWS, d), bf16)],
        compiler_params=pltpu.CompilerParams(dimension_semantics=("parallel",),
                                             vmem_limit_bytes=VMEM_LIMIT),
        name="arrange_w_in",
    )(w_in)


def _ffn_kernel(*refs, with_mix):
    if with_mix:
        x_ref, ya_ref, yb_ref, wo_ref, g_ref, w1_ref, w3_ref, w2_ref, o_ref = refs
        x = x_ref[0]
        for gi, y_ref in ((0, ya_ref), (1, yb_ref)):
            for half in range(2):
                rows = slice((gi + 2 * half) * GROUP_WIDTH, (gi + 2 * half + 1) * GROUP_WIDTH)
                x = x + _dot(y_ref[0, :, half * GROUP_WIDTH:(half + 1) * GROUP_WIDTH], wo_ref[rows, :])
    else:
        x_ref, g_ref, w1_ref, w3_ref, w2_ref, o_ref = refs
        x = x_ref[0]
    halves = [slice(0, x.shape[0] // 2), slice(x.shape[0] // 2, x.shape[0])]
    hs = [_rms(x[r], g_ref[...]).astype(bf16) for r in halves]
    a = [_dot(h, w1_ref[...]) for h in hs]
    b = [_dot(h, w3_ref[...]) for h in hs]
    act = [(a[i] * jax.nn.sigmoid(a[i]) * b[i]).astype(bf16) for i in range(len(halves))]
    for i, r in enumerate(halves):
        o_ref[0, r, :] = x[r] + 0.5 * _dot(act[i], w2_ref[...])


def _ffn(x, gain, w1, w3, w2, layer, mix=None, tm=FFN_TILE):
    b, t, d = x.shape
    row = lambda w: pl.BlockSpec((1, tm, w), lambda i, j: (i, j, 0))
    args, specs = [x], [row(d)]
    if mix is not None:
        ya, yb, wo = mix
        args += [ya, yb, wo]
        specs += [row(ya.shape[2]), row(yb.shape[2]), _layer_spec(wo, layer)]
    args += [gain, w1, w3, w2]
    specs += [_layer_spec(w, layer) for w in (gain, w1, w3, w2)]
    return pl.pallas_call(
        functools.partial(_ffn_kernel, with_mix=mix is not None),
        grid=(b, t // tm),
        in_specs=specs,
        out_specs=row(d),
        out_shape=jax.ShapeDtypeStruct((b, t, d), f32),
        compiler_params=pltpu.CompilerParams(dimension_semantics=("parallel", "parallel"),
                                             vmem_limit_bytes=VMEM_LIMIT),
        name="ffn" if mix is None else "ffn_mix",
    )(*args)


def _proj_kernel(x_ref, g_ref, wt_ref, w_ref, seg_ref, gq_ref, gk_ref, cw_ref, pw_ref, ps_ref, gbd_ref,
                 qt_ref, kk_ref, vt_ref, kc_ref, gt_ref, ybd_ref,
                 zhalo_ref, vhalo_ref):
    tt = x_ref.shape[1]
    ti = pl.program_id(1)

    @pl.when(ti == 0)
    def _():
        zhalo_ref[...] = jnp.zeros_like(zhalo_ref)
        vhalo_ref[...] = jnp.zeros_like(vhalo_ref)

    halves = [slice(0, tt // 2), slice(tt // 2, tt)]
    hs = [_rms(x_ref[0, r, :], g_ref[...]).astype(bf16) for r in halves]
    ut = jnp.concatenate([_dot_nt(wt_ref[...], h) for h in hs], axis=1)
    cols = lambda off, n: jnp.concatenate([_dot(h, w_ref[:, off:off + n]) for h in hs], axis=0)
    u_k = cols(OFF_K, K_COLS)
    ub = cols(OFF_B, B_COLS)
    v = cols(OFF_D, D_COLS)
    kc_ref[0] = cols(OFF_KC, KC_COLS)

    for hd in range(QT_ROWS // HEAD_DIM):
        rows = slice(hd * HEAD_DIM, (hd + 1) * HEAD_DIM)
        u = ut[rows]
        ms = jnp.mean(u * u, axis=0, keepdims=True)
        qt_ref[0, rows, :] = (u * lax.rsqrt(ms + EPS) * gq_ref[rows, :]).astype(bf16)
    ones = jnp.ones((HEAD_DIM, LANES), bf16)
    for hv in range(V_HEADS):
        vh = ut[QT_ROWS + hv * HEAD_DIM:QT_ROWS + (hv + 1) * HEAD_DIM].astype(bf16)
        for c in range(tt // LANES):
            vt_ref[0, c, hv * VHEAD:hv * VHEAD + HEAD_DIM, :] = vh[:, c * LANES:(c + 1) * LANES]
            vt_ref[0, c, hv * VHEAD + HEAD_DIM:(hv + 1) * VHEAD, :] = ones
    gt_ref[0] = jax.nn.sigmoid(ut[QT_ROWS + V_HEADS * HEAD_DIM:])

    hi, lo = _split(u_k * u_k)
    ss = _dot(hi, seg_ref[...]) + _dot(lo, seg_ref[...])
    kk_ref[0] = (u_k * lax.rsqrt(ss * (1.0 / HEAD_DIM) + EPS) * gk_ref[...]).astype(bf16)

    s = jnp.concatenate([vhalo_ref[...], v], axis=0)
    grp = jnp.right_shift(lax.broadcasted_iota(jnp.int32, (1, D_COLS), 1), POOL_SHIFT)
    acc = None
    for k, w in enumerate(POOL_WINDOWS):
        s = s + pltpu.roll(s, w // 2, 0)
        acc = s if acc is None else jnp.where(grp >= k, s, acc)
    win = jnp.where(grp == 0, POOL_WINDOWS[0],
                    jnp.where(grp == 1, POOL_WINDOWS[1],
                              jnp.where(grp == 2, POOL_WINDOWS[2], POOL_WINDOWS[3])))
    t = ti * tt + lax.broadcasted_iota(jnp.int32, (tt, 1), 0)
    cnt = jnp.minimum(t + 1, win).astype(f32)
    d = acc[POOL_HALO:] / cnt - v
    o_d = _dot(d.astype(bf16), pw_ref[...]) * ps_ref[...]
    vhalo_ref[...] = v[tt - POOL_HALO:]

    z = ub[:, GROUP_WIDTH:2 * GROUP_WIDTH] * ub[:, 2 * GROUP_WIDTH:]
    ze = jnp.concatenate([zhalo_ref[...], z], axis=0)
    z1 = pltpu.roll(ze, 1, 0)[CONV_HALO:]
    z2 = pltpu.roll(ze, 2, 0)[CONV_HALO:]
    conv = cw_ref[0:1, :] * z2 + cw_ref[1:2, :] * z1 + cw_ref[2:3, :] * z
    o_b = ub[:, :GROUP_WIDTH] * conv
    zhalo_ref[...] = z[tt - CONV_HALO:]

    ybd_ref[0, :, :GROUP_WIDTH] = _rms(o_b, gbd_ref[:, :GROUP_WIDTH]).astype(bf16)
    ybd_ref[0, :, GROUP_WIDTH:] = _rms(o_d, gbd_ref[:, GROUP_WIDTH:]).astype(bf16)


def _proj(x3, gain, wt, w, layer, seg, gq, gk, conv_w, pool_w, pool_scale, gbd, tt=PROJ_TILE):
    b, t, d = x3.shape
    tok = lambda w_: pl.BlockSpec((1, tt, w_), lambda i, j: (i, j, 0))
    feat = lambda r: pl.BlockSpec((1, r, tt), lambda i, j: (i, 0, j))
    stacked = [gain, wt, w]
    per_layer = [gq, gk, conv_w, pool_w, pool_scale, gbd]
    return pl.pallas_call(
        _proj_kernel,
        grid=(b, t // tt),
        in_specs=([tok(d)] + [_layer_spec(c, layer) for c in stacked] + [_const_spec(seg.shape)]
                  + [_layer_spec(c, layer) for c in per_layer]),
        out_specs=[feat(QT_ROWS), tok(K_COLS),
                   pl.BlockSpec((1, tt // LANES, VT_ROWS, LANES), lambda i, j: (i, j, 0, 0)),
                   tok(KC_COLS), feat(GT_ROWS), tok(2 * GROUP_WIDTH)],
        out_shape=[jax.ShapeDtypeStruct((b, QT_ROWS, t), bf16),
                   jax.ShapeDtypeStruct((b, t, K_COLS), bf16),
                   jax.ShapeDtypeStruct((b, t // LANES, VT_ROWS, LANES), bf16),
                   jax.ShapeDtypeStruct((b, t, KC_COLS), f32),
                   jax.ShapeDtypeStruct((b, GT_ROWS, t), f32),
                   jax.ShapeDtypeStruct((b, t, 2 * GROUP_WIDTH), bf16)],
        scratch_shapes=[pltpu.VMEM((CONV_HALO, GROUP_WIDTH), f32), pltpu.VMEM((POOL_HALO, D_COLS), f32)],
        compiler_params=pltpu.CompilerParams(dimension_semantics=("parallel", "arbitrary"),
                                             vmem_limit_bytes=VMEM_LIMIT),
        name="proj",
    )(x3, *stacked, seg, *per_layer)


def _compress_kernel(kvc_ref, pk_ref, w1k_ref, w2k_ref, gk_ref, pv_ref, w1v_ref, w2vt_ref,
                     kc_ref, vct_ref):
    nch = kvc_ref.shape[1] // CMP_STRIDE
    half = CMP_STRIDE * HEAD_DIM
    rows = [kvc_ref[0, pl.ds(r, nch, stride=CMP_STRIDE), :] for r in range(CMP_STRIDE)]
    k_chunks = jnp.concatenate([x[:, :HEAD_DIM] for x in rows], axis=1)
    v_chunks = jnp.concatenate([x[:, HEAD_DIM:] for x in rows], axis=1)

    def hidden(c, pos_ref, w1_ref):
        ha = _dot((c + pos_ref[0:1, :]).astype(bf16), w1_ref[0:half, :])
        hb = _dot((c + pos_ref[1:2, :]).astype(bf16), w1_ref[half:2 * half, :])
        return jax.nn.gelu(ha + pltpu.roll(hb, nch - 1, 0)).astype(bf16)

    kc_ref[0] = _rms(_dot(hidden(k_chunks, pk_ref, w1k_ref), w2k_ref[...]), gk_ref[...]).astype(bf16)
    vct_ref[0] = _dot_nt(w2vt_ref[...], hidden(v_chunks, pv_ref, w1v_ref)).astype(bf16)


def _compress(kvc, layer, pk, w1k, w2k, gk, pv, w1v, w2vt):
    b, t, width = kvc.shape
    nch = t // CMP_STRIDE
    consts = [pk, w1k, w2k, gk, pv, w1v, w2vt]
    return pl.pallas_call(
        _compress_kernel,
        grid=(b,),
        in_specs=[pl.BlockSpec((1, t, width), lambda i: (i, 0, 0))] + [_layer_spec(c, layer) for c in consts],
        out_specs=[pl.BlockSpec((1, nch, HEAD_DIM), lambda i: (i, 0, 0)),
                   pl.BlockSpec((1, HEAD_DIM, nch), lambda i: (i, 0, 0))],
        out_shape=[jax.ShapeDtypeStruct((b, nch, HEAD_DIM), bf16),
                   jax.ShapeDtypeStruct((b, HEAD_DIM, nch), bf16)],
        compiler_params=pltpu.CompilerParams(dimension_semantics=("parallel",),
                                             vmem_limit_bytes=VMEM_LIMIT),
        name="compress",
    )(kvc, *consts)


def _attn_kernel(qt_ref, kk_ref, vt_ref, kc_ref, vct_ref, gt_ref, sink_ref, gac_ref, ovt_ref,
                 cwin_ref, cswa_ref, csel_ref, y_ref, negsel_ref):
    tile = qt_ref.shape[2]
    sq = SUBQ
    subs = range(tile // sq)
    t_len = kk_ref.shape[1]
    nch = kc_ref.shape[1]
    n_sel = ovt_ref.shape[0]
    seg = SLC_SEG
    rep = SWA_HEADS // SWA_KV_HEADS
    kv_groups = range(SWA_KV_HEADS)
    tile0 = pl.program_id(1) * tile
    t0 = [tile0 + u * sq for u in subs]
    lane_head = lambda a, h: a[:, h * sq:(h + 1) * sq]

    def stack_heads(u, first, count):
        return jnp.concatenate(
            [qt_ref[0, (first + h) * HEAD_DIM:(first + h + 1) * HEAD_DIM, u * sq:(u + 1) * sq]
             for h in range(count)], axis=1)

    def values(chunk0, n_chunks, head):
        rows = slice(head * VHEAD, (head + 1) * VHEAD)
        return jnp.concatenate([vt_ref[0, chunk0 + c, rows, :] for c in range(n_chunks)], axis=1)

    def finish(acc, extra=None):
        den = acc[HEAD_DIM:]
        if extra is not None:
            den = den + extra
        return acc[:HEAD_DIM] / den

    def mask_rows(table_ref, u, s0, span, r_lo):
        return table_ref[pl.ds(pl.multiple_of(s0 - t0[u] - r_lo, sq), span), :]

    def emit(u, gi, o):
        cols = slice(gi * GROUP_WIDTH, (gi + 1) * GROUP_WIDTH)
        y_ref[0, u * sq:(u + 1) * sq, cols] = _rms(o.T, gac_ref[:, cols]).astype(bf16)

    sink = lambda g, r: sink_ref[:, g * rep + r:g * rep + r + 1]
    span_w = NSA_WINDOW + sq
    span_c = SWA_WINDOW + sq
    s0_w = [pl.multiple_of(jnp.maximum(t0[u] - NSA_WINDOW, 0), sq) for u in subs]
    s0_c = [pl.multiple_of(jnp.maximum(t0[u] - SWA_WINDOW, 0), sq) for u in subs]

    qa = [stack_heads(u, 0, NSA_HEADS) for u in subs]
    s_cmp = [_dot(kc_ref[0], qa[u]) for u in subs]
    s_w = [_dot(kk_ref[0, pl.ds(s0_w[u], span_w), HEAD_DIM:2 * HEAD_DIM], qa[u]) for u in subs]
    s_c = [[_dot(kk_ref[0, pl.ds(s0_c[u], span_c), (2 + g) * HEAD_DIM:(3 + g) * HEAD_DIM],
                 stack_heads(u, NSA_HEADS + g * rep, rep)) for g in kv_groups] for u in subs]

    n_i = lax.broadcasted_iota(jnp.int32, (nch, NSA_HEADS * sq), 0)
    c_lane = lax.broadcasted_iota(jnp.int32, (nch, NSA_HEADS * sq), 1) & (sq - 1)
    c_valid = [(n_i * CMP_STRIDE + (CMP_LEN - 1) <= t0[u] + c_lane) & (n_i < nch - 1) for u in subs]
    z_cmp = [jnp.where(c_valid[u], s_cmp[u], NEG) for u in subs]
    m_cmp = [jnp.max(z_cmp[u], axis=0, keepdims=True) for u in subs]
    z_w, m_w, z_c, m_c = [], [], [], []
    for u in subs:
        bias = mask_rows(cwin_ref, u, s0_w[u], span_w, -NSA_WINDOW)
        z_w.append(jnp.concatenate([lane_head(s_w[u], h) + bias for h in range(NSA_HEADS)], axis=1))
        m_w.append(jnp.max(z_w[u], axis=0, keepdims=True))
        bias = mask_rows(cswa_ref, u, s0_c[u], span_c, -SWA_WINDOW)
        z_c.append([[lane_head(s_c[u][g], r) + bias for r in range(rep)] for g in kv_groups])
        m_c.append([[jnp.maximum(jnp.max(z_c[u][g][r], axis=0, keepdims=True), sink(g, r))
                     for r in range(rep)] for g in kv_groups])

    e_cmp = [jnp.exp2(z_cmp[u] - m_cmp[u]) for u in subs]
    p_w = [jnp.exp2(z_w[u] - m_w[u]).astype(bf16) for u in subs]
    p_c = [[jnp.concatenate([jnp.exp2(z_c[u][g][r] - m_c[u][g][r]).astype(bf16) for r in range(rep)], axis=1)
            for g in kv_groups] for u in subs]

    p_cmp = [jnp.where(c_valid[u], e_cmp[u] / jnp.sum(e_cmp[u], axis=0, keepdims=True), 0.0) for u in subs]
    o_cmp = [_dot(vct_ref[0], p_cmp[u].astype(bf16)) for u in subs]
    imp = []
    for u in subs:
        p_sum = lane_head(p_cmp[u], 0)
        for h in range(1, NSA_HEADS):
            p_sum = p_sum + lane_head(p_cmp[u], h)
        p_hi, p_lo = _split(p_sum)
        imp.append(_dot(ovt_ref[...], p_hi) + _dot(ovt_ref[...], p_lo))
    o_win = [finish(_dot(values(s0_w[u] // LANES, span_w // LANES, 1), p_w[u])) for u in subs]
    for u in subs:
        o_c = []
        for g in kv_groups:
            acc = _dot(values(s0_c[u] // LANES, span_c // LANES, 2 + g), p_c[u][g])
            o_c += [finish(lane_head(acc, r), jnp.exp2(sink(g, r) - m_c[u][g][r])) for r in range(rep)]
        emit(u, 1, jnp.concatenate(o_c, axis=0))

    j = lax.broadcasted_iota(jnp.int32, (n_sel, sq), 0)
    jf = j.astype(f32)
    for u in subs:
        t = t0[u] + lax.broadcasted_iota(jnp.int32, (n_sel, sq), 1)
        cur = jnp.right_shift(t, SLC_SHIFT)
        forced = (j == 0) | (j == cur) | (j == cur - 1)
        work = jnp.where(forced, FORCE, jnp.where(j * SLC_BLOCK > t, -1.0, imp[u]))
        chosen = jnp.zeros((n_sel, sq), f32)
        for _ in range(min(SLC_TOP, n_sel)):
            top = jnp.max(work, axis=0, keepdims=True)
            first = jnp.min(jnp.where(work == top, jf, float(n_sel)), axis=0, keepdims=True)
            hit = jf == first
            chosen = jnp.where(hit, 1.0, chosen)
            work = jnp.where(hit, TAKEN, work)
        negsel_ref[u] = jnp.where(chosen > 0.5, 0.0, NEG)

    def selected(n):
        def scores(k):
            k_seg = kk_ref[0, k * seg:(k + 1) * seg, 0:HEAD_DIM]
            return [_dot(k_seg, qa[u]) for u in subs]

        def masked(k, s):
            z = []
            for u in subs:
                bias = jnp.concatenate(
                    [jnp.broadcast_to(negsel_ref[u, b:b + 1, :], (SLC_BLOCK, sq))
                     for b in range(k * (seg // SLC_BLOCK), (k + 1) * (seg // SLC_BLOCK))], axis=0)
                if k == n - 1:
                    bias = bias + mask_rows(csel_ref, u, k * seg, seg, -seg)
                z.append(jnp.concatenate([lane_head(s[u], h) + bias for h in range(NSA_HEADS)], axis=1))
            return z

        m = [jnp.full((1, NSA_HEADS * sq), NEG, f32) for _ in subs]
        acc = [jnp.zeros((VHEAD, NSA_HEADS * sq), f32) for _ in subs]
        s_next = scores(0)
        for k in range(n):
            s_cur = s_next
            if k + 1 < n:
                s_next = scores(k + 1)
            z = masked(k, s_cur)
            m_new = [jnp.maximum(m[u], jnp.max(z[u], axis=0, keepdims=True)) for u in subs]
            p = [jnp.exp2(z[u] - m_new[u]).astype(bf16) for u in subs]
            v_seg = values(k * (seg // LANES), seg // LANES, 0)
            acc = [jnp.exp2(m[u] - m_new[u]) * acc[u] + _dot(v_seg, p[u]) for u in subs]
            m = m_new
        for u in subs:
            o_slc = finish(acc[u])
            gt = gt_ref[0, :, u * sq:(u + 1) * sq]
            o_a = []
            for h in range(NSA_HEADS):
                g = lambda k: gt[NSA_BRANCHES * h + k:NSA_BRANCHES * h + k + 1, :]
                o_a.append(g(0) * lane_head(o_cmp[u], h) + g(1) * lane_head(o_slc, h)
                           + g(2) * lane_head(o_win[u], h))
            emit(u, 0, jnp.concatenate(o_a, axis=0))

    n_seg = (tile0 + tile - 1) // seg + 1
    for n in range(1, t_len // seg + 1):
        pl.when(n_seg == n)(functools.partial(selected, n))


def _attn(qt, kk, vt, kc, vct, gt, layer, sinks, gac, ovt, tables):
    b, _, t = qt.shape
    tq = Q_TILE
    assert SLC_SEG % tq == 0
    full = lambda a: pl.BlockSpec((1,) + a.shape[1:], lambda i, j: (i,) + (0,) * (a.ndim - 1))
    feat = lambda a: pl.BlockSpec((1, a.shape[1], tq), lambda i, j: (i, 0, j))
    width = 2 * GROUP_WIDTH
    return pl.pallas_call(
        _attn_kernel,
        grid=(b, t // tq),
        in_specs=[feat(qt), full(kk), full(vt), full(kc), full(vct), feat(gt),
                  _layer_spec(sinks, layer), _layer_spec(gac, layer), _const_spec(ovt.shape)]
                 + [_const_spec(c.shape) for c in tables],
        out_specs=pl.BlockSpec((1, tq, width), lambda i, j: (i, j, 0)),
        out_shape=jax.ShapeDtypeStruct((b, t, width), bf16),
        scratch_shapes=[pltpu.VMEM((tq // SUBQ, ovt.shape[0], SUBQ), f32)],
        compiler_params=pltpu.CompilerParams(dimension_semantics=("parallel", "parallel"),
                                             vmem_limit_bytes=VMEM_LIMIT),
        name="attn",
    )(qt, kk, vt, kc, vct, gt, sinks, gac, ovt, *tables)


def _overlap_t(t_len):
    n_c = (t_len - CMP_LEN) // CMP_STRIDE + 1
    nch = t_len // CMP_STRIDE
    ci = np.arange(nch)[None, :] * CMP_STRIDE
    sj = np.arange(t_len // SLC_BLOCK)[:, None] * SLC_BLOCK
    ov = (ci <= sj + SLC_BLOCK - 1) & (ci + CMP_LEN - 1 >= sj) & (np.arange(nch)[None, :] < n_c)
    return jnp.asarray(ov, bf16)


def _mask_tables():
    c = np.arange(SUBQ)[None, :]

    def table(r_lo, r_hi, visible):
        r = np.arange(r_lo, r_hi)[:, None]
        return jnp.asarray(np.where(visible(r, c), 0.0, NEG), f32)

    band = lambda w: (lambda r, c: (r <= c) & (r > c - w))
    return (table(-NSA_WINDOW, NSA_WINDOW + SUBQ, band(NSA_WINDOW)),
            table(-SWA_WINDOW, SWA_WINDOW + SUBQ, band(SWA_WINDOW)),
            table(-SLC_SEG, SLC_SEG, lambda r, c: r <= c))


def _block_diag(blocks):
    n, r, c = blocks.shape
    eye = jnp.eye(n, dtype=blocks.dtype)
    return (eye[:, None, :, None] * blocks[:, :, None, :]).reshape(n * r, n * c)


def kernel(x, ffn1_norm, ffn1_w1, ffn1_w3, ffn1_w2, mix_norm, w_in, nsa_q_norm, nsa_kc_norm, nsa_ks_norm, nsa_kw_norm, cmp_pos_k, cmp_w1_k, cmp_w2_k, cmp_pos_v, cmp_w1_v, cmp_w2_v, conv_w, swa_q_norm, swa_k_norm, swa_sinks, pool_w, pool_scale, group_norm, w_out, ffn2_norm, ffn2_w1, ffn2_w3, ffn2_w2):
    b, t, d = x.shape
    depth = w_in.shape[0]
    assert t % max(FFN_TILE, PROJ_TILE, Q_TILE, SLC_SEG) == 0 and t >= NSA_WINDOW + SUBQ
    assert d == N_MIX * GROUP_WIDTH and w_in.shape[2] == sum(SPLIT_SIZES)
    ovt = _overlap_t(t)
    tables = _mask_tables()
    seg = jnp.asarray(np.kron(np.eye(K_COLS // HEAD_DIM), np.ones((HEAD_DIM, HEAD_DIM))), bf16)
    cb = lambda v: v.astype(bf16)

    up_rows, down_rows = 512, ffn1_w2.shape[1] // 2
    ffn1 = (_to_bf16(ffn1_w1, up_rows), _to_bf16(ffn1_w3, up_rows), _to_bf16(ffn1_w2, down_rows))
    ffn2 = (_to_bf16(ffn2_w1, up_rows), _to_bf16(ffn2_w3, up_rows), _to_bf16(ffn2_w2, down_rows))
    wo = _to_bf16(w_out, up_rows)
    w_tok, w_feat = _arrange_w_in(w_in)

    vec = lambda v: v[:, None, :].astype(f32)
    groups = lambda v, ids: jnp.concatenate([v[:, i * GROUP_WIDTH:(i + 1) * GROUP_WIDTH] for i in ids], axis=1)
    q_scale = ATTN_SCALE * LOG2E
    gq = jnp.concatenate([jnp.tile(nsa_q_norm * q_scale, (1, NSA_HEADS)),
                          jnp.tile(swa_q_norm * q_scale, (1, SWA_HEADS))], axis=1)[:, :, None]
    gk = vec(jnp.concatenate([nsa_ks_norm, nsa_kw_norm, jnp.tile(swa_k_norm, (1, SWA_KV_HEADS))], axis=1))
    pool_bd = jax.vmap(_block_diag)(pool_w).astype(bf16)
    pos = lambda p: p.reshape(depth, CMP_LEN // CMP_STRIDE, CMP_STRIDE * HEAD_DIM)
    cmp_consts = (pos(cmp_pos_k), cb(cmp_w1_k), cb(cmp_w2_k), vec(nsa_kc_norm),
                  pos(cmp_pos_v), cb(cmp_w1_v), cb(jnp.swapaxes(cmp_w2_v, 1, 2)))
    n1, nm, n2 = vec(ffn1_norm), vec(mix_norm), vec(ffn2_norm)
    ps, gbd, gac = vec(pool_scale), vec(groups(group_norm, (1, 3))), vec(groups(group_norm, (0, 2)))
    sinks = vec(swa_sinks * LOG2E)

    for l in range(depth):
        x = _ffn(x, n1, *ffn1, layer=l)
        qt, kk, vt, kvc, gt, ybd = _proj(x, nm, w_feat, w_tok, l, seg, gq, gk, conv_w, pool_bd, ps, gbd)
        kc, vct = _compress(kvc, l, *cmp_consts)
        yac = _attn(qt, kk, vt, kc, vct, gt, l, sinks, gac, ovt, tables)
        x = _ffn(x, n2, *ffn2, layer=l, mix=(yac, ybd, wo))
    return x
```

```python
import functools

import numpy as np
import jax
import jax.numpy as jnp
from jax import lax
from jax.experimental import pallas as pl
from jax.experimental.pallas import tpu as pltpu

HEAD_DIM = 64
GROUP_WIDTH = 256
N_MIX = 4
NSA_HEADS = 4
NSA_BRANCHES = 3
CMP_LEN = 32
CMP_STRIDE = 16
SLC_BLOCK = 64
SLC_TOP = 8
NSA_WINDOW = 512
SWA_HEADS = 4
SWA_KV_HEADS = 2
SWA_WINDOW = 128
POOL_WINDOWS = (2, 4, 8, 16)
FFN_TILE = 512
PROJ_TILE = 1024
Q_TILE = 512
SUBQ = 128
EPS = 1e-6
NEG = -1e30
FORCE = 1e4
TAKEN = -2.0
ATTN_SCALE = HEAD_DIM ** -0.5
LOG2E = float(np.log2(np.e))
SLC_SEG = 512
SLC_SHIFT = SLC_BLOCK.bit_length() - 1
POOL_SHIFT = (GROUP_WIDTH // len(POOL_WINDOWS)).bit_length() - 1

LANES = 128
SUBLANES = 8
CONV_HALO = 8
POOL_HALO = 16
VMEM_LIMIT = 56 * 2 ** 20

QT_ROWS = 512
V_HEADS = 4
VHEAD = 2 * HEAD_DIM
VT_ROWS = V_HEADS * VHEAD
GT_ROWS = 2 * SUBLANES
T_ROWS = QT_ROWS + V_HEADS * HEAD_DIM + GT_ROWS
K_COLS = 256
KC_COLS = 128
B_COLS = 768
D_COLS = 256
OFF_K = 0
OFF_KC = OFF_K + K_COLS
OFF_B = OFF_KC + KC_COLS
OFF_D = OFF_B + B_COLS
W_COLS = OFF_D + D_COLS

SPLIT_SIZES = (256, 64, 64, 64, 64, 64, 64, 12, 256, 256, 256, 256, 128, 128, 256)

f32 = jnp.float32
bf16 = jnp.bfloat16


def _rms(x, g):
    return x * lax.rsqrt(jnp.mean(x * x, axis=-1, keepdims=True) + EPS) * g


def _dot(a, b):
    return jnp.dot(a, b, preferred_element_type=f32)


def _dot_nt(a, b):
    return lax.dot_general(a, b, (((1,), (1,)), ((), ())), preferred_element_type=f32)


def _split(a):
    hi = a.astype(bf16)
    return hi, (a - hi.astype(f32)).astype(bf16)


def _const_spec(shape):
    nd = len(shape)
    return pl.BlockSpec(shape, lambda *_: (0,) * nd, pipeline_mode=pl.Buffered(1))


def _layer_spec(stacked, layer):
    return pl.BlockSpec((None,) + stacked.shape[1:], lambda *_: (layer, 0, 0), pipeline_mode=pl.Buffered(1))


def _cast_kernel(w_ref, o_ref):
    o_ref[...] = w_ref[...].astype(bf16)


def _to_bf16(w, rows):
    depth, r, c = w.shape
    spec = pl.BlockSpec((1, rows, c), lambda l, i: (l, i, 0))
    return pl.pallas_call(
        _cast_kernel,
        grid=(depth, r // rows),
        in_specs=[spec],
        out_specs=spec,
        out_shape=jax.ShapeDtypeStruct(w.shape, bf16),
        compiler_params=pltpu.CompilerParams(dimension_semantics=("parallel", "parallel"),
                                             vmem_limit_bytes=VMEM_LIMIT),
        name="cast",
    )(w)


def _arrange_kernel(w_ref, tok_ref, feat_ref):
    w = w_ref[0]
    offs = np.concatenate([[0], np.cumsum(SPLIT_SIZES)])
    (a_q, a_kc, a_vc, a_ks, a_vs, a_kw, a_vw, a_g,
     b_b, b_c, b_x, c_q, c_k, c_v, d_v) = [w[:, offs[k]:offs[k + 1]] for k in range(len(SPLIT_SIZES))]
    tok_ref[0] = jnp.concatenate([a_ks, a_kw, c_k, a_kc, a_vc, b_b, b_c, b_x, d_v], axis=1).astype(bf16)
    feat = [a_q, c_q, a_vs, a_vw, c_v, a_g]
    width = sum(p.shape[1] for p in feat)
    padded = -(-width // LANES) * LANES
    feat = jnp.concatenate(feat + [jnp.zeros((w.shape[0], padded - width), f32)], axis=1)
    feat_ref[0] = feat.T[:T_ROWS].astype(bf16)


def _arrange_w_in(w_in):
    depth, d, width = w_in.shape
    return pl.pallas_call(
        _arrange_kernel,
        grid=(depth,),
        in_specs=[pl.BlockSpec((1, d, width), lambda l: (l, 0, 0))],
        out_specs=[pl.BlockSpec((1, d, W_COLS), lambda l: (l, 0, 0)),
                   pl.BlockSpec((1, T_ROWS, d), lambda l: (l, 0, 0))],
        out_shape=[jax.ShapeDtypeStruct((depth, d, W_COLS), bf16),
                   jax.ShapeDtypeStruct((depth, T_ROWS, d), bf16)],
        compiler_params=pltpu.CompilerParams(dimension_semantics=("parallel",),
                                             vmem_limit_bytes=VMEM_LIMIT),
        name="arrange_w_in",
    )(w_in)


def _ffn_kernel(*refs, with_mix):
    if with_mix:
        x_ref, ya_ref, yb_ref, wo_ref, g_ref, w1_ref, w3_ref, w2_ref, o_ref = refs
        x = x_ref[0]
        for gi, y_ref in ((0, ya_ref), (1, yb_ref)):
            for half in range(2):
                rows = slice((gi + 2 * half) * GROUP_WIDTH, (gi + 2 * half + 1) * GROUP_WIDTH)
                x = x + _dot(y_ref[0, :, half * GROUP_WIDTH:(half + 1) * GROUP_WIDTH], wo_ref[rows, :])
    else:
        x_ref, g_ref, w1_ref, w3_ref, w2_ref, o_ref = refs
        x = x_ref[0]
    halves = [slice(0, x.shape[0] // 2), slice(x.shape[0] // 2, x.shape[0])]
    hs = [_rms(x[r], g_ref[...]).astype(bf16) for r in halves]
    a = [_dot(h, w1_ref[...]) for h in hs]
    b = [_dot(h, w3_ref[...]) for h in hs]
    act = [(a[i] * jax.nn.sigmoid(a[i]) * b[i]).astype(bf16) for i in range(len(halves))]
    for i, r in enumerate(halves):
        o_ref[0, r, :] = x[r] + 0.5 * _dot(act[i], w2_ref[...])


def _ffn(x, gain, w1, w3, w2, layer, mix=None, tm=FFN_TILE):
    b, t, d = x.shape
    row = lambda w: pl.BlockSpec((1, tm, w), lambda i, j: (i, j, 0))
    args, specs = [x], [row(d)]
    if mix is not None:
        ya, yb, wo = mix
        args += [ya, yb, wo]
        specs += [row(ya.shape[2]), row(yb.shape[2]), _layer_spec(wo, layer)]
    args += [gain, w1, w3, w2]
    specs += [_layer_spec(w, layer) for w in (gain, w1, w3, w2)]
    return pl.pallas_call(
        functools.partial(_ffn_kernel, with_mix=mix is not None),
        grid=(b, t // tm),
        in_specs=specs,
        out_specs=row(d),
        out_shape=jax.ShapeDtypeStruct((b, t, d), f32),
        compiler_params=pltpu.CompilerParams(dimension_semantics=("parallel", "parallel"),
                                             vmem_limit_bytes=VMEM_LIMIT),
        name="ffn" if mix is None else "ffn_mix",
    )(*args)


def _proj_kernel(x_ref, g_ref, wt_ref, w_ref, seg_ref, gq_ref, gk_ref, cw_ref, pw_ref, ps_ref, gbd_ref,
                 qt_ref, kk_ref, vt_ref, kc_ref, gt_ref, ybd_ref,
                 zhalo_ref, vhalo_ref):
    tt = x_ref.shape[1]
    ti = pl.program_id(1)

    @pl.when(ti == 0)
    def _():
        zhalo_ref[...] = jnp.zeros_like(zhalo_ref)
        vhalo_ref[...] = jnp.zeros_like(vhalo_ref)

    halves = [slice(0, tt // 2), slice(tt // 2, tt)]
    hs = [_rms(x_ref[0, r, :], g_ref[...]).astype(bf16) for r in halves]
    ut = jnp.concatenate([_dot_nt(wt_ref[...], h) for h in hs], axis=1)
    cols = lambda off, n: jnp.concatenate([_dot(h, w_ref[:, off:off + n]) for h in hs], axis=0)
    u_k = cols(OFF_K, K_COLS)
    ub = cols(OFF_B, B_COLS)
    v = cols(OFF_D, D_COLS)
    kc_ref[0] = cols(OFF_KC, KC_COLS)

    for hd in range(QT_ROWS // HEAD_DIM):
        rows = slice(hd * HEAD_DIM, (hd + 1) * HEAD_DIM)
        u = ut[rows]
        ms = jnp.mean(u * u, axis=0, keepdims=True)
        qt_ref[0, rows, :] = (u * lax.rsqrt(ms + EPS) * gq_ref[rows, :]).astype(bf16)
    ones = jnp.ones((HEAD_DIM, LANES), bf16)
    for hv in range(V_HEADS):
        vh = ut[QT_ROWS + hv * HEAD_DIM:QT_ROWS + (hv + 1) * HEAD_DIM].astype(bf16)
        for c in range(tt // LANES):
            vt_ref[0, c, hv * VHEAD:hv * VHEAD + HEAD_DIM, :] = vh[:, c * LANES:(c + 1) * LANES]
            vt_ref[0, c, hv * VHEAD + HEAD_DIM:(hv + 1) * VHEAD, :] = ones
    gt_ref[0] = jax.nn.sigmoid(ut[QT_ROWS + V_HEADS * HEAD_DIM:])

    hi, lo = _split(u_k * u_k)
    ss = _dot(hi, seg_ref[...]) + _dot(lo, seg_ref[...])
    kk_ref[0] = (u_k * lax.rsqrt(ss * (1.0 / HEAD_DIM) + EPS) * gk_ref[...]).astype(bf16)

    s = jnp.concatenate([vhalo_ref[...], v], axis=0)
    grp = jnp.right_shift(lax.broadcasted_iota(jnp.int32, (1, D_COLS), 1), POOL_SHIFT)
    acc = None
    for k, w in enumerate(POOL_WINDOWS):
        s = s + pltpu.roll(s, w // 2, 0)
        acc = s if acc is None else jnp.where(grp >= k, s, acc)
    win = jnp.where(grp == 0, POOL_WINDOWS[0],
                    jnp.where(grp == 1, POOL_WINDOWS[1],
                              jnp.where(grp == 2, POOL_WINDOWS[2], POOL_WINDOWS[3])))
    t = ti * tt + lax.broadcasted_iota(jnp.int32, (tt, 1), 0)
    cnt = jnp.minimum(t + 1, win).astype(f32)
    d = acc[POOL_HALO:] / cnt - v
    o_d = _dot(d.astype(bf16), pw_ref[...]) * ps_ref[...]
    vhalo_ref[...] = v[tt - POOL_HALO:]

    z = ub[:, GROUP_WIDTH:2 * GROUP_WIDTH] * ub[:, 2 * GROUP_WIDTH:]
    ze = jnp.concatenate([zhalo_ref[...], z], axis=0)
    z1 = pltpu.roll(ze, 1, 0)[CONV_HALO:]
    z2 = pltpu.roll(ze, 2, 0)[CONV_HALO:]
    conv = cw_ref[0:1, :] * z2 + cw_ref[1:2, :] * z1 + cw_ref[2:3, :] * z
    o_b = ub[:, :GROUP_WIDTH] * conv
    zhalo_ref[...] = z[tt - CONV_HALO:]

    ybd_ref[0, :, :GROUP_WIDTH] = _rms(o_b, gbd_ref[:, :GROUP_WIDTH]).astype(bf16)
    ybd_ref[0, :, GROUP_WIDTH:] = _rms(o_d, gbd_ref[:, GROUP_WIDTH:]).astype(bf16)


def _proj(x3, gain, wt, w, layer, seg, gq, gk, conv_w, pool_w, pool_scale, gbd, tt=PROJ_TILE):
    b, t, d = x3.shape
    tok = lambda w_: pl.BlockSpec((1, tt, w_), lambda i, j: (i, j, 0))
    feat = lambda r: pl.BlockSpec((1, r, tt), lambda i, j: (i, 0, j))
    stacked = [gain, wt, w]
    per_layer = [gq, gk, conv_w, pool_w, pool_scale, gbd]
    return pl.pallas_call(
        _proj_kernel,
        grid=(b, t // tt),
        in_specs=([tok(d)] + [_layer_spec(c, layer) for c in stacked] + [_const_spec(seg.shape)]
                  + [_layer_spec(c, layer) for c in per_layer]),
        out_specs=[feat(QT_ROWS), tok(K_COLS),
                   pl.BlockSpec((1, tt // LANES, VT_ROWS, LANES), lambda i, j: (i, j, 0, 0)),
                   tok(KC_COLS), feat(GT_ROWS), tok(2 * GROUP_WIDTH)],
        out_shape=[jax.ShapeDtypeStruct((b, QT_ROWS, t), bf16),
                   jax.ShapeDtypeStruct((b, t, K_COLS), bf16),
                   jax.ShapeDtypeStruct((b, t // LANES, VT_ROWS, LANES), bf16),
                   jax.ShapeDtypeStruct((b, t, KC_COLS), f32),
                   jax.ShapeDtypeStruct((b, GT_ROWS, t), f32),
                   jax.ShapeDtypeStruct((b, t, 2 * GROUP_WIDTH), bf16)],
        scratch_shapes=[pltpu.VMEM((CONV_HALO, GROUP_WIDTH), f32), pltpu.VMEM((POOL_HALO, D_COLS), f32)],
        compiler_params=pltpu.CompilerParams(dimension_semantics=("parallel", "arbitrary"),
                                             vmem_limit_bytes=VMEM_LIMIT),
        name="proj",
    )(x3, *stacked, seg, *per_layer)


def _compress_kernel(kvc_ref, pk_ref, w1k_ref, w2k_ref, gk_ref, pv_ref, w1v_ref, w2vt_ref,
                     kc_ref, vct_ref):
    nch = kvc_ref.shape[1] // CMP_STRIDE
    half = CMP_STRIDE * HEAD_DIM
    rows = [kvc_ref[0, pl.ds(r, nch, stride=CMP_STRIDE), :] for r in range(CMP_STRIDE)]
    k_chunks = jnp.concatenate([x[:, :HEAD_DIM] for x in rows], axis=1)
    v_chunks = jnp.concatenate([x[:, HEAD_DIM:] for x in rows], axis=1)

    def hidden(c, pos_ref, w1_ref):
        ha = _dot((c + pos_ref[0:1, :]).astype(bf16), w1_ref[0:half, :])
        hb = _dot((c + pos_ref[1:2, :]).astype(bf16), w1_ref[half:2 * half, :])
        return jax.nn.gelu(ha + pltpu.roll(hb, nch - 1, 0)).astype(bf16)

    kc_ref[0] = _rms(_dot(hidden(k_chunks, pk_ref, w1k_ref), w2k_ref[...]), gk_ref[...]).astype(bf16)
    vct_ref[0] = _dot_nt(w2vt_ref[...], hidden(v_chunks, pv_ref, w1v_ref)).astype(bf16)


def _compress(kvc, layer, pk, w1k, w2k, gk, pv, w1v, w2vt):
    b, t, width = kvc.shape
    nch = t // CMP_STRIDE
    consts = [pk, w1k, w2k, gk, pv, w1v, w2vt]
    return pl.pallas_call(
        _compress_kernel,
        grid=(b,),
        in_specs=[pl.BlockSpec((1, t, width), lambda i: (i, 0, 0))] + [_layer_spec(c, layer) for c in consts],
        out_specs=[pl.BlockSpec((1, nch, HEAD_DIM), lambda i: (i, 0, 0)),
                   pl.BlockSpec((1, HEAD_DIM, nch), lambda i: (i, 0, 0))],
        out_shape=[jax.ShapeDtypeStruct((b, nch, HEAD_DIM), bf16),
                   jax.ShapeDtypeStruct((b, HEAD_DIM, nch), bf16)],
        compiler_params=pltpu.CompilerParams(dimension_semantics=("parallel",),
                                             vmem_limit_bytes=VMEM_LIMIT),
        name="compress",
    )(kvc, *consts)


def _attn_kernel(qt_ref, kk_ref, vt_ref, kc_ref, vct_ref, gt_ref, sink_ref, gac_ref, ovt_ref,
                 cwin_ref, cswa_ref, ctri_ref, ccmp_ref, y_ref, negsel_ref):
    tile = qt_ref.shape[2]
    sq = SUBQ
    subs = range(tile // sq)
    t_len = kk_ref.shape[1]
    nch = kc_ref.shape[1]
    n_sel = ovt_ref.shape[0]
    seg = SLC_SEG
    rep = SWA_HEADS // SWA_KV_HEADS
    kv_groups = range(SWA_KV_HEADS)
    tile0 = pl.program_id(1) * tile
    t0 = [tile0 + u * sq for u in subs]
    lane_head = lambda a, h: a[:, h * sq:(h + 1) * sq]

    def stack_heads(u, first, count):
        return jnp.concatenate(
            [qt_ref[0, (first + h) * HEAD_DIM:(first + h + 1) * HEAD_DIM, u * sq:(u + 1) * sq]
             for h in range(count)], axis=1)

    def values(chunk0, n_chunks, head):
        rows = slice(head * VHEAD, (head + 1) * VHEAD)
        return jnp.concatenate([vt_ref[0, chunk0 + c, rows, :] for c in range(n_chunks)], axis=1)

    def finish(acc, extra=None):
        den = acc[HEAD_DIM:]
        if extra is not None:
            den = den + extra
        return acc[:HEAD_DIM] / den

    def mask_rows(table_ref, u, s0, span, r_lo):
        return table_ref[pl.ds(pl.multiple_of(s0 - t0[u] - r_lo, sq), span), :]

    def emit(u, gi, o):
        cols = slice(gi * GROUP_WIDTH, (gi + 1) * GROUP_WIDTH)
        y_ref[0, u * sq:(u + 1) * sq, cols] = _rms(o.T, gac_ref[:, cols]).astype(bf16)

    sink = lambda g, r: sink_ref[:, g * rep + r:g * rep + r + 1]
    span_w = NSA_WINDOW + sq
    span_c = SWA_WINDOW + sq
    s0_w = [pl.multiple_of(jnp.maximum(t0[u] - NSA_WINDOW, 0), sq) for u in subs]
    s0_c = [pl.multiple_of(jnp.maximum(t0[u] - SWA_WINDOW, 0), sq) for u in subs]

    qa = [stack_heads(u, 0, NSA_HEADS) for u in subs]
    s_cmp = [_dot(kc_ref[0], qa[u]) for u in subs]
    s_w = [_dot(kk_ref[0, pl.ds(s0_w[u], span_w), HEAD_DIM:2 * HEAD_DIM], qa[u]) for u in subs]
    s_c = [[_dot(kk_ref[0, pl.ds(s0_c[u], span_c), (2 + g) * HEAD_DIM:(3 + g) * HEAD_DIM],
                 stack_heads(u, NSA_HEADS + g * rep, rep)) for g in kv_groups] for u in subs]

    z_cmp = []
    for u in subs:
        start = pl.multiple_of(nch - t0[u] // CMP_STRIDE, SUBLANES)
        bias = ccmp_ref[pl.ds(start, nch), :]
        z_cmp.append(jnp.concatenate([lane_head(s_cmp[u], h) + bias for h in range(NSA_HEADS)], axis=1))
    m_cmp = [jnp.max(z_cmp[u], axis=0, keepdims=True) for u in subs]
    z_w, m_w, z_c, m_c = [], [], [], []
    for u in subs:
        bias = mask_rows(cwin_ref, u, s0_w[u], span_w, -NSA_WINDOW)
        z_w.append(jnp.concatenate([lane_head(s_w[u], h) + bias for h in range(NSA_HEADS)], axis=1))
        m_w.append(jnp.max(z_w[u], axis=0, keepdims=True))
        bias = mask_rows(cswa_ref, u, s0_c[u], span_c, -SWA_WINDOW)
        z_c.append([[lane_head(s_c[u][g], r) + bias for r in range(rep)] for g in kv_groups])
        m_c.append([[jnp.maximum(jnp.max(z_c[u][g][r], axis=0, keepdims=True), sink(g, r))
                     for r in range(rep)] for g in kv_groups])

    e_cmp = [jnp.exp2(z_cmp[u] - m_cmp[u]) for u in subs]
    p_w = [jnp.exp2(z_w[u] - m_w[u]).astype(bf16) for u in subs]
    p_c = [[jnp.concatenate([jnp.exp2(z_c[u][g][r] - m_c[u][g][r]).astype(bf16) for r in range(rep)], axis=1)
            for g in kv_groups] for u in subs]

    p_cmp = [jnp.where(m_cmp[u] > 0.5 * NEG, e_cmp[u] / jnp.sum(e_cmp[u], axis=0, keepdims=True), 0.0)
             for u in subs]
    o_cmp = [_dot(vct_ref[0], p_cmp[u].astype(bf16)) for u in subs]
    imp = []
    for u in subs:
        p_sum = lane_head(p_cmp[u], 0)
        for h in range(1, NSA_HEADS):
            p_sum = p_sum + lane_head(p_cmp[u], h)
        p_hi, p_lo = _split(p_sum)
        imp.append(_dot(ovt_ref[...], p_hi) + _dot(ovt_ref[...], p_lo))
    o_win = [finish(_dot(values(s0_w[u] // LANES, span_w // LANES, 1), p_w[u])) for u in subs]
    for u in subs:
        o_c = []
        for g in kv_groups:
            acc = _dot(values(s0_c[u] // LANES, span_c // LANES, 2 + g), p_c[u][g])
            o_c += [finish(lane_head(acc, r), jnp.exp2(sink(g, r) - m_c[u][g][r])) for r in range(rep)]
        emit(u, 1, jnp.concatenate(o_c, axis=0))

    j = lax.broadcasted_iota(jnp.int32, (n_sel, sq), 0)
    jf = j.astype(f32)
    for u in subs:
        t = t0[u] + lax.broadcasted_iota(jnp.int32, (n_sel, sq), 1)
        cur = jnp.right_shift(t, SLC_SHIFT)
        forced = (j == 0) | (j == cur) | (j == cur - 1)
        work = jnp.where(forced, FORCE, jnp.where(j * SLC_BLOCK > t, -1.0, imp[u]))
        chosen = jnp.zeros((n_sel, sq), f32)
        for _ in range(min(SLC_TOP, n_sel)):
            top = jnp.max(work, axis=0, keepdims=True)
            first = jnp.min(jnp.where(work == top, jf, float(n_sel)), axis=0, keepdims=True)
            hit = jf == first
            chosen = jnp.where(hit, 1.0, chosen)
            work = jnp.where(hit, TAKEN, work)
        negsel_ref[u] = jnp.where(chosen > 0.5, 0.0, NEG)

    def selected(n):
        def key_rows(k, u):
            return k * seg, ((k + 1) * seg if k < n - 1 else k * seg + (u + 1) * sq)

        def scores(k):
            return [_dot(kk_ref[0, slice(*key_rows(k, u)), 0:HEAD_DIM], qa[u]) for u in subs]

        def masked(k, s):
            z = []
            for u in subs:
                lo, hi = key_rows(k, u)
                bias = [jnp.broadcast_to(negsel_ref[u, b:b + 1, :], (SLC_BLOCK, sq))
                        for b in range(lo // SLC_BLOCK, hi // SLC_BLOCK)]
                if k == n - 1:
                    own = sq // SLC_BLOCK
                    bias = bias[:-own] + [jnp.concatenate(bias[-own:], axis=0) + ctri_ref[...]]
                bias = jnp.concatenate(bias, axis=0)
                z.append(jnp.concatenate([lane_head(s[u], h) + bias for h in range(NSA_HEADS)], axis=1))
            return z

        def weighted(k, p):
            out = []
            for u in subs:
                lo, hi = key_rows(k, u)
                out.append(_dot(values(lo // LANES, (hi - lo) // LANES, 0), p[u]))
            return out

        m = [jnp.full((1, NSA_HEADS * sq), NEG, f32) for _ in subs]
        acc = [jnp.zeros((VHEAD, NSA_HEADS * sq), f32) for _ in subs]
        s_next = scores(0)
        for k in range(n):
            s_cur = s_next
            if k + 1 < n:
                s_next = scores(k + 1)
            z = masked(k, s_cur)
            m_new = [jnp.maximum(m[u], jnp.max(z[u], axis=0, keepdims=True)) for u in subs]
            p = [jnp.exp2(z[u] - m_new[u]).astype(bf16) for u in subs]
            pv = weighted(k, p)
            acc = [jnp.exp2(m[u] - m_new[u]) * acc[u] + pv[u] for u in subs]
            m = m_new
        for u in subs:
            o_slc = finish(acc[u])
            gt = gt_ref[0, :, u * sq:(u + 1) * sq]
            o_a = []
            for h in range(NSA_HEADS):
                g = lambda k: gt[NSA_BRANCHES * h + k:NSA_BRANCHES * h + k + 1, :]
                o_a.append(g(0) * lane_head(o_cmp[u], h) + g(1) * lane_head(o_slc, h)
                           + g(2) * lane_head(o_win[u], h))
            emit(u, 0, jnp.concatenate(o_a, axis=0))

    n_seg = (tile0 + tile - 1) // seg + 1
    for n in range(1, t_len // seg + 1):
        pl.when(n_seg == n)(functools.partial(selected, n))


def _attn(qt, kk, vt, kc, vct, gt, layer, sinks, gac, ovt, tables):
    b, _, t = qt.shape
    tq = Q_TILE
    assert SLC_SEG == tq
    full = lambda a: pl.BlockSpec((1,) + a.shape[1:], lambda i, j: (i,) + (0,) * (a.ndim - 1))
    feat = lambda a: pl.BlockSpec((1, a.shape[1], tq), lambda i, j: (i, 0, j))
    width = 2 * GROUP_WIDTH
    return pl.pallas_call(
        _attn_kernel,
        grid=(b, t // tq),
        in_specs=[feat(qt), full(kk), full(vt), full(kc), full(vct), feat(gt),
                  _layer_spec(sinks, layer), _layer_spec(gac, layer), _const_spec(ovt.shape)]
                 + [_const_spec(c.shape) for c in tables],
        out_specs=pl.BlockSpec((1, tq, width), lambda i, j: (i, j, 0)),
        out_shape=jax.ShapeDtypeStruct((b, t, width), bf16),
        scratch_shapes=[pltpu.VMEM((tq // SUBQ, ovt.shape[0], SUBQ), f32)],
        compiler_params=pltpu.CompilerParams(dimension_semantics=("parallel", "parallel"),
                                             vmem_limit_bytes=VMEM_LIMIT),
        name="attn",
    )(qt, kk, vt, kc, vct, gt, sinks, gac, ovt, *tables)


def _overlap_t(t_len):
    n_c = (t_len - CMP_LEN) // CMP_STRIDE + 1
    nch = t_len // CMP_STRIDE
    ci = np.arange(nch)[None, :] * CMP_STRIDE
    sj = np.arange(t_len // SLC_BLOCK)[:, None] * SLC_BLOCK
    ov = (ci <= sj + SLC_BLOCK - 1) & (ci + CMP_LEN - 1 >= sj) & (np.arange(nch)[None, :] < n_c)
    return jnp.asarray(ov, bf16)


def _mask_tables(t_len):
    c = np.arange(SUBQ)[None, :]

    def table(r_lo, r_hi, visible):
        r = np.arange(r_lo, r_hi)[:, None]
        return jnp.asarray(np.where(visible(r, c), 0.0, NEG), f32)

    band = lambda w: (lambda r, c: (r <= c) & (r > c - w))
    nch = t_len // CMP_STRIDE
    return (table(-NSA_WINDOW, NSA_WINDOW + SUBQ, band(NSA_WINDOW)),
            table(-SWA_WINDOW, SWA_WINDOW + SUBQ, band(SWA_WINDOW)),
            table(0, SUBQ, lambda r, c: r <= c),
            table(-nch, nch, lambda n, c: n * CMP_STRIDE + CMP_LEN - 1 <= c))


def _block_diag(blocks):
    n, r, c = blocks.shape
    eye = jnp.eye(n, dtype=blocks.dtype)
    return (eye[:, None, :, None] * blocks[:, :, None, :]).reshape(n * r, n * c)


def kernel(x, ffn1_norm, ffn1_w1, ffn1_w3, ffn1_w2, mix_norm, w_in, nsa_q_norm, nsa_kc_norm, nsa_ks_norm, nsa_kw_norm, cmp_pos_k, cmp_w1_k, cmp_w2_k, cmp_pos_v, cmp_w1_v, cmp_w2_v, conv_w, swa_q_norm, swa_k_norm, swa_sinks, pool_w, pool_scale, group_norm, w_out, ffn2_norm, ffn2_w1, ffn2_w3, ffn2_w2):
    b, t, d = x.shape
    depth = w_in.shape[0]
    assert t % max(FFN_TILE, PROJ_TILE, Q_TILE, SLC_SEG) == 0 and t >= NSA_WINDOW + SUBQ
    assert d == N_MIX * GROUP_WIDTH and w_in.shape[2] == sum(SPLIT_SIZES)
    ovt = _overlap_t(t)
    tables = _mask_tables(t)
    seg = jnp.asarray(np.kron(np.eye(K_COLS // HEAD_DIM), np.ones((HEAD_DIM, HEAD_DIM))), bf16)
    cb = lambda v: v.astype(bf16)

    up_rows, down_rows = 512, ffn1_w2.shape[1] // 2
    ffn1 = (_to_bf16(ffn1_w1, up_rows), _to_bf16(ffn1_w3, up_rows), _to_bf16(ffn1_w2, down_rows))
    ffn2 = (_to_bf16(ffn2_w1, up_rows), _to_bf16(ffn2_w3, up_rows), _to_bf16(ffn2_w2, down_rows))
    wo = _to_bf16(w_out, up_rows)
    w_tok, w_feat = _arrange_w_in(w_in)

    vec = lambda v: v[:, None, :].astype(f32)
    groups = lambda v, ids: jnp.concatenate([v[:, i * GROUP_WIDTH:(i + 1) * GROUP_WIDTH] for i in ids], axis=1)
    q_scale = ATTN_SCALE * LOG2E
    gq = jnp.concatenate([jnp.tile(nsa_q_norm * q_scale, (1, NSA_HEADS)),
                          jnp.tile(swa_q_norm * q_scale, (1, SWA_HEADS))], axis=1)[:, :, None]
    gk = vec(jnp.concatenate([nsa_ks_norm, nsa_kw_norm, jnp.tile(swa_k_norm, (1, SWA_KV_HEADS))], axis=1))
    pool_bd = jax.vmap(_block_diag)(pool_w).astype(bf16)
    pos = lambda p: p.reshape(depth, CMP_LEN // CMP_STRIDE, CMP_STRIDE * HEAD_DIM)
    cmp_consts = (pos(cmp_pos_k), cb(cmp_w1_k), cb(cmp_w2_k), vec(nsa_kc_norm),
                  pos(cmp_pos_v), cb(cmp_w1_v), cb(jnp.swapaxes(cmp_w2_v, 1, 2)))
    n1, nm, n2 = vec(ffn1_norm), vec(mix_norm), vec(ffn2_norm)
    ps, gbd, gac = vec(pool_scale), vec(groups(group_norm, (1, 3))), vec(groups(group_norm, (0, 2)))
    sinks = vec(swa_sinks * LOG2E)

    for l in range(depth):
        x = _ffn(x, n1, *ffn1, layer=l)
        qt, kk, vt, kvc, gt, ybd = _proj(x, nm, w_feat, w_tok, l, seg, gq, gk, conv_w, pool_bd, ps, gbd)
        kc, vct = _compress(kvc, l, *cmp_consts)
        yac = _attn(qt, kk, vt, kc, vct, gt, l, sinks, gac, ovt, tables)
        x = _ffn(x, n2, *ffn2, layer=l, mix=(yac, ybd, wo))
    return x
```

```python
import functools

import numpy as np
import jax
import jax.numpy as jnp
from jax import lax
from jax.experimental import pallas as pl
from jax.experimental.pallas import tpu as pltpu

HEAD_DIM = 64
GROUP_WIDTH = 256
N_MIX = 4
NSA_HEADS = 4
NSA_BRANCHES = 3
CMP_LEN = 32
CMP_STRIDE = 16
SLC_BLOCK = 64
SLC_TOP = 8
NSA_WINDOW = 512
SWA_HEADS = 4
SWA_KV_HEADS = 2
SWA_WINDOW = 128
POOL_WINDOWS = (2, 4, 8, 16)
FFN_TILE = 512
PROJ_TILE = 1024
Q_TILE = 512
SUBQ = 128
EPS = 1e-6
NEG = -1e30
FORCE = 1e4
TAKEN = -2.0
ATTN_SCALE = HEAD_DIM ** -0.5
LOG2E = float(np.log2(np.e))
SLC_SEG = 512
SLC_SHIFT = SLC_BLOCK.bit_length() - 1
POOL_SHIFT = (GROUP_WIDTH // len(POOL_WINDOWS)).bit_length() - 1

LANES = 128
SUBLANES = 8
CONV_HALO = 8
POOL_HALO = 16
VMEM_LIMIT = 56 * 2 ** 20

QT_ROWS = 512
V_HEADS = 4
VHEAD = 2 * HEAD_DIM
VT_ROWS = V_HEADS * VHEAD
GT_ROWS = 2 * SUBLANES
T_ROWS = QT_ROWS + V_HEADS * HEAD_DIM + GT_ROWS
K_COLS = 256
KC_COLS = 128
B_COLS = 768
D_COLS = 256
OFF_K = 0
OFF_KC = OFF_K + K_COLS
OFF_B = OFF_KC + KC_COLS
OFF_D = OFF_B + B_COLS
W_COLS = OFF_D + D_COLS

SPLIT_SIZES = (256, 64, 64, 64, 64, 64, 64, 12, 256, 256, 256, 256, 128, 128, 256)

f32 = jnp.float32
bf16 = jnp.bfloat16


def _rms(x, g):
    return x * lax.rsqrt(jnp.mean(x * x, axis=-1, keepdims=True) + EPS) * g


def _dot(a, b):
    return jnp.dot(a, b, preferred_element_type=f32)


def _dot_nt(a, b):
    return lax.dot_general(a, b, (((1,), (1,)), ((), ())), preferred_element_type=f32)


def _split(a):
    hi = a.astype(bf16)
    return hi, (a - hi.astype(f32)).astype(bf16)


def _const_spec(shape):
    nd = len(shape)
    return pl.BlockSpec(shape, lambda *_: (0,) * nd, pipeline_mode=pl.Buffered(1))


def _layer_spec(stacked, layer):
    return pl.BlockSpec((None,) + stacked.shape[1:], lambda *_: (layer, 0, 0), pipeline_mode=pl.Buffered(1))


def _cast_kernel(w_ref, o_ref):
    o_ref[...] = w_ref[...].astype(bf16)


def _to_bf16(w, rows):
    depth, r, c = w.shape
    spec = pl.BlockSpec((1, rows, c), lambda l, i: (l, i, 0))
    return pl.pallas_call(
        _cast_kernel,
        grid=(depth, r // rows),
        in_specs=[spec],
        out_specs=spec,
        out_shape=jax.ShapeDtypeStruct(w.shape, bf16),
        compiler_params=pltpu.CompilerParams(dimension_semantics=("parallel", "parallel"),
                                             vmem_limit_bytes=VMEM_LIMIT),
        name="cast",
    )(w)


def _arrange_kernel(w_ref, tok_ref, feat_ref):
    w = w_ref[0]
    offs = np.concatenate([[0], np.cumsum(SPLIT_SIZES)])
    (a_q, a_kc, a_vc, a_ks, a_vs, a_kw, a_vw, a_g,
     b_b, b_c, b_x, c_q, c_k, c_v, d_v) = [w[:, offs[k]:offs[k + 1]] for k in range(len(SPLIT_SIZES))]
    tok_ref[0] = jnp.concatenate([a_ks, a_kw, c_k, a_kc, a_vc, b_b, b_c, b_x, d_v], axis=1).astype(bf16)
    feat = [a_q, c_q, a_vs, a_vw, c_v, a_g]
    width = sum(p.shape[1] for p in feat)
    padded = -(-width // LANES) * LANES
    feat = jnp.concatenate(feat + [jnp.zeros((w.shape[0], padded - width), f32)], axis=1)
    feat_ref[0] = feat.T[:T_ROWS].astype(bf16)


def _arrange_w_in(w_in):
    depth, d, width = w_in.shape
    return pl.pallas_call(
        _arrange_kernel,
        grid=(depth,),
        in_specs=[pl.BlockSpec((1, d, width), lambda l: (l, 0, 0))],
        out_specs=[pl.BlockSpec((1, d, W_COLS), lambda l: (l, 0, 0)),
                   pl.BlockSpec((1, T_ROWS, d), lambda l: (l, 0, 0))],
        out_shape=[jax.ShapeDtypeStruct((depth, d, W_COLS), bf16),
                   jax.ShapeDtypeStruct((depth, T_ROWS, d), bf16)],
        compiler_params=pltpu.CompilerParams(dimension_semantics=("parallel",),
                                             vmem_limit_bytes=VMEM_LIMIT),
        name="arrange_w_in",
    )(w_in)


def _ffn_kernel(x_ref, g_ref, w1_ref, w3_ref, w2_ref, o_ref):
    x = x_ref[0]
    halves = [slice(0, x.shape[0] // 2), slice(x.shape[0] // 2, x.shape[0])]
    hs = [_rms(x[r], g_ref[...]).astype(bf16) for r in halves]
    a = [_dot(h, w1_ref[...]) for h in hs]
    b = [_dot(h, w3_ref[...]) for h in hs]
    act = [(a[i] * jax.nn.sigmoid(a[i]) * b[i]).astype(bf16) for i in range(len(halves))]
    for i, r in enumerate(halves):
        o_ref[0, r, :] = x[r] + 0.5 * _dot(act[i], w2_ref[...])


def _ffn(x, gain, w1, w3, w2, layer, tm=FFN_TILE):
    b, t, d = x.shape
    row = pl.BlockSpec((1, tm, d), lambda i, j: (i, j, 0))
    return pl.pallas_call(
        _ffn_kernel,
        grid=(b, t // tm),
        in_specs=[row] + [_layer_spec(w, layer) for w in (gain, w1, w3, w2)],
        out_specs=row,
        out_shape=jax.ShapeDtypeStruct((b, t, d), f32),
        compiler_params=pltpu.CompilerParams(dimension_semantics=("parallel", "parallel"),
                                             vmem_limit_bytes=VMEM_LIMIT),
        name="ffn",
    )(x, gain, w1, w3, w2)


def _proj_kernel(x_ref, g_ref, wt_ref, w_ref, seg_ref, gq_ref, gk_ref, cw_ref, pw_ref, ps_ref, gbd_ref,
                 qt_ref, kk_ref, vt_ref, kc_ref, gt_ref, ybd_ref,
                 zhalo_ref, vhalo_ref):
    tt = x_ref.shape[1]
    ti = pl.program_id(1)

    @pl.when(ti == 0)
    def _():
        zhalo_ref[...] = jnp.zeros_like(zhalo_ref)
        vhalo_ref[...] = jnp.zeros_like(vhalo_ref)

    halves = [slice(0, tt // 2), slice(tt // 2, tt)]
    hs = [_rms(x_ref[0, r, :], g_ref[...]).astype(bf16) for r in halves]
    ut = jnp.concatenate([_dot_nt(wt_ref[...], h) for h in hs], axis=1)
    cols = lambda off, n: jnp.concatenate([_dot(h, w_ref[:, off:off + n]) for h in hs], axis=0)
    u_k = cols(OFF_K, K_COLS)
    ub = cols(OFF_B, B_COLS)
    v = cols(OFF_D, D_COLS)
    kc_ref[0] = cols(OFF_KC, KC_COLS)

    for hd in range(QT_ROWS // HEAD_DIM):
        rows = slice(hd * HEAD_DIM, (hd + 1) * HEAD_DIM)
        u = ut[rows]
        ms = jnp.mean(u * u, axis=0, keepdims=True)
        qt_ref[0, rows, :] = (u * lax.rsqrt(ms + EPS) * gq_ref[rows, :]).astype(bf16)
    ones = jnp.ones((HEAD_DIM, LANES), bf16)
    for hv in range(V_HEADS):
        vh = ut[QT_ROWS + hv * HEAD_DIM:QT_ROWS + (hv + 1) * HEAD_DIM].astype(bf16)
        for c in range(tt // LANES):
            vt_ref[0, c, hv * VHEAD:hv * VHEAD + HEAD_DIM, :] = vh[:, c * LANES:(c + 1) * LANES]
            vt_ref[0, c, hv * VHEAD + HEAD_DIM:(hv + 1) * VHEAD, :] = ones
    gt_ref[0] = jax.nn.sigmoid(ut[QT_ROWS + V_HEADS * HEAD_DIM:])

    hi, lo = _split(u_k * u_k)
    ss = _dot(hi, seg_ref[...]) + _dot(lo, seg_ref[...])
    kk_ref[0] = (u_k * lax.rsqrt(ss * (1.0 / HEAD_DIM) + EPS) * gk_ref[...]).astype(bf16)

    s = jnp.concatenate([vhalo_ref[...], v], axis=0)
    grp = jnp.right_shift(lax.broadcasted_iota(jnp.int32, (1, D_COLS), 1), POOL_SHIFT)
    acc = None
    for k, w in enumerate(POOL_WINDOWS):
        s = s + pltpu.roll(s, w // 2, 0)
        acc = s if acc is None else jnp.where(grp >= k, s, acc)
    win = jnp.where(grp == 0, POOL_WINDOWS[0],
                    jnp.where(grp == 1, POOL_WINDOWS[1],
                              jnp.where(grp == 2, POOL_WINDOWS[2], POOL_WINDOWS[3])))
    t = ti * tt + lax.broadcasted_iota(jnp.int32, (tt, 1), 0)
    cnt = jnp.minimum(t + 1, win).astype(f32)
    d = acc[POOL_HALO:] / cnt - v
    o_d = _dot(d.astype(bf16), pw_ref[...]) * ps_ref[...]
    vhalo_ref[...] = v[tt - POOL_HALO:]

    z = ub[:, GROUP_WIDTH:2 * GROUP_WIDTH] * ub[:, 2 * GROUP_WIDTH:]
    ze = jnp.concatenate([zhalo_ref[...], z], axis=0)
    z1 = pltpu.roll(ze, 1, 0)[CONV_HALO:]
    z2 = pltpu.roll(ze, 2, 0)[CONV_HALO:]
    conv = cw_ref[0:1, :] * z2 + cw_ref[1:2, :] * z1 + cw_ref[2:3, :] * z
    o_b = ub[:, :GROUP_WIDTH] * conv
    zhalo_ref[...] = z[tt - CONV_HALO:]

    ybd_ref[0, :, :GROUP_WIDTH] = _rms(o_b, gbd_ref[:, :GROUP_WIDTH]).astype(bf16)
    ybd_ref[0, :, GROUP_WIDTH:] = _rms(o_d, gbd_ref[:, GROUP_WIDTH:]).astype(bf16)


def _proj(x3, gain, wt, w, layer, seg, gq, gk, conv_w, pool_w, pool_scale, gbd, tt=PROJ_TILE):
    b, t, d = x3.shape
    tok = lambda w_: pl.BlockSpec((1, tt, w_), lambda i, j: (i, j, 0))
    feat = lambda r: pl.BlockSpec((1, r, tt), lambda i, j: (i, 0, j))
    stacked = [gain, wt, w]
    per_layer = [gq, gk, conv_w, pool_w, pool_scale, gbd]
    return pl.pallas_call(
        _proj_kernel,
        grid=(b, t // tt),
        in_specs=([tok(d)] + [_layer_spec(c, layer) for c in stacked] + [_const_spec(seg.shape)]
                  + [_layer_spec(c, layer) for c in per_layer]),
        out_specs=[feat(QT_ROWS), tok(K_COLS),
                   pl.BlockSpec((1, tt // LANES, VT_ROWS, LANES), lambda i, j: (i, j, 0, 0)),
                   tok(KC_COLS), feat(GT_ROWS), tok(2 * GROUP_WIDTH)],
        out_shape=[jax.ShapeDtypeStruct((b, QT_ROWS, t), bf16),
                   jax.ShapeDtypeStruct((b, t, K_COLS), bf16),
                   jax.ShapeDtypeStruct((b, t // LANES, VT_ROWS, LANES), bf16),
                   jax.ShapeDtypeStruct((b, t, KC_COLS), f32),
                   jax.ShapeDtypeStruct((b, GT_ROWS, t), f32),
                   jax.ShapeDtypeStruct((b, t, 2 * GROUP_WIDTH), bf16)],
        scratch_shapes=[pltpu.VMEM((CONV_HALO, GROUP_WIDTH), f32), pltpu.VMEM((POOL_HALO, D_COLS), f32)],
        compiler_params=pltpu.CompilerParams(dimension_semantics=("parallel", "arbitrary"),
                                             vmem_limit_bytes=VMEM_LIMIT),
        name="proj",
    )(x3, *stacked, seg, *per_layer)


def _compress_kernel(kvc_ref, pk_ref, w1k_ref, w2k_ref, gk_ref, pv_ref, w1v_ref, w2vt_ref,
                     kc_ref, vct_ref):
    nch = kvc_ref.shape[1] // CMP_STRIDE
    half = CMP_STRIDE * HEAD_DIM
    rows = [kvc_ref[0, pl.ds(r, nch, stride=CMP_STRIDE), :] for r in range(CMP_STRIDE)]
    k_chunks = jnp.concatenate([x[:, :HEAD_DIM] for x in rows], axis=1)
    v_chunks = jnp.concatenate([x[:, HEAD_DIM:] for x in rows], axis=1)

    def hidden(c, pos_ref, w1_ref):
        ha = _dot((c + pos_ref[0:1, :]).astype(bf16), w1_ref[0:half, :])
        hb = _dot((c + pos_ref[1:2, :]).astype(bf16), w1_ref[half:2 * half, :])
        return jax.nn.gelu(ha + pltpu.roll(hb, nch - 1, 0)).astype(bf16)

    kc_ref[0] = _rms(_dot(hidden(k_chunks, pk_ref, w1k_ref), w2k_ref[...]), gk_ref[...]).astype(bf16)
    vct_ref[0] = _dot_nt(w2vt_ref[...], hidden(v_chunks, pv_ref, w1v_ref)).astype(bf16)


def _compress(kvc, layer, pk, w1k, w2k, gk, pv, w1v, w2vt):
    b, t, width = kvc.shape
    nch = t // CMP_STRIDE
    consts = [pk, w1k, w2k, gk, pv, w1v, w2vt]
    return pl.pallas_call(
        _compress_kernel,
        grid=(b,),
        in_specs=[pl.BlockSpec((1, t, width), lambda i: (i, 0, 0))] + [_layer_spec(c, layer) for c in consts],
        out_specs=[pl.BlockSpec((1, nch, HEAD_DIM), lambda i: (i, 0, 0)),
                   pl.BlockSpec((1, HEAD_DIM, nch), lambda i: (i, 0, 0))],
        out_shape=[jax.ShapeDtypeStruct((b, nch, HEAD_DIM), bf16),
                   jax.ShapeDtypeStruct((b, HEAD_DIM, nch), bf16)],
        compiler_params=pltpu.CompilerParams(dimension_semantics=("parallel",),
                                             vmem_limit_bytes=VMEM_LIMIT),
        name="compress",
    )(kvc, *consts)


def _attn_kernel(qt_ref, kk_ref, vt_ref, kc_ref, vct_ref, gt_ref, x_ref, ybd_ref, wo_ref,
                 sink_ref, gac_ref, ovt_ref, cwin_ref, cswa_ref, ctri_ref, ccmp_ref,
                 o_ref, negsel_ref, yac_ref):
    tile = qt_ref.shape[2]
    sq = SUBQ
    subs = range(tile // sq)
    t_len = kk_ref.shape[1]
    nch = kc_ref.shape[1]
    n_sel = ovt_ref.shape[0]
    seg = SLC_SEG
    rep = SWA_HEADS // SWA_KV_HEADS
    kv_groups = range(SWA_KV_HEADS)
    tile0 = pl.program_id(1) * tile
    t0 = [tile0 + u * sq for u in subs]
    lane_head = lambda a, h: a[:, h * sq:(h + 1) * sq]

    def stack_heads(u, first, count):
        return jnp.concatenate(
            [qt_ref[0, (first + h) * HEAD_DIM:(first + h + 1) * HEAD_DIM, u * sq:(u + 1) * sq]
             for h in range(count)], axis=1)

    def values(chunk0, n_chunks, head):
        rows = slice(head * VHEAD, (head + 1) * VHEAD)
        return jnp.concatenate([vt_ref[0, chunk0 + c, rows, :] for c in range(n_chunks)], axis=1)

    def finish(acc, extra=None):
        den = acc[HEAD_DIM:]
        if extra is not None:
            den = den + extra
        return acc[:HEAD_DIM] / den

    def mask_rows(table_ref, u, s0, span, r_lo):
        return table_ref[pl.ds(pl.multiple_of(s0 - t0[u] - r_lo, sq), span), :]

    def emit(u, gi, o):
        cols = slice(gi * GROUP_WIDTH, (gi + 1) * GROUP_WIDTH)
        yac_ref[u * sq:(u + 1) * sq, cols] = _rms(o.T, gac_ref[:, cols]).astype(bf16)

    def project(y, group):
        return _dot(y, wo_ref[group * GROUP_WIDTH:(group + 1) * GROUP_WIDTH, :])

    sink = lambda g, r: sink_ref[:, g * rep + r:g * rep + r + 1]
    span_w = NSA_WINDOW + sq
    span_c = SWA_WINDOW + sq
    s0_w = [pl.multiple_of(jnp.maximum(t0[u] - NSA_WINDOW, 0), sq) for u in subs]
    s0_c = [pl.multiple_of(jnp.maximum(t0[u] - SWA_WINDOW, 0), sq) for u in subs]

    qa = [stack_heads(u, 0, NSA_HEADS) for u in subs]
    s_cmp = [_dot(kc_ref[0], qa[u]) for u in subs]
    s_w = [_dot(kk_ref[0, pl.ds(s0_w[u], span_w), HEAD_DIM:2 * HEAD_DIM], qa[u]) for u in subs]
    s_c = [[_dot(kk_ref[0, pl.ds(s0_c[u], span_c), (2 + g) * HEAD_DIM:(3 + g) * HEAD_DIM],
                 stack_heads(u, NSA_HEADS + g * rep, rep)) for g in kv_groups] for u in subs]

    z_cmp = []
    for u in subs:
        start = pl.multiple_of(nch - t0[u] // CMP_STRIDE, SUBLANES)
        bias = ccmp_ref[pl.ds(start, nch), :]
        z_cmp.append(jnp.concatenate([lane_head(s_cmp[u], h) + bias for h in range(NSA_HEADS)], axis=1))
    m_cmp = [jnp.max(z_cmp[u], axis=0, keepdims=True) for u in subs]
    z_w, m_w, z_c, m_c = [], [], [], []
    for u in subs:
        bias = mask_rows(cwin_ref, u, s0_w[u], span_w, -NSA_WINDOW)
        z_w.append(jnp.concatenate([lane_head(s_w[u], h) + bias for h in range(NSA_HEADS)], axis=1))
        m_w.append(jnp.max(z_w[u], axis=0, keepdims=True))
        bias = mask_rows(cswa_ref, u, s0_c[u], span_c, -SWA_WINDOW)
        z_c.append([[lane_head(s_c[u][g], r) + bias for r in range(rep)] for g in kv_groups])
        m_c.append([[jnp.maximum(jnp.max(z_c[u][g][r], axis=0, keepdims=True), sink(g, r))
                     for r in range(rep)] for g in kv_groups])

    o_ref[0] = (x_ref[0] + project(ybd_ref[0, :, :GROUP_WIDTH], 1)
                + project(ybd_ref[0, :, GROUP_WIDTH:], 3))

    e_cmp = [jnp.exp2(z_cmp[u] - m_cmp[u]) for u in subs]
    p_w = [jnp.exp2(z_w[u] - m_w[u]).astype(bf16) for u in subs]
    p_c = [[jnp.concatenate([jnp.exp2(z_c[u][g][r] - m_c[u][g][r]).astype(bf16) for r in range(rep)], axis=1)
            for g in kv_groups] for u in subs]

    p_cmp = [jnp.where(m_cmp[u] > 0.5 * NEG, e_cmp[u] / jnp.sum(e_cmp[u], axis=0, keepdims=True), 0.0)
             for u in subs]
    o_cmp = [_dot(vct_ref[0], p_cmp[u].astype(bf16)) for u in subs]
    imp = []
    for u in subs:
        p_sum = lane_head(p_cmp[u], 0)
        for h in range(1, NSA_HEADS):
            p_sum = p_sum + lane_head(p_cmp[u], h)
        p_hi, p_lo = _split(p_sum)
        imp.append(_dot(ovt_ref[...], p_hi) + _dot(ovt_ref[...], p_lo))
    o_win = [finish(_dot(values(s0_w[u] // LANES, span_w // LANES, 1), p_w[u])) for u in subs]
    for u in subs:
        o_c = []
        for g in kv_groups:
            acc = _dot(values(s0_c[u] // LANES, span_c // LANES, 2 + g), p_c[u][g])
            o_c += [finish(lane_head(acc, r), jnp.exp2(sink(g, r) - m_c[u][g][r])) for r in range(rep)]
        emit(u, 1, jnp.concatenate(o_c, axis=0))
    o_ref[0] += project(yac_ref[:, GROUP_WIDTH:], 2)

    j = lax.broadcasted_iota(jnp.int32, (n_sel, sq), 0)
    jf = j.astype(f32)
    for u in subs:
        t = t0[u] + lax.broadcasted_iota(jnp.int32, (n_sel, sq), 1)
        cur = jnp.right_shift(t, SLC_SHIFT)
        forced = (j == 0) | (j == cur) | (j == cur - 1)
        work = jnp.where(forced, FORCE, jnp.where(j * SLC_BLOCK > t, -1.0, imp[u]))
        chosen = jnp.zeros((n_sel, sq), f32)
        for _ in range(min(SLC_TOP, n_sel)):
            top = jnp.max(work, axis=0, keepdims=True)
            first = jnp.min(jnp.where(work == top, jf, float(n_sel)), axis=0, keepdims=True)
            hit = jf == first
            chosen = jnp.where(hit, 1.0, chosen)
            work = jnp.where(hit, TAKEN, work)
        negsel_ref[u] = jnp.where(chosen > 0.5, 0.0, NEG)

    def selected(n):
        def key_rows(k, u):
            return k * seg, ((k + 1) * seg if k < n - 1 else k * seg + (u + 1) * sq)

        def scores(k):
            return [_dot(kk_ref[0, slice(*key_rows(k, u)), 0:HEAD_DIM], qa[u]) for u in subs]

        def masked(k, s):
            z = []
            for u in subs:
                lo, hi = key_rows(k, u)
                bias = [jnp.broadcast_to(negsel_ref[u, b:b + 1, :], (SLC_BLOCK, sq))
                        for b in range(lo // SLC_BLOCK, hi // SLC_BLOCK)]
                if k == n - 1:
                    own = sq // SLC_BLOCK
                    bias = bias[:-own] + [jnp.concatenate(bias[-own:], axis=0) + ctri_ref[...]]
                bias = jnp.concatenate(bias, axis=0)
                z.append(jnp.concatenate([lane_head(s[u], h) + bias for h in range(NSA_HEADS)], axis=1))
            return z

        def weighted(k, p):
            out = []
            for u in subs:
                lo, hi = key_rows(k, u)
                out.append(_dot(values(lo // LANES, (hi - lo) // LANES, 0), p[u]))
            return out

        m = [jnp.full((1, NSA_HEADS * sq), NEG, f32) for _ in subs]
        acc = [jnp.zeros((VHEAD, NSA_HEADS * sq), f32) for _ in subs]
        s_next = scores(0)
        for k in range(n):
            s_cur = s_next
            if k + 1 < n:
                s_next = scores(k + 1)
            z = masked(k, s_cur)
            m_new = [jnp.maximum(m[u], jnp.max(z[u], axis=0, keepdims=True)) for u in subs]
            p = [jnp.exp2(z[u] - m_new[u]).astype(bf16) for u in subs]
            pv = weighted(k, p)
            acc = [jnp.exp2(m[u] - m_new[u]) * acc[u] + pv[u] for u in subs]
            m = m_new
        for u in subs:
            o_slc = finish(acc[u])
            gt = gt_ref[0, :, u * sq:(u + 1) * sq]
            o_a = []
            for h in range(NSA_HEADS):
                g = lambda k: gt[NSA_BRANCHES * h + k:NSA_BRANCHES * h + k + 1, :]
                o_a.append(g(0) * lane_head(o_cmp[u], h) + g(1) * lane_head(o_slc, h)
                           + g(2) * lane_head(o_win[u], h))
            emit(u, 0, jnp.concatenate(o_a, axis=0))
        o_ref[0] += project(yac_ref[:, :GROUP_WIDTH], 0)

    n_seg = (tile0 + tile - 1) // seg + 1
    for n in range(1, t_len // seg + 1):
        pl.when(n_seg == n)(functools.partial(selected, n))


def _attn(qt, kk, vt, kc, vct, gt, x, ybd, wo, layer, sinks, gac, ovt, tables):
    b, _, t = qt.shape
    tq = Q_TILE
    assert SLC_SEG == tq
    full = lambda a: pl.BlockSpec((1,) + a.shape[1:], lambda i, j: (i,) + (0,) * (a.ndim - 1))
    feat = lambda a: pl.BlockSpec((1, a.shape[1], tq), lambda i, j: (i, 0, j))
    tok = lambda a: pl.BlockSpec((1, tq, a.shape[2]), lambda i, j: (i, j, 0))
    return pl.pallas_call(
        _attn_kernel,
        grid=(b, t // tq),
        in_specs=[feat(qt), full(kk), full(vt), full(kc), full(vct), feat(gt), tok(x), tok(ybd),
                  _layer_spec(wo, layer),
                  _layer_spec(sinks, layer), _layer_spec(gac, layer), _const_spec(ovt.shape)]
                 + [_const_spec(c.shape) for c in tables],
        out_specs=tok(x),
        out_shape=jax.ShapeDtypeStruct(x.shape, f32),
        scratch_shapes=[pltpu.VMEM((tq // SUBQ, ovt.shape[0], SUBQ), f32),
                        pltpu.VMEM((tq, 2 * GROUP_WIDTH), bf16)],
        compiler_params=pltpu.CompilerParams(dimension_semantics=("parallel", "parallel"),
                                             vmem_limit_bytes=VMEM_LIMIT),
        name="attn",
    )(qt, kk, vt, kc, vct, gt, x, ybd, wo, sinks, gac, ovt, *tables)


def _overlap_t(t_len):
    n_c = (t_len - CMP_LEN) // CMP_STRIDE + 1
    nch = t_len // CMP_STRIDE
    ci = np.arange(nch)[None, :] * CMP_STRIDE
    sj = np.arange(t_len // SLC_BLOCK)[:, None] * SLC_BLOCK
    ov = (ci <= sj + SLC_BLOCK - 1) & (ci + CMP_LEN - 1 >= sj) & (np.arange(nch)[None, :] < n_c)
    return jnp.asarray(ov, bf16)


def _mask_tables(t_len):
    c = np.arange(SUBQ)[None, :]

    def table(r_lo, r_hi, visible):
        r = np.arange(r_lo, r_hi)[:, None]
        return jnp.asarray(np.where(visible(r, c), 0.0, NEG), f32)

    band = lambda w: (lambda r, c: (r <= c) & (r > c - w))
    nch = t_len // CMP_STRIDE
    return (table(-NSA_WINDOW, NSA_WINDOW + SUBQ, band(NSA_WINDOW)),
            table(-SWA_WINDOW, SWA_WINDOW + SUBQ, band(SWA_WINDOW)),
            table(0, SUBQ, lambda r, c: r <= c),
            table(-nch, nch, lambda n, c: n * CMP_STRIDE + CMP_LEN - 1 <= c))


def _block_diag(blocks):
    n, r, c = blocks.shape
    eye = jnp.eye(n, dtype=blocks.dtype)
    return (eye[:, None, :, None] * blocks[:, :, None, :]).reshape(n * r, n * c)


def kernel(x, ffn1_norm, ffn1_w1, ffn1_w3, ffn1_w2, mix_norm, w_in, nsa_q_norm, nsa_kc_norm, nsa_ks_norm, nsa_kw_norm, cmp_pos_k, cmp_w1_k, cmp_w2_k, cmp_pos_v, cmp_w1_v, cmp_w2_v, conv_w, swa_q_norm, swa_k_norm, swa_sinks, pool_w, pool_scale, group_norm, w_out, ffn2_norm, ffn2_w1, ffn2_w3, ffn2_w2):
    b, t, d = x.shape
    depth = w_in.shape[0]
    assert t % max(FFN_TILE, PROJ_TILE, Q_TILE, SLC_SEG) == 0 and t >= NSA_WINDOW + SUBQ
    assert d == N_MIX * GROUP_WIDTH and w_in.shape[2] == sum(SPLIT_SIZES)
    ovt = _overlap_t(t)
    tables = _mask_tables(t)
    seg = jnp.asarray(np.kron(np.eye(K_COLS // HEAD_DIM), np.ones((HEAD_DIM, HEAD_DIM))), bf16)
    cb = lambda v: v.astype(bf16)

    up_rows, down_rows = 512, ffn1_w2.shape[1] // 2
    ffn1 = (_to_bf16(ffn1_w1, up_rows), _to_bf16(ffn1_w3, up_rows), _to_bf16(ffn1_w2, down_rows))
    ffn2 = (_to_bf16(ffn2_w1, up_rows), _to_bf16(ffn2_w3, up_rows), _to_bf16(ffn2_w2, down_rows))
    wo = _to_bf16(w_out, up_rows)
    w_tok, w_feat = _arrange_w_in(w_in)

    vec = lambda v: v[:, None, :].astype(f32)
    groups = lambda v, ids: jnp.concatenate([v[:, i * GROUP_WIDTH:(i + 1) * GROUP_WIDTH] for i in ids], axis=1)
    q_scale = ATTN_SCALE * LOG2E
    gq = jnp.concatenate([jnp.tile(nsa_q_norm * q_scale, (1, NSA_HEADS)),
                          jnp.tile(swa_q_norm * q_scale, (1, SWA_HEADS))], axis=1)[:, :, None]
    gk = vec(jnp.concatenate([nsa_ks_norm, nsa_kw_norm, jnp.tile(swa_k_norm, (1, SWA_KV_HEADS))], axis=1))
    pool_bd = jax.vmap(_block_diag)(pool_w).astype(bf16)
    pos = lambda p: p.reshape(depth, CMP_LEN // CMP_STRIDE, CMP_STRIDE * HEAD_DIM)
    cmp_consts = (pos(cmp_pos_k), cb(cmp_w1_k), cb(cmp_w2_k), vec(nsa_kc_norm),
                  pos(cmp_pos_v), cb(cmp_w1_v), cb(jnp.swapaxes(cmp_w2_v, 1, 2)))
    n1, nm, n2 = vec(ffn1_norm), vec(mix_norm), vec(ffn2_norm)
    ps, gbd, gac = vec(pool_scale), vec(groups(group_norm, (1, 3))), vec(groups(group_norm, (0, 2)))
    sinks = vec(swa_sinks * LOG2E)

    for l in range(depth):
        x = _ffn(x, n1, *ffn1, layer=l)
        qt, kk, vt, kvc, gt, ybd = _proj(x, nm, w_feat, w_tok, l, seg, gq, gk, conv_w, pool_bd, ps, gbd)
        kc, vct = _compress(kvc, l, *cmp_consts)
        x = _attn(qt, kk, vt, kc, vct, gt, x, ybd, wo, l, sinks, gac, ovt, tables)
        x = _ffn(x, n2, *ffn2, layer=l)
    return x
```

```python
import functools

import numpy as np
import jax
import jax.numpy as jnp
from jax import lax
from jax.experimental import pallas as pl
from jax.experimental.pallas import tpu as pltpu

HEAD_DIM = 64
GROUP_WIDTH = 256
N_MIX = 4
NSA_HEADS = 4
NSA_BRANCHES = 3
CMP_LEN = 32
CMP_STRIDE = 16
SLC_BLOCK = 64
SLC_TOP = 8
NSA_WINDOW = 512
SWA_HEADS = 4
SWA_KV_HEADS = 2
SWA_WINDOW = 128
POOL_WINDOWS = (2, 4, 8, 16)
FFN_TILE = 512
PROJ_TILE = 1024
Q_TILE = 512
SUBQ = 128
EPS = 1e-6
NEG = -1e30
FORCE = 1e4
TAKEN = -2.0
ATTN_SCALE = HEAD_DIM ** -0.5
LOG2E = float(np.log2(np.e))
SLC_SEG = 512
SLC_SHIFT = SLC_BLOCK.bit_length() - 1
POOL_SHIFT = (GROUP_WIDTH // len(POOL_WINDOWS)).bit_length() - 1

LANES = 128
SUBLANES = 8
CONV_HALO = 8
POOL_HALO = 16
VMEM_LIMIT = 56 * 2 ** 20

QT_ROWS = 512
V_HEADS = 4
VHEAD = 2 * HEAD_DIM
VT_ROWS = V_HEADS * VHEAD
GT_ROWS = 2 * SUBLANES
T_ROWS = QT_ROWS + V_HEADS * HEAD_DIM + GT_ROWS
K_COLS = 256
KC_COLS = 128
B_COLS = 768
D_COLS = 256
OFF_K = 0
OFF_KC = OFF_K + K_COLS
OFF_B = OFF_KC + KC_COLS
OFF_D = OFF_B + B_COLS
W_COLS = OFF_D + D_COLS

SPLIT_SIZES = (256, 64, 64, 64, 64, 64, 64, 12, 256, 256, 256, 256, 128, 128, 256)

f32 = jnp.float32
bf16 = jnp.bfloat16


def _rms(x, g):
    return x * lax.rsqrt(jnp.mean(x * x, axis=-1, keepdims=True) + EPS) * g


def _dot(a, b):
    return jnp.dot(a, b, preferred_element_type=f32)


def _dot_nt(a, b):
    return lax.dot_general(a, b, (((1,), (1,)), ((), ())), preferred_element_type=f32)


def _split(a):
    hi = a.astype(bf16)
    return hi, (a - hi.astype(f32)).astype(bf16)


def _const_spec(shape):
    nd = len(shape)
    return pl.BlockSpec(shape, lambda *_: (0,) * nd, pipeline_mode=pl.Buffered(1))


def _layer_spec(stacked, layer):
    return pl.BlockSpec((None,) + stacked.shape[1:], lambda *_: (layer, 0, 0), pipeline_mode=pl.Buffered(1))


def _cast_kernel(w_ref, o_ref):
    o_ref[...] = w_ref[...].astype(bf16)


def _to_bf16(w, rows):
    depth, r, c = w.shape
    spec = pl.BlockSpec((1, rows, c), lambda l, i: (l, i, 0))
    return pl.pallas_call(
        _cast_kernel,
        grid=(depth, r // rows),
        in_specs=[spec],
        out_specs=spec,
        out_shape=jax.ShapeDtypeStruct(w.shape, bf16),
        compiler_params=pltpu.CompilerParams(dimension_semantics=("parallel", "parallel"),
                                             vmem_limit_bytes=VMEM_LIMIT),
        name="cast",
    )(w)


def _arrange_kernel(w_ref, tok_ref, feat_ref):
    w = w_ref[0]
    offs = np.concatenate([[0], np.cumsum(SPLIT_SIZES)])
    (a_q, a_kc, a_vc, a_ks, a_vs, a_kw, a_vw, a_g,
     b_b, b_c, b_x, c_q, c_k, c_v, d_v) = [w[:, offs[k]:offs[k + 1]] for k in range(len(SPLIT_SIZES))]
    tok_ref[0] = jnp.concatenate([a_ks, a_kw, c_k, a_kc, a_vc, b_b, b_c, b_x, d_v], axis=1).astype(bf16)
    feat = [a_q, c_q, a_vs, a_vw, c_v, a_g]
    width = sum(p.shape[1] for p in feat)
    padded = -(-width // LANES) * LANES
    feat = jnp.concatenate(feat + [jnp.zeros((w.shape[0], padded - width), f32)], axis=1)
    feat_ref[0] = feat.T[:T_ROWS].astype(bf16)


def _arrange_w_in(w_in):
    depth, d, width = w_in.shape
    return pl.pallas_call(
        _arrange_kernel,
        grid=(depth,),
        in_specs=[pl.BlockSpec((1, d, width), lambda l: (l, 0, 0))],
        out_specs=[pl.BlockSpec((1, d, W_COLS), lambda l: (l, 0, 0)),
                   pl.BlockSpec((1, T_ROWS, d), lambda l: (l, 0, 0))],
        out_shape=[jax.ShapeDtypeStruct((depth, d, W_COLS), bf16),
                   jax.ShapeDtypeStruct((depth, T_ROWS, d), bf16)],
        compiler_params=pltpu.CompilerParams(dimension_semantics=("parallel",),
                                             vmem_limit_bytes=VMEM_LIMIT),
        name="arrange_w_in",
    )(w_in)


def _ffn_kernel(*refs, with_mix):
    if with_mix:
        x_ref, ya_ref, yb_ref, wo_ref, g_ref, w1_ref, w3_ref, w2_ref, o_ref = refs
        x = x_ref[0]
        for gi, y_ref in ((0, ya_ref), (1, yb_ref)):
            for half in range(2):
                rows = slice((gi + 2 * half) * GROUP_WIDTH, (gi + 2 * half + 1) * GROUP_WIDTH)
                x = x + _dot(y_ref[0, :, half * GROUP_WIDTH:(half + 1) * GROUP_WIDTH], wo_ref[rows, :])
    else:
        x_ref, g_ref, w1_ref, w3_ref, w2_ref, o_ref = refs
        x = x_ref[0]
    halves = [slice(0, x.shape[0] // 2), slice(x.shape[0] // 2, x.shape[0])]
    hs = [_rms(x[r], g_ref[...]).astype(bf16) for r in halves]
    a, b = [], []
    for h in hs:
        a.append(_dot(h, w1_ref[...]))
        b.append(_dot(h, w3_ref[...]))
    act = [(a[i] * jax.nn.sigmoid(a[i]) * b[i]).astype(bf16) for i in range(len(halves))]
    for i, r in enumerate(halves):
        o_ref[0, r, :] = x[r] + 0.5 * _dot(act[i], w2_ref[...])


def _ffn(x, gain, w1, w3, w2, layer, mix=None, tm=FFN_TILE):
    b, t, d = x.shape
    row = lambda w: pl.BlockSpec((1, tm, w), lambda i, j: (i, j, 0))
    args, specs = [x], [row(d)]
    if mix is not None:
        ya, yb, wo = mix
        args += [ya, yb, wo]
        specs += [row(ya.shape[2]), row(yb.shape[2]), _layer_spec(wo, layer)]
    args += [gain, w1, w3, w2]
    specs += [_layer_spec(w, layer) for w in (gain, w1, w3, w2)]
    return pl.pallas_call(
        functools.partial(_ffn_kernel, with_mix=mix is not None),
        grid=(b, t // tm),
        in_specs=specs,
        out_specs=row(d),
        out_shape=jax.ShapeDtypeStruct((b, t, d), f32),
        compiler_params=pltpu.CompilerParams(dimension_semantics=("parallel", "parallel"),
                                             vmem_limit_bytes=VMEM_LIMIT),
        name="ffn" if mix is None else "ffn_mix",
    )(*args)


def _proj_kernel(x_ref, g_ref, wt_ref, w_ref, seg_ref, gq_ref, gk_ref, cw_ref, pw_ref, ps_ref, gbd_ref,
                 qt_ref, kk_ref, vt_ref, kc_ref, gt_ref, ybd_ref,
                 zhalo_ref, vhalo_ref):
    tt = x_ref.shape[1]
    ti = pl.program_id(1)

    @pl.when(ti == 0)
    def _():
        zhalo_ref[...] = jnp.zeros_like(zhalo_ref)
        vhalo_ref[...] = jnp.zeros_like(vhalo_ref)

    halves = [slice(0, tt // 2), slice(tt // 2, tt)]
    hs = [_rms(x_ref[0, r, :], g_ref[...]).astype(bf16) for r in halves]
    ut = jnp.concatenate([_dot_nt(wt_ref[...], h) for h in hs], axis=1)
    cols = lambda off, n: jnp.concatenate([_dot(h, w_ref[:, off:off + n]) for h in hs], axis=0)
    u_k = cols(OFF_K, K_COLS)
    ub = cols(OFF_B, B_COLS)
    v = cols(OFF_D, D_COLS)
    kc_ref[0] = cols(OFF_KC, KC_COLS)

    for hd in range(QT_ROWS // HEAD_DIM):
        rows = slice(hd * HEAD_DIM, (hd + 1) * HEAD_DIM)
        u = ut[rows]
        ms = jnp.mean(u * u, axis=0, keepdims=True)
        qt_ref[0, rows, :] = (u * lax.rsqrt(ms + EPS) * gq_ref[rows, :]).astype(bf16)
    ones = jnp.ones((HEAD_DIM, LANES), bf16)
    for hv in range(V_HEADS):
        vh = ut[QT_ROWS + hv * HEAD_DIM:QT_ROWS + (hv + 1) * HEAD_DIM].astype(bf16)
        for c in range(tt // LANES):
            vt_ref[0, c, hv * VHEAD:hv * VHEAD + HEAD_DIM, :] = vh[:, c * LANES:(c + 1) * LANES]
            vt_ref[0, c, hv * VHEAD + HEAD_DIM:(hv + 1) * VHEAD, :] = ones
    gt_ref[0] = jax.nn.sigmoid(ut[QT_ROWS + V_HEADS * HEAD_DIM:])

    hi, lo = _split(u_k * u_k)
    ss = _dot(hi, seg_ref[...]) + _dot(lo, seg_ref[...])
    kk_ref[0] = (u_k * lax.rsqrt(ss * (1.0 / HEAD_DIM) + EPS) * gk_ref[...]).astype(bf16)

    s = jnp.concatenate([vhalo_ref[...], v], axis=0)
    grp = jnp.right_shift(lax.broadcasted_iota(jnp.int32, (1, D_COLS), 1), POOL_SHIFT)
    acc = None
    for k, w in enumerate(POOL_WINDOWS):
        s = s + pltpu.roll(s, w // 2, 0)
        acc = s if acc is None else jnp.where(grp >= k, s, acc)
    win = jnp.where(grp == 0, POOL_WINDOWS[0],
                    jnp.where(grp == 1, POOL_WINDOWS[1],
                              jnp.where(grp == 2, POOL_WINDOWS[2], POOL_WINDOWS[3])))
    t = ti * tt + lax.broadcasted_iota(jnp.int32, (tt, 1), 0)
    cnt = jnp.minimum(t + 1, win).astype(f32)
    d = acc[POOL_HALO:] / cnt - v
    o_d = _dot(d.astype(bf16), pw_ref[...]) * ps_ref[...]
    vhalo_ref[...] = v[tt - POOL_HALO:]

    z = ub[:, GROUP_WIDTH:2 * GROUP_WIDTH] * ub[:, 2 * GROUP_WIDTH:]
    ze = jnp.concatenate([zhalo_ref[...], z], axis=0)
    z1 = pltpu.roll(ze, 1, 0)[CONV_HALO:]
    z2 = pltpu.roll(ze, 2, 0)[CONV_HALO:]
    conv = cw_ref[0:1, :] * z2 + cw_ref[1:2, :] * z1 + cw_ref[2:3, :] * z
    o_b = ub[:, :GROUP_WIDTH] * conv
    zhalo_ref[...] = z[tt - CONV_HALO:]

    ybd_ref[0, :, :GROUP_WIDTH] = _rms(o_b, gbd_ref[:, :GROUP_WIDTH]).astype(bf16)
    ybd_ref[0, :, GROUP_WIDTH:] = _rms(o_d, gbd_ref[:, GROUP_WIDTH:]).astype(bf16)


def _proj(x3, gain, wt, w, layer, seg, gq, gk, conv_w, pool_w, pool_scale, gbd, tt=PROJ_TILE):
    b, t, d = x3.shape
    tok = lambda w_: pl.BlockSpec((1, tt, w_), lambda i, j: (i, j, 0))
    feat = lambda r: pl.BlockSpec((1, r, tt), lambda i, j: (i, 0, j))
    stacked = [gain, wt, w]
    per_layer = [gq, gk, conv_w, pool_w, pool_scale, gbd]
    return pl.pallas_call(
        _proj_kernel,
        grid=(b, t // tt),
        in_specs=([tok(d)] + [_layer_spec(c, layer) for c in stacked] + [_const_spec(seg.shape)]
                  + [_layer_spec(c, layer) for c in per_layer]),
        out_specs=[feat(QT_ROWS), tok(K_COLS),
                   pl.BlockSpec((1, tt // LANES, VT_ROWS, LANES), lambda i, j: (i, j, 0, 0)),
                   tok(KC_COLS), feat(GT_ROWS), tok(2 * GROUP_WIDTH)],
        out_shape=[jax.ShapeDtypeStruct((b, QT_ROWS, t), bf16),
                   jax.ShapeDtypeStruct((b, t, K_COLS), bf16),
                   jax.ShapeDtypeStruct((b, t // LANES, VT_ROWS, LANES), bf16),
                   jax.ShapeDtypeStruct((b, t, KC_COLS), f32),
                   jax.ShapeDtypeStruct((b, GT_ROWS, t), f32),
                   jax.ShapeDtypeStruct((b, t, 2 * GROUP_WIDTH), bf16)],
        scratch_shapes=[pltpu.VMEM((CONV_HALO, GROUP_WIDTH), f32), pltpu.VMEM((POOL_HALO, D_COLS), f32)],
        compiler_params=pltpu.CompilerParams(dimension_semantics=("parallel", "arbitrary"),
                                             vmem_limit_bytes=VMEM_LIMIT),
        name="proj",
    )(x3, *stacked, seg, *per_layer)


def _compress_kernel(kvc_ref, pk_ref, w1k_ref, w2k_ref, gk_ref, pv_ref, w1v_ref, w2vt_ref,
                     kc_ref, vct_ref):
    nch = kvc_ref.shape[1] // CMP_STRIDE
    half = CMP_STRIDE * HEAD_DIM
    rows = [kvc_ref[0, pl.ds(r, nch, stride=CMP_STRIDE), :] for r in range(CMP_STRIDE)]
    k_chunks = jnp.concatenate([x[:, :HEAD_DIM] for x in rows], axis=1)
    v_chunks = jnp.concatenate([x[:, HEAD_DIM:] for x in rows], axis=1)

    def hidden(c, pos_ref, w1_ref):
        ha = _dot((c + pos_ref[0:1, :]).astype(bf16), w1_ref[0:half, :])
        hb = _dot((c + pos_ref[1:2, :]).astype(bf16), w1_ref[half:2 * half, :])
        return jax.nn.gelu(ha + pltpu.roll(hb, nch - 1, 0)).astype(bf16)

    kc_ref[0] = _rms(_dot(hidden(k_chunks, pk_ref, w1k_ref), w2k_ref[...]), gk_ref[...]).astype(bf16)
    vct_ref[0] = _dot_nt(w2vt_ref[...], hidden(v_chunks, pv_ref, w1v_ref)).astype(bf16)


def _compress(kvc, layer, pk, w1k, w2k, gk, pv, w1v, w2vt):
    b, t, width = kvc.shape
    nch = t // CMP_STRIDE
    consts = [pk, w1k, w2k, gk, pv, w1v, w2vt]
    return pl.pallas_call(
        _compress_kernel,
        grid=(b,),
        in_specs=[pl.BlockSpec((1, t, width), lambda i: (i, 0, 0))] + [_layer_spec(c, layer) for c in consts],
        out_specs=[pl.BlockSpec((1, nch, HEAD_DIM), lambda i: (i, 0, 0)),
                   pl.BlockSpec((1, HEAD_DIM, nch), lambda i: (i, 0, 0))],
        out_shape=[jax.ShapeDtypeStruct((b, nch, HEAD_DIM), bf16),
                   jax.ShapeDtypeStruct((b, HEAD_DIM, nch), bf16)],
        compiler_params=pltpu.CompilerParams(dimension_semantics=("parallel",),
                                             vmem_limit_bytes=VMEM_LIMIT),
        name="compress",
    )(kvc, *consts)


def _attn_kernel(qt_ref, kk_ref, vt_ref, kc_ref, vct_ref, gt_ref, sink_ref, gac_ref, ovt_ref,
                 cwin_ref, cswa_ref, ctri_ref, ccmp_ref, y_ref, negsel_ref):
    tile = qt_ref.shape[2]
    sq = SUBQ
    subs = range(tile // sq)
    t_len = kk_ref.shape[1]
    nch = kc_ref.shape[1]
    n_sel = ovt_ref.shape[0]
    seg = SLC_SEG
    rep = SWA_HEADS // SWA_KV_HEADS
    kv_groups = range(SWA_KV_HEADS)
    tile0 = pl.program_id(1) * tile
    t0 = [tile0 + u * sq for u in subs]
    lane_head = lambda a, h: a[:, h * sq:(h + 1) * sq]

    def stack_heads(u, first, count):
        return jnp.concatenate(
            [qt_ref[0, (first + h) * HEAD_DIM:(first + h + 1) * HEAD_DIM, u * sq:(u + 1) * sq]
             for h in range(count)], axis=1)

    def values(chunk0, n_chunks, head):
        rows = slice(head * VHEAD, (head + 1) * VHEAD)
        return jnp.concatenate([vt_ref[0, chunk0 + c, rows, :] for c in range(n_chunks)], axis=1)

    def finish(acc, extra=None):
        den = acc[HEAD_DIM:]
        if extra is not None:
            den = den + extra
        return acc[:HEAD_DIM] / den

    def mask_rows(table_ref, u, s0, span, r_lo):
        return table_ref[pl.ds(pl.multiple_of(s0 - t0[u] - r_lo, sq), span), :]

    def emit(u, gi, o):
        cols = slice(gi * GROUP_WIDTH, (gi + 1) * GROUP_WIDTH)
        y_ref[0, u * sq:(u + 1) * sq, cols] = _rms(o.T, gac_ref[:, cols]).astype(bf16)

    sink = lambda g, r: sink_ref[:, g * rep + r:g * rep + r + 1]
    span_w = NSA_WINDOW + sq
    span_c = SWA_WINDOW + sq
    s0_w = [pl.multiple_of(jnp.maximum(t0[u] - NSA_WINDOW, 0), sq) for u in subs]
    s0_c = [pl.multiple_of(jnp.maximum(t0[u] - SWA_WINDOW, 0), sq) for u in subs]

    qa = [stack_heads(u, 0, NSA_HEADS) for u in subs]
    s_cmp = [_dot(kc_ref[0], qa[u]) for u in subs]
    s_w = [_dot(kk_ref[0, pl.ds(s0_w[u], span_w), HEAD_DIM:2 * HEAD_DIM], qa[u]) for u in subs]
    s_c = [[_dot(kk_ref[0, pl.ds(s0_c[u], span_c), (2 + g) * HEAD_DIM:(3 + g) * HEAD_DIM],
                 stack_heads(u, NSA_HEADS + g * rep, rep)) for g in kv_groups] for u in subs]

    z_cmp = []
    for u in subs:
        start = pl.multiple_of(nch - t0[u] // CMP_STRIDE, SUBLANES)
        bias = ccmp_ref[pl.ds(start, nch), :]
        z_cmp.append(jnp.concatenate([lane_head(s_cmp[u], h) + bias for h in range(NSA_HEADS)], axis=1))
    m_cmp = [jnp.max(z_cmp[u], axis=0, keepdims=True) for u in subs]
    z_w, m_w, z_c, m_c = [], [], [], []
    for u in subs:
        bias = mask_rows(cwin_ref, u, s0_w[u], span_w, -NSA_WINDOW)
        z_w.append(jnp.concatenate([lane_head(s_w[u], h) + bias for h in range(NSA_HEADS)], axis=1))
        m_w.append(jnp.max(z_w[u], axis=0, keepdims=True))
        bias = mask_rows(cswa_ref, u, s0_c[u], span_c, -SWA_WINDOW)
        z_c.append([[lane_head(s_c[u][g], r) + bias for r in range(rep)] for g in kv_groups])
        m_c.append([[jnp.maximum(jnp.max(z_c[u][g][r], axis=0, keepdims=True), sink(g, r))
                     for r in range(rep)] for g in kv_groups])

    e_cmp = [jnp.exp2(z_cmp[u] - m_cmp[u]) for u in subs]
    p_w = [jnp.exp2(z_w[u] - m_w[u]).astype(bf16) for u in subs]
    p_c = [[jnp.concatenate([jnp.exp2(z_c[u][g][r] - m_c[u][g][r]).astype(bf16) for r in range(rep)], axis=1)
            for g in kv_groups] for u in subs]

    p_cmp = [jnp.where(m_cmp[u] > 0.5 * NEG, e_cmp[u] / jnp.sum(e_cmp[u], axis=0, keepdims=True), 0.0)
             for u in subs]
    o_cmp = [_dot(vct_ref[0], p_cmp[u].astype(bf16)) for u in subs]
    imp = []
    for u in subs:
        p_sum = lane_head(p_cmp[u], 0)
        for h in range(1, NSA_HEADS):
            p_sum = p_sum + lane_head(p_cmp[u], h)
        p_hi, p_lo = _split(p_sum)
        imp.append(_dot(ovt_ref[...], p_hi) + _dot(ovt_ref[...], p_lo))
    o_win = [finish(_dot(values(s0_w[u] // LANES, span_w // LANES, 1), p_w[u])) for u in subs]
    for u in subs:
        o_c = []
        for g in kv_groups:
            acc = _dot(values(s0_c[u] // LANES, span_c // LANES, 2 + g), p_c[u][g])
            o_c += [finish(lane_head(acc, r), jnp.exp2(sink(g, r) - m_c[u][g][r])) for r in range(rep)]
        emit(u, 1, jnp.concatenate(o_c, axis=0))

    j = lax.broadcasted_iota(jnp.int32, (n_sel, sq), 0)
    jf = j.astype(f32)
    for u in subs:
        t = t0[u] + lax.broadcasted_iota(jnp.int32, (n_sel, sq), 1)
        cur = jnp.right_shift(t, SLC_SHIFT)
        forced = (j == 0) | (j == cur) | (j == cur - 1)
        work = jnp.where(forced, FORCE, jnp.where(j * SLC_BLOCK > t, -1.0, imp[u]))
        chosen = jnp.zeros((n_sel, sq), f32)
        for _ in range(min(SLC_TOP, n_sel)):
            top = jnp.max(work, axis=0, keepdims=True)
            first = jnp.min(jnp.where(work == top, jf, float(n_sel)), axis=0, keepdims=True)
            hit = jf == first
            chosen = jnp.where(hit, 1.0, chosen)
            work = jnp.where(hit, TAKEN, work)
        negsel_ref[u] = jnp.where(chosen > 0.5, 0.0, NEG)

    def selected(n):
        def key_rows(k, u):
            return k * seg, ((k + 1) * seg if k < n - 1 else k * seg + (u + 1) * sq)

        def scores(k):
            return [_dot(kk_ref[0, slice(*key_rows(k, u)), 0:HEAD_DIM], qa[u]) for u in subs]

        def masked(k, s):
            z = []
            for u in subs:
                lo, hi = key_rows(k, u)
                bias = [jnp.broadcast_to(negsel_ref[u, b:b + 1, :], (SLC_BLOCK, sq))
                        for b in range(lo // SLC_BLOCK, hi // SLC_BLOCK)]
                if k == n - 1:
                    own = sq // SLC_BLOCK
                    bias = bias[:-own] + [jnp.concatenate(bias[-own:], axis=0) + ctri_ref[...]]
                bias = jnp.concatenate(bias, axis=0)
                z.append(jnp.concatenate([lane_head(s[u], h) + bias for h in range(NSA_HEADS)], axis=1))
            return z

        def weighted(k, p):
            out = []
            for u in subs:
                lo, hi = key_rows(k, u)
                out.append(_dot(values(lo // LANES, (hi - lo) // LANES, 0), p[u]))
            return out

        m = [jnp.full((1, NSA_HEADS * sq), NEG, f32) for _ in subs]
        acc = [jnp.zeros((VHEAD, NSA_HEADS * sq), f32) for _ in subs]
        s_next = scores(0)
        for k in range(n):
            s_cur = s_next
            if k + 1 < n:
                s_next = scores(k + 1)
            z = masked(k, s_cur)
            m_new = [jnp.maximum(m[u], jnp.max(z[u], axis=0, keepdims=True)) for u in subs]
            p = [jnp.exp2(z[u] - m_new[u]).astype(bf16) for u in subs]
            pv = weighted(k, p)
            acc = [jnp.exp2(m[u] - m_new[u]) * acc[u] + pv[u] for u in subs]
            m = m_new
        for u in subs:
            o_slc = finish(acc[u])
            gt = gt_ref[0, :, u * sq:(u + 1) * sq]
            o_a = []
            for h in range(NSA_HEADS):
                g = lambda k: gt[NSA_BRANCHES * h + k:NSA_BRANCHES * h + k + 1, :]
                o_a.append(g(0) * lane_head(o_cmp[u], h) + g(1) * lane_head(o_slc, h)
                           + g(2) * lane_head(o_win[u], h))
            emit(u, 0, jnp.concatenate(o_a, axis=0))

    n_seg = (tile0 + tile - 1) // seg + 1
    for n in range(1, t_len // seg + 1):
        pl.when(n_seg == n)(functools.partial(selected, n))


def _attn(qt, kk, vt, kc, vct, gt, layer, sinks, gac, ovt, tables):
    b, _, t = qt.shape
    tq = Q_TILE
    assert SLC_SEG == tq
    full = lambda a: pl.BlockSpec((1,) + a.shape[1:], lambda i, j: (i,) + (0,) * (a.ndim - 1))
    feat = lambda a: pl.BlockSpec((1, a.shape[1], tq), lambda i, j: (i, 0, j))
    width = 2 * GROUP_WIDTH
    return pl.pallas_call(
        _attn_kernel,
        grid=(b, t // tq),
        in_specs=[feat(qt), full(kk), full(vt), full(kc), full(vct), feat(gt),
                  _layer_spec(sinks, layer), _layer_spec(gac, layer), _const_spec(ovt.shape)]
                 + [_const_spec(c.shape) for c in tables],
        out_specs=pl.BlockSpec((1, tq, width), lambda i, j: (i, j, 0)),
        out_shape=jax.ShapeDtypeStruct((b, t, width), bf16),
        scratch_shapes=[pltpu.VMEM((tq // SUBQ, ovt.shape[0], SUBQ), f32)],
        compiler_params=pltpu.CompilerParams(dimension_semantics=("parallel", "parallel"),
                                             vmem_limit_bytes=VMEM_LIMIT),
        name="attn",
    )(qt, kk, vt, kc, vct, gt, sinks, gac, ovt, *tables)


def _overlap_t(t_len):
    n_c = (t_len - CMP_LEN) // CMP_STRIDE + 1
    nch = t_len // CMP_STRIDE
    ci = np.arange(nch)[None, :] * CMP_STRIDE
    sj = np.arange(t_len // SLC_BLOCK)[:, None] * SLC_BLOCK
    ov = (ci <= sj + SLC_BLOCK - 1) & (ci + CMP_LEN - 1 >= sj) & (np.arange(nch)[None, :] < n_c)
    return jnp.asarray(ov, bf16)


def _mask_tables(t_len):
    c = np.arange(SUBQ)[None, :]

    def table(r_lo, r_hi, visible):
        r = np.arange(r_lo, r_hi)[:, None]
        return jnp.asarray(np.where(visible(r, c), 0.0, NEG), f32)

    band = lambda w: (lambda r, c: (r <= c) & (r > c - w))
    nch = t_len // CMP_STRIDE
    return (table(-NSA_WINDOW, NSA_WINDOW + SUBQ, band(NSA_WINDOW)),
            table(-SWA_WINDOW, SWA_WINDOW + SUBQ, band(SWA_WINDOW)),
            table(0, SUBQ, lambda r, c: r <= c),
            table(-nch, nch, lambda n, c: n * CMP_STRIDE + CMP_LEN - 1 <= c))


def _block_diag(blocks):
    n, r, c = blocks.shape
    eye = jnp.eye(n, dtype=blocks.dtype)
    return (eye[:, None, :, None] * blocks[:, :, None, :]).reshape(n * r, n * c)


def kernel(x, ffn1_norm, ffn1_w1, ffn1_w3, ffn1_w2, mix_norm, w_in, nsa_q_norm, nsa_kc_norm, nsa_ks_norm, nsa_kw_norm, cmp_pos_k, cmp_w1_k, cmp_w2_k, cmp_pos_v, cmp_w1_v, cmp_w2_v, conv_w, swa_q_norm, swa_k_norm, swa_sinks, pool_w, pool_scale, group_norm, w_out, ffn2_norm, ffn2_w1, ffn2_w3, ffn2_w2):
    b, t, d = x.shape
    depth = w_in.shape[0]
    assert t % max(FFN_TILE, PROJ_TILE, Q_TILE, SLC_SEG) == 0 and t >= NSA_WINDOW + SUBQ
    assert d == N_MIX * GROUP_WIDTH and w_in.shape[2] == sum(SPLIT_SIZES)
    ovt = _overlap_t(t)
    tables = _mask_tables(t)
    seg = jnp.asarray(np.kron(np.eye(K_COLS // HEAD_DIM), np.ones((HEAD_DIM, HEAD_DIM))), bf16)
    cb = lambda v: v.astype(bf16)

    up_rows, down_rows = 512, ffn1_w2.shape[1] // 2
    ffn1 = (_to_bf16(ffn1_w1, up_rows), _to_bf16(ffn1_w3, up_rows), _to_bf16(ffn1_w2, down_rows))
    ffn2 = (_to_bf16(ffn2_w1, up_rows), _to_bf16(ffn2_w3, up_rows), _to_bf16(ffn2_w2, down_rows))
    wo = _to_bf16(w_out, up_rows)
    w_tok, w_feat = _arrange_w_in(w_in)

    vec = lambda v: v[:, None, :].astype(f32)
    groups = lambda v, ids: jnp.concatenate([v[:, i * GROUP_WIDTH:(i + 1) * GROUP_WIDTH] for i in ids], axis=1)
    q_scale = ATTN_SCALE * LOG2E
    gq = jnp.concatenate([jnp.tile(nsa_q_norm * q_scale, (1, NSA_HEADS)),
                          jnp.tile(swa_q_norm * q_scale, (1, SWA_HEADS))], axis=1)[:, :, None]
    gk = vec(jnp.concatenate([nsa_ks_norm, nsa_kw_norm, jnp.tile(swa_k_norm, (1, SWA_KV_HEADS))], axis=1))
    pool_bd = jax.vmap(_block_diag)(pool_w).astype(bf16)
    pos = lambda p: p.reshape(depth, CMP_LEN // CMP_STRIDE, CMP_STRIDE * HEAD_DIM)
    cmp_consts = (pos(cmp_pos_k), cb(cmp_w1_k), cb(cmp_w2_k), vec(nsa_kc_norm),
                  pos(cmp_pos_v), cb(cmp_w1_v), cb(jnp.swapaxes(cmp_w2_v, 1, 2)))
    n1, nm, n2 = vec(ffn1_norm), vec(mix_norm), vec(ffn2_norm)
    ps, gbd, gac = vec(pool_scale), vec(groups(group_norm, (1, 3))), vec(groups(group_norm, (0, 2)))
    sinks = vec(swa_sinks * LOG2E)

    for l in range(depth):
        x = _ffn(x, n1, *ffn1, layer=l)
        qt, kk, vt, kvc, gt, ybd = _proj(x, nm, w_feat, w_tok, l, seg, gq, gk, conv_w, pool_bd, ps, gbd)
        kc, vct = _compress(kvc, l, *cmp_consts)
        yac = _attn(qt, kk, vt, kc, vct, gt, l, sinks, gac, ovt, tables)
        x = _ffn(x, n2, *ffn2, layer=l, mix=(yac, ybd, wo))
    return x
```
